```python
import jax, jax.numpy as jnp
from jax import lax
import numpy as np


D_MODEL = 2048
BATCH = 4
SEQ = 4096
DEPTH = 1

GRID_W = 64
PLE_DIM = 256
NA_HEADS = 8
NA_HEAD_DIM = 128
NA_WIN_H = 8
NA_WIN_W = 16
NA_WIDTH = NA_HEADS * NA_HEAD_DIM
MLA_HEADS = 8
MLA_NOPE_DIM = 128
MLA_ROPE_DIM = 64
MLA_V_DIM = 128
MLA_QK_DIM = MLA_NOPE_DIM + MLA_ROPE_DIM
Q_LORA_RANK = 512
KV_LORA_RANK = 512
MLA_WIDTH = MLA_HEADS * MLA_V_DIM
ROPE_THETA = 10000.0
Q_BLOCK = 128
MIX_WIDTH = NA_WIDTH + MLA_WIDTH
IN_WIDTH = 3 * NA_WIDTH + Q_LORA_RANK + KV_LORA_RANK + MLA_ROPE_DIM
N_GROUPS = 8
EXPERTS_PER_GROUP = 8
N_EXPERTS = N_GROUPS * EXPERTS_PER_GROUP
TOP_K_EXPERT = 2
D_EXPERT = 512
MOE_BLOCK = 256
ALPHA = (2 * DEPTH) ** 0.25
BETA = (8 * DEPTH) ** -0.25
LN_EPS = 1e-5
RMS_EPS = 1e-6

kernel_name = 'hybrid_na_mla_hmoe_block'


def layer_norm(x, g, b):
    xf = x.astype(jnp.float32)
    mu = jnp.mean(xf, axis=-1, keepdims=True)
    var = jnp.mean(jnp.square(xf - mu), axis=-1, keepdims=True)
    y = (xf - mu) * lax.rsqrt(var + LN_EPS)
    return (y * g.astype(jnp.float32) + b.astype(jnp.float32)).astype(x.dtype)


def rms_norm(x, g):
    xf = x.astype(jnp.float32)
    y = xf * lax.rsqrt(jnp.mean(jnp.square(xf), axis=-1, keepdims=True) + RMS_EPS)
    return (y * g.astype(jnp.float32)).astype(x.dtype)


def rope_tables(positions, dtype):
    inv_freq = 1.0 / (ROPE_THETA ** (jnp.arange(0, MLA_ROPE_DIM, 2, dtype=jnp.float32) / MLA_ROPE_DIM))
    ang = positions.astype(jnp.float32)[..., None] * inv_freq
    return jnp.cos(ang).astype(dtype), jnp.sin(ang).astype(dtype)


def apply_rope(x, cos, sin):
    x1, x2 = jnp.split(x, 2, axis=-1)
    return jnp.concatenate([x1 * cos - x2 * sin, x1 * sin + x2 * cos], axis=-1)


def neighbourhood_attention(q, k, v, rpb):
    B, S, H, Dh = q.shape
    rows = S // GRID_W
    kh = min(NA_WIN_H, rows)
    kw = NA_WIN_W
    scale = Dh ** -0.5
    qg = q.reshape(B, rows, GRID_W, H, Dh)
    kg = k.reshape(B, rows, GRID_W, H, Dh)
    vg = v.reshape(B, rows, GRID_W, H, Dh)
    col = jnp.arange(GRID_W)
    c0 = jnp.clip(col - kw // 2, 0, GRID_W - kw)
    col_idx = c0[:, None] + jnp.arange(kw)[None, :]
    col_off = col_idx - col[:, None] + (NA_WIN_W - 1)
    rpb_c = rpb[:, :, col_off]
    r0 = jnp.clip(jnp.arange(rows) - kh // 2, 0, rows - kh)

    def one_row(r):
        start = r0[r]
        q_r = lax.dynamic_index_in_dim(qg, r, axis=1, keepdims=False)
        k_rows = lax.dynamic_slice_in_dim(kg, start, kh, axis=1)
        v_rows = lax.dynamic_slice_in_dim(vg, start, kh, axis=1)
        k_win = k_rows[:, :, col_idx]
        v_win = v_rows[:, :, col_idx]
        row_off = start + jnp.arange(kh) - r + (NA_WIN_H - 1)
        bias = rpb_c[:, row_off].transpose(0, 2, 1, 3)
        s = jnp.einsum('bqhd,brqwhd->bhqrw', q_r, k_win).astype(jnp.float32) * scale
        s = s + bias.astype(jnp.float32)[None]
        pr = jax.nn.softmax(s.reshape(B, H, GRID_W, kh * kw), axis=-1)
        pr = pr.reshape(s.shape).astype(v.dtype)
        return jnp.einsum('bhqrw,brqwhd->bqhd', pr, v_win)

    out = lax.map(one_row, jnp.arange(rows))
    return out.transpose(1, 0, 2, 3, 4).reshape(B, S, H * Dh)


def dense_attention_blocked(q, k, v, scale):
    B, S, H, Dq = q.shape
    Dv = v.shape[-1]
    qb = q.reshape(B, S // Q_BLOCK, Q_BLOCK, H, Dq).transpose(1, 0, 2, 3, 4)

    def block(qi):
        s = jnp.einsum('bqhd,bkhd->bhqk', qi, k).astype(jnp.float32) * scale
        pr = jax.nn.softmax(s, axis=-1).astype(v.dtype)
        return jnp.einsum('bhqk,bkhd->bqhd', pr, v)

    o = lax.map(block, qb)
    return o.transpose(1, 0, 2, 3, 4).reshape(B, S, H * Dv)


def token_mixer(h, cos, sin, w_in, rpb, q_norm_g, kv_norm_g, w_uq, w_uk, w_uv, w_o):
    B, S, _ = h.shape
    proj = jnp.einsum('bsd,df->bsf', h, w_in)
    s1 = NA_WIDTH
    s2 = 2 * NA_WIDTH
    s3 = 3 * NA_WIDTH
    s4 = s3 + Q_LORA_RANK
    s5 = s4 + KV_LORA_RANK
    q_na = proj[..., :s1].reshape(B, S, NA_HEADS, NA_HEAD_DIM)
    k_na = proj[..., s1:s2].reshape(B, S, NA_HEADS, NA_HEAD_DIM)
    v_na = proj[..., s2:s3].reshape(B, S, NA_HEADS, NA_HEAD_DIM)
    c_q = proj[..., s3:s4]
    c_kv = proj[..., s4:s5]
    k_r = proj[..., s5:]
    o_na = neighbourhood_attention(q_na, k_na, v_na, rpb)
    q = jnp.einsum('bsr,rf->bsf', rms_norm(c_q, q_norm_g), w_uq).reshape(B, S, MLA_HEADS, MLA_QK_DIM)
    q_nope, q_pe = q[..., :MLA_NOPE_DIM], q[..., MLA_NOPE_DIM:]
    q_pe = apply_rope(q_pe, cos[:, :, None, :], sin[:, :, None, :])
    c_kv = rms_norm(c_kv, kv_norm_g)
    k_nope = jnp.einsum('bsr,rf->bsf', c_kv, w_uk).reshape(B, S, MLA_HEADS, MLA_NOPE_DIM)
    v = jnp.einsum('bsr,rf->bsf', c_kv, w_uv).reshape(B, S, MLA_HEADS, MLA_V_DIM)
    k_pe = apply_rope(k_r, cos, sin)
    k_pe = jnp.broadcast_to(k_pe[:, :, None, :], (B, S, MLA_HEADS, MLA_ROPE_DIM))
    q_full = jnp.concatenate([q_nope, q_pe], axis=-1)
    k_full = jnp.concatenate([k_nope, k_pe], axis=-1)
    o_mla = dense_attention_blocked(q_full, k_full, v, MLA_QK_DIM ** -0.5)
    o = jnp.concatenate([o_na, o_mla], axis=-1)
    return jnp.einsum('bsf,fd->bsd', o, w_o)


def hier_moe(h, w_group, b_group, w_router, b_router, w_gate, w_up, w_down):
    B, S, D = h.shape
    N = B * S
    t = h.reshape(N, D)
    g_prob = jax.nn.softmax(jnp.einsum('nd,dg->ng', t, w_group).astype(jnp.float32) + b_group.astype(jnp.float32), axis=-1)
    g_val, g_idx = lax.top_k(g_prob, 1)
    e_logits = jnp.einsum('nd,de->ne', t, w_router).astype(jnp.float32).reshape(N, N_GROUPS, EXPERTS_PER_GROUP)
    e_logits = e_logits + b_router.astype(jnp.float32)
    e_sel = jnp.take_along_axis(e_logits, g_idx[:, :, None], axis=1)[:, 0]
    e_prob = jax.nn.softmax(e_sel, axis=-1)
    e_val, e_idx = lax.top_k(e_prob, TOP_K_EXPERT)
    e_val = e_val / jnp.sum(e_val, axis=-1, keepdims=True)
    gate = g_val * e_val
    eid = g_idx * EXPERTS_PER_GROUP + e_idx
    M = N * TOP_K_EXPERT
    flat_eid = eid.reshape(-1)
    flat_tok = jnp.repeat(jnp.arange(N), TOP_K_EXPERT)
    flat_w = gate.reshape(-1)
    order = jnp.argsort(flat_eid)
    s_eid = flat_eid[order]
    s_tok = flat_tok[order]
    s_w = flat_w[order]
    counts = jnp.bincount(flat_eid, length=N_EXPERTS)
    starts = jnp.cumsum(counts) - counts
    padded = ((counts + MOE_BLOCK - 1) // MOE_BLOCK) * MOE_BLOCK
    pad_ends = jnp.cumsum(padded)
    pad_starts = pad_ends - padded
    pos = pad_starts[s_eid] + (jnp.arange(M) - starts[s_eid])
    n_blocks = (M + N_EXPERTS * (MOE_BLOCK - 1) + MOE_BLOCK - 1) // MOE_BLOCK
    P = n_blocks * MOE_BLOCK
    xs = jnp.zeros((P, D), t.dtype).at[pos].set(t[s_tok])
    block_e = jnp.clip(jnp.searchsorted(pad_ends, jnp.arange(n_blocks) * MOE_BLOCK, side='right'), 0, N_EXPERTS - 1)

    def expert_block(args):
        xb, e = args
        hdn = jax.nn.silu(xb @ w_gate[e]) * (xb @ w_up[e])
        return hdn @ w_down[e]

    ys = lax.map(expert_block, (xs.reshape(n_blocks, MOE_BLOCK, D), block_e)).reshape(P, D)
    y = jnp.zeros((N, D), t.dtype).at[s_tok].add(ys[pos] * s_w[:, None].astype(t.dtype))
    return y.reshape(B, S, D)


def per_layer_embedding(h, p_i, w_ple, w_ple_gate, ple_norm_g):
    gate = jax.nn.sigmoid(jnp.einsum('bsd,de->bse', h, w_ple_gate).astype(jnp.float32))
    e = jnp.einsum('bsq,qd->bsd', p_i, w_ple)
    return rms_norm(e * gate.astype(e.dtype), ple_norm_g)


def setup_inputs(seed: int = 0) -> dict:
    key = jax.random.key(seed)
    ks = jax.random.split(key, 24)

    def nrm(k, shape, scale):
        return jax.random.normal(k, shape, jnp.float32) * scale

    D = D_MODEL
    x = nrm(ks[0], (BATCH, SEQ, D), 1.0)
    p = nrm(ks[1], (DEPTH, BATCH, SEQ, PLE_DIM), 1.0)
    positions = jnp.tile(jnp.arange(SEQ, dtype=jnp.int32)[None, :], (BATCH, 1))
    col_scale = jnp.ones((IN_WIDTH,), jnp.float32).at[2 * NA_WIDTH:3 * NA_WIDTH].set(BETA)
    w_in = nrm(ks[2], (DEPTH, D, IN_WIDTH), D ** -0.5) * col_scale
    rpb = nrm(ks[3], (DEPTH, NA_HEADS, 2 * NA_WIN_H - 1, 2 * NA_WIN_W - 1), 0.02)
    q_norm_g = 1.0 + nrm(ks[4], (DEPTH, Q_LORA_RANK), 0.02)
    kv_norm_g = 1.0 + nrm(ks[5], (DEPTH, KV_LORA_RANK), 0.02)
    w_uq = nrm(ks[6], (DEPTH, Q_LORA_RANK, MLA_HEADS * MLA_QK_DIM), Q_LORA_RANK ** -0.5)
    w_uk = nrm(ks[7], (DEPTH, KV_LORA_RANK, MLA_HEADS * MLA_NOPE_DIM), KV_LORA_RANK ** -0.5)
    w_uv = nrm(ks[8], (DEPTH, KV_LORA_RANK, MLA_HEADS * MLA_V_DIM), KV_LORA_RANK ** -0.5 * BETA)
    w_o = nrm(ks[9], (DEPTH, MIX_WIDTH, D), MIX_WIDTH ** -0.5 * BETA)
    ln1_g = 1.0 + nrm(ks[10], (DEPTH, D), 0.02)
    ln1_b = nrm(ks[11], (DEPTH, D), 0.02)
    w_group = nrm(ks[12], (DEPTH, D, N_GROUPS), D ** -0.5)
    b_group = nrm(ks[13], (DEPTH, N_GROUPS), 0.01)
    w_router = nrm(ks[14], (DEPTH, D, N_EXPERTS), D ** -0.5)
    b_router = nrm(ks[15], (DEPTH, N_GROUPS, EXPERTS_PER_GROUP), 0.01)
    w_gate = nrm(ks[16], (DEPTH, N_EXPERTS, D, D_EXPERT), D ** -0.5)
    w_up = nrm(ks[17], (DEPTH, N_EXPERTS, D, D_EXPERT), D ** -0.5 * BETA)
    w_down = nrm(ks[18], (DEPTH, N_EXPERTS, D_EXPERT, D), D_EXPERT ** -0.5 * BETA)
    ln2_g = 1.0 + nrm(ks[19], (DEPTH, D), 0.02)
    ln2_b = nrm(ks[20], (DEPTH, D), 0.02)
    w_ple = nrm(ks[21], (DEPTH, PLE_DIM, D), PLE_DIM ** -0.5)
    w_ple_gate = nrm(ks[22], (DEPTH, D, D), D ** -0.5)
    ple_norm_g = 1.0 + nrm(ks[23], (DEPTH, D), 0.02)
    return {'x': x, 'p': p, 'positions': positions, 'w_in': w_in, 'rpb': rpb,
            'q_norm_g': q_norm_g, 'kv_norm_g': kv_norm_g, 'w_uq': w_uq, 'w_uk': w_uk,
            'w_uv': w_uv, 'w_o': w_o, 'ln1_g': ln1_g, 'ln1_b': ln1_b,
            'w_group': w_group, 'b_group': b_group, 'w_router': w_router, 'b_router': b_router,
            'w_gate': w_gate, 'w_up': w_up, 'w_down': w_down, 'ln2_g': ln2_g, 'ln2_b': ln2_b,
            'w_ple': w_ple, 'w_ple_gate': w_ple_gate, 'ple_norm_g': ple_norm_g}


def reference(x, p, positions, w_in, rpb, q_norm_g, kv_norm_g, w_uq, w_uk, w_uv, w_o,
              ln1_g, ln1_b, w_group, b_group, w_router, b_router, w_gate, w_up, w_down,
              ln2_g, ln2_b, w_ple, w_ple_gate, ple_norm_g):
    cos, sin = rope_tables(positions, x.dtype)
    for i in range(DEPTH):
        mix = token_mixer(x, cos, sin, w_in[i], rpb[i], q_norm_g[i], kv_norm_g[i],
                          w_uq[i], w_uk[i], w_uv[i], w_o[i])
        x = layer_norm(ALPHA * x + mix, ln1_g[i], ln1_b[i])
        ffn = hier_moe(x, w_group[i], b_group[i], w_router[i], b_router[i],
                       w_gate[i], w_up[i], w_down[i])
        x = layer_norm(ALPHA * x + ffn, ln2_g[i], ln2_b[i])
        x = x + per_layer_embedding(x, p[i], w_ple[i], w_ple_gate[i], ple_norm_g[i])
    return x
```

```python
import functools

import numpy as np
import jax
import jax.numpy as jnp
from jax import lax
from jax.experimental import pallas as pl
from jax.experimental.pallas import tpu as pltpu

D_MODEL = 2048
DEPTH = 1
GRID_W = 64
PLE_DIM = 256
NA_HEADS = 8
NA_HEAD_DIM = 128
NA_WIN_H = 8
NA_WIN_W = 16
NA_WIDTH = NA_HEADS * NA_HEAD_DIM
MLA_HEADS = 8
MLA_NOPE_DIM = 128
MLA_ROPE_DIM = 64
MLA_V_DIM = 128
MLA_QK_DIM = MLA_NOPE_DIM + MLA_ROPE_DIM
Q_LORA_RANK = 512
KV_LORA_RANK = 512
ROPE_THETA = 10000.0
N_GROUPS = 8
EXPERTS_PER_GROUP = 8
N_EXPERTS = N_GROUPS * EXPERTS_PER_GROUP
TOP_K_EXPERT = 2
D_EXPERT = 512
ALPHA = (2 * DEPTH) ** 0.25
LN_EPS = 1e-5
RMS_EPS = 1e-6

LANES = 128
VMEM_LIMIT_BYTES = 56 * 1024 * 1024

PROJ_TM = 256
NA_ROWS = 4
NA_QB = NA_ROWS * GRID_W
NA_WIN_ROWS = NA_ROWS + NA_WIN_H
NA_KB = NA_WIN_ROWS * GRID_W
MLA_TQ = 512
MIX_TM = 256
MOE_TB = 256
DISP_TM = 256
FIN_TM = 256
NEG_BIG = -1e30

BF16 = jnp.bfloat16
F32 = jnp.float32

_NT = (((1,), (1,)), ((), ()))


def _dot(a, b):
    return jnp.dot(a, b, preferred_element_type=F32)


def _dot_nt(a, b):
    return lax.dot_general(a, b, _NT, preferred_element_type=F32)


def _params(*sem):
    return pltpu.CompilerParams(dimension_semantics=sem, vmem_limit_bytes=VMEM_LIMIT_BYTES)


def _resident(shape):
    nd = len(shape)
    return pl.BlockSpec(shape, lambda *_: (0,) * nd, pipeline_mode=pl.Buffered(1))


def _pack_bf16_pairs(v):
    c = v.shape[1] // 2
    lo = pltpu.bitcast(v[:, :c].astype(BF16).astype(F32), jnp.uint32) >> 16
    hi = pltpu.bitcast(v[:, c:].astype(BF16).astype(F32), jnp.uint32) & jnp.uint32(0xFFFF0000)
    return lo | hi


def _unpack_bf16_pairs(u):
    lo = pltpu.bitcast(u << 16, F32)
    hi = pltpu.bitcast(u & jnp.uint32(0xFFFF0000), F32)
    return lo, hi


def _proj_kernel(x_ref, t_ref, w_ref, wvt_ref, wuq_ref, wuk_ref, wuvt_ref, gq_ref, gkv_ref,
                 qna_ref, kna_ref, vnat_ref, qm_ref, km_ref, vmt_ref):
    xb = x_ref[0].astype(BF16)
    na_scale = NA_HEAD_DIM ** -0.5
    mla_scale = MLA_QK_DIM ** -0.5
    q = _dot(xb, w_ref[:, 0:NA_WIDTH]) * na_scale
    k = _dot(xb, w_ref[:, NA_WIDTH:2 * NA_WIDTH])
    vt = _dot_nt(wvt_ref[...], xb)
    for h in range(NA_HEADS):
        sl = slice(h * NA_HEAD_DIM, (h + 1) * NA_HEAD_DIM)
        qna_ref[0, h] = q[:, sl].astype(BF16)
        kna_ref[0, h] = k[:, sl].astype(BF16)
        vnat_ref[0, h] = vt[sl, :].astype(BF16)

    c0 = 2 * NA_WIDTH
    cq = _dot(xb, w_ref[:, c0:c0 + Q_LORA_RANK])
    ckv = _dot(xb, w_ref[:, c0 + Q_LORA_RANK:c0 + Q_LORA_RANK + KV_LORA_RANK])
    kr4 = _dot(xb, w_ref[:, c0 + Q_LORA_RANK + KV_LORA_RANK:])

    def rms(c, g):
        return c * lax.rsqrt(jnp.mean(c * c, axis=-1, keepdims=True) + RMS_EPS) * g

    cqn = rms(cq, gq_ref[...]).astype(BF16)
    ckvn = rms(ckv, gkv_ref[...]).astype(BF16)

    t = t_ref[0]
    e = kr4 * t
    lane = lax.broadcasted_iota(jnp.int32, (1, LANES), 1)
    sign = jnp.where((lane // 32) % 2 == 0, -1.0, 1.0).astype(F32)
    kpe = (e + sign * pltpu.roll(e, 64, 1)).astype(BF16)

    qf = _dot(cqn, wuq_ref[...])
    kn = _dot(ckvn, wuk_ref[...])
    vmt = _dot_nt(wuvt_ref[...], ckvn)
    ts = t * mla_scale
    for h in range(MLA_HEADS):
        qm_ref[0, h, :, 0:128] = (qf[:, h * 256:h * 256 + 128] * mla_scale).astype(BF16)
        qm_ref[0, h, :, 128:256] = (qf[:, h * 256 + 128:(h + 1) * 256] * ts).astype(BF16)
        km_ref[0, h, :, 0:128] = kn[:, h * 128:(h + 1) * 128].astype(BF16)
        km_ref[0, h, :, 128:256] = kpe
        vmt_ref[0, h] = vmt[h * 128:(h + 1) * 128, :].astype(BF16)


def _proj(x, rope_t, w_main, wvt, wuq, wuk, wuvt, gq, gkv):
    B, S, D = x.shape
    tm = PROJ_TM
    hm = lambda b, i: (b, 0, i, 0)
    hmt = lambda b, i: (b, 0, 0, i)
    out_shape = (
        jax.ShapeDtypeStruct((B, NA_HEADS, S, NA_HEAD_DIM), BF16),
        jax.ShapeDtypeStruct((B, NA_HEADS, S, NA_HEAD_DIM), BF16),
        jax.ShapeDtypeStruct((B, NA_HEADS, NA_HEAD_DIM, S), BF16),
        jax.ShapeDtypeStruct((B, MLA_HEADS, S, 256), BF16),
        jax.ShapeDtypeStruct((B, MLA_HEADS, S, 256), BF16),
        jax.ShapeDtypeStruct((B, MLA_HEADS, MLA_V_DIM, S), BF16),
    )
    return pl.pallas_call(
        _proj_kernel,
        out_shape=out_shape,
        grid=(B, S // tm),
        in_specs=[
            pl.BlockSpec((1, tm, D), lambda b, i: (b, i, 0)),
            pl.BlockSpec((1, tm, LANES), lambda b, i: (b, i, 0)),
            _resident(w_main.shape), _resident(wvt.shape), _resident(wuq.shape),
            _resident(wuk.shape), _resident(wuvt.shape), _resident(gq.shape), _resident(gkv.shape),
        ],
        out_specs=(
            pl.BlockSpec((1, NA_HEADS, tm, NA_HEAD_DIM), hm),
            pl.BlockSpec((1, NA_HEADS, tm, NA_HEAD_DIM), hm),
            pl.BlockSpec((1, NA_HEADS, NA_HEAD_DIM, tm), hmt),
            pl.BlockSpec((1, MLA_HEADS, tm, 256), hm),
            pl.BlockSpec((1, MLA_HEADS, tm, 256), hm),
            pl.BlockSpec((1, MLA_HEADS, MLA_V_DIM, tm), hmt),
        ),
        compiler_params=_params("arbitrary", "arbitrary"),
        name="proj",
    )(x, rope_t, w_main, wvt, wuq, wuk, wuvt, gq, gkv)


def _na_bias_table(rpb):
    rows = GRID_W
    n_blocks = rows // NA_ROWS
    kk = np.arange(NA_KB)
    qq = np.arange(NA_QB)
    kr, kc = kk // GRID_W, kk % GRID_W
    qr, qc = qq // GRID_W, qq % GRID_W
    c0 = np.clip(qc - NA_WIN_W // 2, 0, GRID_W - NA_WIN_W)
    col_ok = (kc[:, None] >= c0[None, :]) & (kc[:, None] < c0[None, :] + NA_WIN_W)
    col_off = np.clip(kc[:, None] - qc[None, :] + (NA_WIN_W - 1), 0, 2 * NA_WIN_W - 2)
    tabs = []
    for rb in (0, 1, n_blocks - 1):
        w0 = int(np.clip(NA_ROWS * rb - NA_WIN_H // 2, 0, rows - NA_WIN_ROWS))
        r = NA_ROWS * rb + qr
        r0 = np.clip(r - NA_WIN_H // 2, 0, rows - NA_WIN_H)
        krow = w0 + kr
        row_ok = (krow[:, None] >= r0[None, :]) & (krow[:, None] < r0[None, :] + NA_WIN_H)
        row_off = np.clip(krow[:, None] - r[None, :] + (NA_WIN_H - 1), 0, 2 * NA_WIN_H - 2)
        bias = rpb[:, row_off, col_off].astype(F32)
        tabs.append(jnp.where((row_ok & col_ok)[None], bias, NEG_BIG))
    return jnp.stack(tabs, axis=1)


def _na_kernel(q_ref, k_ref, vt_ref, tab_ref, o_ref):
    n_blocks = q_ref.shape[2] // NA_QB
    rows = n_blocks * NA_ROWS

    def body(rb, carry):
        w0 = jnp.clip(NA_ROWS * rb - NA_WIN_H // 2, 0, rows - NA_WIN_ROWS)
        tok0 = pl.multiple_of(w0 * GRID_W, NA_QB)
        q0 = pl.multiple_of(rb * NA_QB, NA_QB)
        cls = jnp.where(rb == 0, 0, jnp.where(rb == n_blocks - 1, 2, 1))
        qb = q_ref[0, 0, pl.ds(q0, NA_QB), :]
        kw = k_ref[0, 0, pl.ds(tok0, NA_KB), :]
        st = _dot_nt(kw, qb) + tab_ref[0, cls]
        m = jnp.max(st, axis=0, keepdims=True)
        p = jnp.exp(st - m)
        l = jnp.sum(p, axis=0, keepdims=True)
        vw = vt_ref[0, 0, :, pl.ds(tok0, NA_KB)]
        ot = _dot(vw, p.astype(BF16)) / l
        o_ref[0, pl.ds(q0, NA_QB), :] = ot.T.astype(o_ref.dtype)
        return carry

    lax.fori_loop(0, n_blocks, body, 0)


def _na(q, k, vt, tab):
    B, H, S, Dh = q.shape
    return pl.pallas_call(
        _na_kernel,
        out_shape=jax.ShapeDtypeStruct((B, S, H * Dh), BF16),
        grid=(H, B),
        in_specs=[
            pl.BlockSpec((1, 1, S, Dh), lambda h, b: (b, h, 0, 0)),
            pl.BlockSpec((1, 1, S, Dh), lambda h, b: (b, h, 0, 0)),
            pl.BlockSpec((1, 1, Dh, S), lambda h, b: (b, h, 0, 0)),
            pl.BlockSpec((1, 3, NA_KB, NA_QB), lambda h, b: (h, 0, 0, 0)),
        ],
        out_specs=pl.BlockSpec((1, S, Dh), lambda h, b: (b, 0, h)),
        compiler_params=_params("arbitrary", "arbitrary"),
        name="na",
    )(q, k, vt, tab)


def _mla_kernel(q_ref, k_ref, vt_ref, o_ref):
    st = _dot_nt(k_ref[0, 0], q_ref[0, 0])
    m = jnp.max(st, axis=0, keepdims=True)
    p = jnp.exp(st - m)
    l = jnp.sum(p, axis=0, keepdims=True)
    ot = _dot(vt_ref[0, 0], p.astype(BF16)) / l
    o_ref[0] = ot.T.astype(o_ref.dtype)


def _mla(q, k, vt):
    B, H, S, Dq = q.shape
    Dv = vt.shape[2]
    tq = MLA_TQ
    return pl.pallas_call(
        _mla_kernel,
        out_shape=jax.ShapeDtypeStruct((B, S, H * Dv), BF16),
        grid=(B, H, S // tq),
        in_specs=[
            pl.BlockSpec((1, 1, tq, Dq), lambda b, h, i: (b, h, i, 0)),
            pl.BlockSpec((1, 1, S, Dq), lambda b, h, i: (b, h, 0, 0)),
            pl.BlockSpec((1, 1, Dv, S), lambda b, h, i: (b, h, 0, 0)),
        ],
        out_specs=pl.BlockSpec((1, tq, Dv), lambda b, h, i: (b, i, h)),
        compiler_params=_params("arbitrary", "arbitrary", "arbitrary"),
        name="mla",
    )(q, k, vt)


def _layer_norm(z, g, b):
    mu = jnp.mean(z, axis=-1, keepdims=True)
    zc = z - mu
    var = jnp.mean(zc * zc, axis=-1, keepdims=True)
    return zc * lax.rsqrt(var + LN_EPS) * g + b


def _first_lane_where(cond, lane):
    return jnp.min(jnp.where(cond, lane, LANES), axis=-1, keepdims=True)


def _mix_kernel(ona_ref, omla_ref, x_ref, wo_ref, g_ref, b_ref, wr_ref, br_ref,
                x1_ref, x1u_ref, ri_ref, gate_ref, cnt_ref, carry_ref):
    step = pl.program_id(0)

    @pl.when(step == 0)
    def _():
        carry_ref[...] = jnp.zeros_like(carry_ref)

    mix = _dot(ona_ref[...], wo_ref[0:NA_WIDTH, :]) + _dot(omla_ref[...], wo_ref[NA_WIDTH:, :])
    x1 = _layer_norm(ALPHA * x_ref[...] + mix, g_ref[...], b_ref[...])
    x1_ref[...] = x1
    x1u_ref[...] = _pack_bf16_pairs(x1)

    tm = x1.shape[0]
    logit = _dot(x1.astype(BF16), wr_ref[...]) + br_ref[...]
    lane = lax.broadcasted_iota(jnp.int32, (tm, LANES), 1)
    g_mask = (lane >= N_EXPERTS) & (lane < N_EXPERTS + N_GROUPS)
    gl = jnp.where(g_mask, logit, NEG_BIG)
    gmax = jnp.max(gl, axis=-1, keepdims=True)
    gex = jnp.where(g_mask, jnp.exp(gl - gmax), 0.0)
    gprob = gex / jnp.sum(gex, axis=-1, keepdims=True)
    g_val = jnp.max(gprob, axis=-1, keepdims=True)
    g_idx = _first_lane_where(g_mask & (gprob == g_val), lane) - N_EXPERTS

    e_mask = (lane < N_EXPERTS) & ((lane // EXPERTS_PER_GROUP) == g_idx)
    el = jnp.where(e_mask, logit, NEG_BIG)
    emax = jnp.max(el, axis=-1, keepdims=True)
    eex = jnp.where(e_mask, jnp.exp(el - emax), 0.0)
    eprob = jnp.where(e_mask, eex / jnp.sum(eex, axis=-1, keepdims=True), -1.0)
    v1 = jnp.max(eprob, axis=-1, keepdims=True)
    i1 = _first_lane_where(eprob == v1, lane)
    eprob2 = jnp.where(lane == i1, -1.0, eprob)
    v2 = jnp.max(eprob2, axis=-1, keepdims=True)
    i2 = _first_lane_where(eprob2 == v2, lane)
    vsum = v1 + v2
    w1 = g_val * (v1 / vsum)
    w2 = g_val * (v2 / vsum)

    oh1 = lane == i1
    oh2 = lane == i2
    oh = jnp.where(oh1 | oh2, 1.0, 0.0)
    r_i = lax.broadcasted_iota(jnp.int32, (tm, tm), 0)
    c_i = lax.broadcasted_iota(jnp.int32, (tm, tm), 1)
    tri = jnp.where(c_i < r_i, 1.0, 0.0).astype(BF16)
    before = _dot(tri, oh.astype(BF16)) + carry_ref[...]
    rank1 = jnp.sum(jnp.where(oh1, before, 0.0), axis=-1, keepdims=True).astype(jnp.int32)
    rank2 = jnp.sum(jnp.where(oh2, before, 0.0), axis=-1, keepdims=True).astype(jnp.int32)
    carry_ref[...] = carry_ref[...] + jnp.sum(oh, axis=0, keepdims=True)
    cnt_ref[...] = carry_ref[...]

    ri_ref[...] = jnp.where(lane == 0, i1, jnp.where(lane == 1, i2,
                            jnp.where(lane == 2, rank1, jnp.where(lane == 3, rank2, 0))))
    gate_ref[...] = jnp.where(lane == 0, w1, jnp.where(lane == 1, w2, 0.0))


def _mix_out(o_na, o_mla, x, wo, g, b, wr, br):
    N, D = x.shape
    tm = MIX_TM
    row = lambda i: (i, 0)
    return pl.pallas_call(
        _mix_kernel,
        out_shape=(
            jax.ShapeDtypeStruct((N, D), F32),
            jax.ShapeDtypeStruct((N, D // 2), jnp.uint32),
            jax.ShapeDtypeStruct((N, LANES), jnp.int32),
            jax.ShapeDtypeStruct((N, LANES), F32),
            jax.ShapeDtypeStruct((1, LANES), F32),
        ),
        grid=(N // tm,),
        in_specs=[
            pl.BlockSpec((tm, NA_WIDTH), row),
            pl.BlockSpec((tm, NA_WIDTH), row),
            pl.BlockSpec((tm, D), row),
            _resident(wo.shape), _resident(g.shape), _resident(b.shape),
            _resident(wr.shape), _resident(br.shape),
        ],
        out_specs=(
            pl.BlockSpec((tm, D), row),
            pl.BlockSpec((tm, D // 2), row),
            pl.BlockSpec((tm, LANES), row),
            pl.BlockSpec((tm, LANES), row),
            pl.BlockSpec((1, LANES), lambda i: (0, 0)),
        ),
        scratch_shapes=[pltpu.VMEM((1, LANES), F32)],
        compiler_params=_params("arbitrary"),
        name="mix_out",
    )(o_na, o_mla, x, wo, g, b, wr, br)


def _row_copy(src, src_row, dst, dst_row, sem):
    return pltpu.make_async_copy(src.at[pl.ds(src_row, 1), :], dst.at[pl.ds(dst_row, 1), :], sem)


def _dispatch_kernel(pos_ref, x_ref, xs_in_ref, xs_ref, sem):
    del xs_in_ref
    tm = x_ref.shape[0]
    base = pl.program_id(0) * (tm * TOP_K_EXPERT)

    def issue(j, carry):
        for kk in range(TOP_K_EXPERT):
            _row_copy(x_ref, j, xs_ref, pos_ref[base + j * TOP_K_EXPERT + kk], sem).start()
        return carry

    lax.fori_loop(0, tm, issue, 0, unroll=8)

    def drain(j, carry):
        for kk in range(TOP_K_EXPERT):
            _row_copy(x_ref, 0, xs_ref, 0, sem).wait()
        return carry

    lax.fori_loop(0, tm, drain, 0, unroll=8)


def _dispatch(pos, x1u, n_rows):
    N, C = x1u.shape
    tm = DISP_TM
    xs0 = jnp.zeros((n_rows, C), x1u.dtype)
    return pl.pallas_call(
        _dispatch_kernel,
        out_shape=jax.ShapeDtypeStruct((n_rows, C), x1u.dtype),
        grid_spec=pltpu.PrefetchScalarGridSpec(
            num_scalar_prefetch=1,
            grid=(N // tm,),
            in_specs=[
                pl.BlockSpec((tm, C), lambda i, pos: (i, 0)),
                pl.BlockSpec(memory_space=pl.ANY),
            ],
            out_specs=pl.BlockSpec(memory_space=pl.ANY),
            scratch_shapes=[pltpu.SemaphoreType.DMA],
        ),
        input_output_aliases={2: 0},
        compiler_params=_params("arbitrary"),
        name="dispatch",
    )(pos, x1u, xs0)


def _experts_kernel(be_ref, nb_ref, xs_ref, wg_ref, wu_ref, wd_ref, ys_ref, wg_s, wu_s, wd_s):
    i = pl.program_id(0)
    active = i < nb_ref[0]
    prev = be_ref[jnp.maximum(i - 1, 0)]
    fresh = active & ((i == 0) | (be_ref[i] != prev))

    @pl.when(fresh)
    def _():
        wg_s[...] = wg_ref[0].astype(BF16)
        wu_s[...] = wu_ref[0].astype(BF16)
        wd_s[...] = wd_ref[0].astype(BF16)

    @pl.when(active)
    def _():
        lo, hi = _unpack_bf16_pairs(xs_ref[...])
        c = lo.shape[1]
        lo = lo.astype(BF16)
        hi = hi.astype(BF16)
        gp = _dot(lo, wg_s[0:c, :]) + _dot(hi, wg_s[c:, :])
        up = _dot(lo, wu_s[0:c, :]) + _dot(hi, wu_s[c:, :])
        hdn = (gp * jax.nn.sigmoid(gp) * up).astype(BF16)
        ys_ref[...] = _pack_bf16_pairs(_dot(hdn, wd_s[...]))

    @pl.when(jnp.logical_not(active))
    def _():
        ys_ref[...] = jnp.zeros_like(ys_ref)


def _experts(block_e, n_used, xs, w_gate, w_up, w_down):
    P, C = xs.shape
    E, D, F = w_gate.shape
    tb = MOE_TB
    n_blocks = P // tb
    blk = lambda i, be, nb: (jnp.minimum(i, nb[0] - 1), 0)
    wsel = lambda i, be, nb: (be[jnp.minimum(i, nb[0] - 1)], 0, 0)
    return pl.pallas_call(
        _experts_kernel,
        out_shape=jax.ShapeDtypeStruct((P, C), jnp.uint32),
        grid_spec=pltpu.PrefetchScalarGridSpec(
            num_scalar_prefetch=2,
            grid=(n_blocks,),
            in_specs=[
                pl.BlockSpec((tb, C), blk),
                pl.BlockSpec((1, D, F), wsel),
                pl.BlockSpec((1, D, F), wsel),
                pl.BlockSpec((1, F, D), wsel),
            ],
            out_specs=pl.BlockSpec((tb, C), lambda i, be, nb: (i, 0)),
            scratch_shapes=[pltpu.VMEM((D, F), BF16), pltpu.VMEM((D, F), BF16),
                            pltpu.VMEM((F, D), BF16)],
        ),
        compiler_params=_params("arbitrary"),
        name="experts",
    )(block_e, n_used, xs, w_gate, w_up, w_down)


def _final_kernel(pos_ref, x1_ref, gate_ref, ys_ref, p_ref, g_ref, b_ref, wpg_ref, wpe_ref,
                  gp_ref, o_ref, buf, sem):
    tm = x1_ref.shape[0]
    base = pl.program_id(0) * (tm * TOP_K_EXPERT)

    def issue(j, carry):
        for kk in range(TOP_K_EXPERT):
            _row_copy(ys_ref, pos_ref[base + j * TOP_K_EXPERT + kk], buf.at[kk], j, sem).start()
        return carry

    lax.fori_loop(0, tm, issue, 0, unroll=8)

    def drain(j, carry):
        for kk in range(TOP_K_EXPERT):
            _row_copy(ys_ref, 0, buf.at[kk], 0, sem).wait()
        return carry

    lax.fori_loop(0, tm, drain, 0, unroll=8)

    gates = gate_ref[...]
    g1 = gates[:, 0:1]
    g2 = gates[:, 1:2]
    lo1, hi1 = _unpack_bf16_pairs(buf[0])
    lo2, hi2 = _unpack_bf16_pairs(buf[1])
    ffn = jnp.concatenate([lo1 * g1 + lo2 * g2, hi1 * g1 + hi2 * g2], axis=1)
    x2 = _layer_norm(ALPHA * x1_ref[...] + ffn, g_ref[...], b_ref[...])

    gate = jax.nn.sigmoid(_dot(x2.astype(BF16), wpg_ref[...]))
    e = _dot(p_ref[...].astype(BF16), wpe_ref[...])
    t = e * gate
    ple = t * lax.rsqrt(jnp.mean(t * t, axis=-1, keepdims=True) + RMS_EPS) * gp_ref[...]
    o_ref[...] = x2 + ple


def _final(pos, x1, gates, ys, p, g, b, wpg, wpe, gp):
    N, D = x1.shape
    tm = FIN_TM
    row = lambda i, pos: (i, 0)
    res = lambda shape: pl.BlockSpec(shape, lambda i, pos: (0,) * len(shape),
                                     pipeline_mode=pl.Buffered(1))
    return pl.pallas_call(
        _final_kernel,
        out_shape=jax.ShapeDtypeStruct((N, D), F32),
        grid_spec=pltpu.PrefetchScalarGridSpec(
            num_scalar_prefetch=1,
            grid=(N // tm,),
            in_specs=[
                pl.BlockSpec((tm, D), row),
                pl.BlockSpec((tm, LANES), row),
                pl.BlockSpec(memory_space=pl.ANY),
                pl.BlockSpec((tm, PLE_DIM), row),
                res(g.shape), res(b.shape), res(wpg.shape), res(wpe.shape), res(gp.shape),
            ],
            out_specs=pl.BlockSpec((tm, D), row),
            scratch_shapes=[pltpu.VMEM((TOP_K_EXPERT, tm, D // 2), jnp.uint32),
                            pltpu.SemaphoreType.DMA],
        ),
        compiler_params=_params("arbitrary"),
        name="final",
    )(pos, x1, gates, ys, p, g, b, wpg, wpe, gp)


def _rope_table(positions):
    inv_freq = 1.0 / (ROPE_THETA ** (jnp.arange(0, MLA_ROPE_DIM, 2, dtype=F32) / MLA_ROPE_DIM))
    ang = positions.astype(F32)[..., None] * inv_freq
    cos, sin = jnp.cos(ang), jnp.sin(ang)
    return jnp.concatenate([cos, cos, sin, sin], axis=-1)


def _layer(x, p_i, rope_t, w_in, rpb, q_norm_g, kv_norm_g, w_uq, w_uk, w_uv, w_o, ln1_g, ln1_b,
           w_group, b_group, w_router, b_router, w_gate, w_up, w_down, ln2_g, ln2_b,
           w_ple, w_ple_gate, ple_norm_g):
    B, S, D = x.shape
    N = B * S
    s3 = 3 * NA_WIDTH
    s5 = s3 + Q_LORA_RANK + KV_LORA_RANK
    half = MLA_ROPE_DIM // 2

    y1, y2 = w_in[:, s5:s5 + half], w_in[:, s5 + half:]
    w_main = jnp.concatenate([w_in[:, :2 * NA_WIDTH], w_in[:, s3:s5], y1, y2, y2, y1],
                             axis=1).astype(BF16)
    wvt = w_in[:, 2 * NA_WIDTH:s3].T.astype(BF16)
    uq = w_uq.reshape(Q_LORA_RANK, MLA_HEADS, MLA_QK_DIM)
    nope, x1c, x2c = (uq[..., :MLA_NOPE_DIM], uq[..., MLA_NOPE_DIM:MLA_NOPE_DIM + half],
                      uq[..., MLA_NOPE_DIM + half:])
    wuq = jnp.concatenate([nope, x1c, x2c, x2c, x1c], axis=-1).reshape(Q_LORA_RANK, -1).astype(BF16)
    wuk = w_uk.astype(BF16)
    wuvt = w_uv.T.astype(BF16)
    row = lambda v: v.reshape(1, -1).astype(F32)

    q_na, k_na, v_nat, q_m, k_m, v_mt = _proj(x, rope_t, w_main, wvt, wuq, wuk, wuvt,
                                               row(q_norm_g), row(kv_norm_g))
    o_na = _na(q_na, k_na, v_nat, _na_bias_table(rpb))
    o_mla = _mla(q_m, k_m, v_mt)

    pad = LANES - N_EXPERTS - N_GROUPS
    wr = jnp.concatenate([w_router, w_group, jnp.zeros((D, pad), F32)], axis=1).astype(BF16)
    br = jnp.concatenate([b_router.reshape(-1), b_group, jnp.zeros((pad,), F32)]).reshape(1, -1)
    x1, x1u, ri, gates, cnt = _mix_out(o_na.reshape(N, -1), o_mla.reshape(N, -1), x.reshape(N, D),
                                       w_o.astype(BF16), row(ln1_g), row(ln1_b), wr, br)

    tb = MOE_TB
    counts = cnt[0, :N_EXPERTS].astype(jnp.int32)
    padded = ((counts + tb - 1) // tb) * tb
    pad_ends = jnp.cumsum(padded)
    pad_starts = pad_ends - padded
    pos = (pad_starts[ri[:, 0:2]] + ri[:, 2:4]).reshape(-1).astype(jnp.int32)
    n_blocks = (N * TOP_K_EXPERT + N_EXPERTS * (tb - 1) + tb - 1) // tb
    block_e = jnp.clip(jnp.searchsorted(pad_ends, jnp.arange(n_blocks) * tb, side='right'),
                       0, N_EXPERTS - 1).astype(jnp.int32)
    n_used = (pad_ends[-1:] // tb).astype(jnp.int32)

    xs = _dispatch(pos, x1u, n_blocks * tb)
    ys = _experts(block_e, n_used, xs, w_gate, w_up, w_down)
    out = _final(pos, x1, gates, ys, p_i.reshape(N, -1), row(ln2_g), row(ln2_b),
                 w_ple_gate.astype(BF16), w_ple.astype(BF16), row(ple_norm_g))
    return out.reshape(B, S, D)


def kernel(x, p, positions, w_in, rpb, q_norm_g, kv_norm_g, w_uq, w_uk, w_uv, w_o, ln1_g, ln1_b,
           w_group, b_group, w_router, b_router, w_gate, w_up, w_down, ln2_g, ln2_b,
           w_ple, w_ple_gate, ple_norm_g):
    rope_t = _rope_table(positions)
    for i in range(DEPTH):
        x = _layer(x, p[i], rope_t, w_in[i], rpb[i], q_norm_g[i], kv_norm_g[i], w_uq[i], w_uk[i],
                   w_uv[i], w_o[i], ln1_g[i], ln1_b[i], w_group[i], b_group[i], w_router[i],
                   b_router[i], w_gate[i], w_up[i], w_down[i], ln2_g[i], ln2_b[i],
                   w_ple[i], w_ple_gate[i], ple_norm_g[i])
    return x
```

```python
import numpy as np
import jax
import jax.numpy as jnp
from jax import lax
from jax.experimental import pallas as pl
from jax.experimental.pallas import tpu as pltpu

D_MODEL = 2048
DEPTH = 1
GRID_W = 64
PLE_DIM = 256
NA_HEADS = 8
NA_HEAD_DIM = 128
NA_WIN_H = 8
NA_WIN_W = 16
NA_WIDTH = NA_HEADS * NA_HEAD_DIM
MLA_HEADS = 8
MLA_NOPE_DIM = 128
MLA_ROPE_DIM = 64
MLA_V_DIM = 128
MLA_QK_DIM = MLA_NOPE_DIM + MLA_ROPE_DIM
Q_LORA_RANK = 512
KV_LORA_RANK = 512
ROPE_THETA = 10000.0
N_GROUPS = 8
EXPERTS_PER_GROUP = 8
N_EXPERTS = N_GROUPS * EXPERTS_PER_GROUP
TOP_K_EXPERT = 2
D_EXPERT = 512
ALPHA = (2 * DEPTH) ** 0.25
LN_EPS = 1e-5
RMS_EPS = 1e-6

LANES = 128
SUBLANES = 8
VMEM_LIMIT_BYTES = 56 * 1024 * 1024

PROJ_TM = 256
NA_ROWS = 4
NA_QB = NA_ROWS * GRID_W
NA_WIN_ROWS = NA_ROWS + NA_WIN_H
NA_KB = NA_WIN_ROWS * GRID_W
MLA_TQ = 512
MLA_KC = 512
MIX_TM = 256
MOE_TB = 256
DISP_TM = 256
FIN_TM = 256
NEG_BIG = -1e30
LOG2E = 1.4426950408889634

BF16 = jnp.bfloat16
F32 = jnp.float32

_NT = (((1,), (1,)), ((), ()))


def _dot(a, b):
    return jnp.dot(a, b, preferred_element_type=F32)


def _dot_nt(a, b):
    return lax.dot_general(a, b, _NT, preferred_element_type=F32)


def _params(*sem):
    return pltpu.CompilerParams(dimension_semantics=sem, vmem_limit_bytes=VMEM_LIMIT_BYTES)


def _resident(shape):
    nd = len(shape)
    return pl.BlockSpec(shape, lambda *_: (0,) * nd, pipeline_mode=pl.Buffered(1))


def _proj_kernel(x_ref, t_ref, w_ref, wvt_ref, wuq_ref, wuk_ref, wuvt_ref, gq_ref, gkv_ref,
                 qna_ref, kna_ref, vnat_ref, qm_ref, km_ref, vmt_ref):
    xb = x_ref[0].astype(BF16)
    na_scale = NA_HEAD_DIM ** -0.5 * LOG2E
    mla_scale = MLA_QK_DIM ** -0.5 * LOG2E
    q = _dot(xb, w_ref[:, 0:NA_WIDTH]) * na_scale
    k = _dot(xb, w_ref[:, NA_WIDTH:2 * NA_WIDTH])
    vt = _dot_nt(wvt_ref[...], xb)
    for h in range(NA_HEADS):
        sl = slice(h * NA_HEAD_DIM, (h + 1) * NA_HEAD_DIM)
        qna_ref[0, h] = q[:, sl].astype(BF16)
        kna_ref[0, h] = k[:, sl].astype(BF16)
        vnat_ref[0, h] = vt[sl, :].astype(BF16)

    c0 = 2 * NA_WIDTH
    cq = _dot(xb, w_ref[:, c0:c0 + Q_LORA_RANK])
    ckv = _dot(xb, w_ref[:, c0 + Q_LORA_RANK:c0 + Q_LORA_RANK + KV_LORA_RANK])
    kr4 = _dot(xb, w_ref[:, c0 + Q_LORA_RANK + KV_LORA_RANK:])

    def rms(c, g):
        return c * lax.rsqrt(jnp.mean(c * c, axis=-1, keepdims=True) + RMS_EPS) * g

    cqn = rms(cq, gq_ref[...]).astype(BF16)
    ckvn = rms(ckv, gkv_ref[...]).astype(BF16)

    t = t_ref[0]
    e = kr4 * t
    lane = lax.broadcasted_iota(jnp.int32, (1, LANES), 1)
    sign = jnp.where((lane // 32) % 2 == 0, -1.0, 1.0).astype(F32)
    kpe = (e + sign * pltpu.roll(e, 64, 1)).astype(BF16)

    qf = _dot(cqn, wuq_ref[...])
    kn = _dot(ckvn, wuk_ref[...])
    vmt = _dot_nt(wuvt_ref[...], ckvn)
    ts = t * mla_scale
    for h in range(MLA_HEADS):
        qm_ref[0, h, :, 0:128] = (qf[:, h * 256:h * 256 + 128] * mla_scale).astype(BF16)
        qm_ref[0, h, :, 128:256] = (qf[:, h * 256 + 128:(h + 1) * 256] * ts).astype(BF16)
        km_ref[0, h, :, 0:128] = kn[:, h * 128:(h + 1) * 128].astype(BF16)
        km_ref[0, h, :, 128:256] = kpe
        vmt_ref[0, h] = vmt[h * 128:(h + 1) * 128, :].astype(BF16)


def _proj(x, rope_t, w_main, wvt, wuq, wuk, wuvt, gq, gkv):
    B, S, D = x.shape
    tm = PROJ_TM
    hm = lambda b, i: (b, 0, i, 0)
    hmt = lambda b, i: (b, 0, 0, i)
    out_shape = (
        jax.ShapeDtypeStruct((B, NA_HEADS, S, NA_HEAD_DIM), BF16),
        jax.ShapeDtypeStruct((B, NA_HEADS, S, NA_HEAD_DIM), BF16),
        jax.ShapeDtypeStruct((B, NA_HEADS, NA_HEAD_DIM, S), BF16),
        jax.ShapeDtypeStruct((B, MLA_HEADS, S, 256), BF16),
        jax.ShapeDtypeStruct((B, MLA_HEADS, S, 256), BF16),
        jax.ShapeDtypeStruct((B, MLA_HEADS, MLA_V_DIM, S), BF16),
    )
    return pl.pallas_call(
        _proj_kernel,
        out_shape=out_shape,
        grid=(B, S // tm),
        in_specs=[
            pl.BlockSpec((1, tm, D), lambda b, i: (b, i, 0)),
            pl.BlockSpec((1, tm, LANES), lambda b, i: (b, i, 0)),
            _resident(w_main.shape), _resident(wvt.shape), _resident(wuq.shape),
            _resident(wuk.shape), _resident(wuvt.shape), _resident(gq.shape), _resident(gkv.shape),
        ],
        out_specs=(
            pl.BlockSpec((1, NA_HEADS, tm, NA_HEAD_DIM), hm),
            pl.BlockSpec((1, NA_HEADS, tm, NA_HEAD_DIM), hm),
            pl.BlockSpec((1, NA_HEADS, NA_HEAD_DIM, tm), hmt),
            pl.BlockSpec((1, MLA_HEADS, tm, 256), hm),
            pl.BlockSpec((1, MLA_HEADS, tm, 256), hm),
            pl.BlockSpec((1, MLA_HEADS, MLA_V_DIM, tm), hmt),
        ),
        compiler_params=_params("arbitrary", "arbitrary"),
        name="proj",
    )(x, rope_t, w_main, wvt, wuq, wuk, wuvt, gq, gkv)


def _na_bias_table(rpb):
    rows = GRID_W
    n_blocks = rows // NA_ROWS
    n_ro, n_co = 2 * NA_WIN_H - 1, 2 * NA_WIN_W - 1
    kr, qr = np.arange(NA_WIN_ROWS), np.arange(NA_ROWS)
    kc, qc = np.arange(GRID_W), np.arange(GRID_W)
    c0 = np.clip(qc - NA_WIN_W // 2, 0, GRID_W - NA_WIN_W)
    col_ok = (kc[:, None] >= c0[None, :]) & (kc[:, None] < c0[None, :] + NA_WIN_W)
    col_off = np.clip(kc[:, None] - qc[None, :] + (NA_WIN_W - 1), 0, n_co - 1)
    oh_c = np.zeros((n_co, GRID_W, GRID_W), np.float32)
    oh_c[col_off, kc[:, None], qc[None, :]] = 1.0
    oh_r = np.zeros((3, n_ro, NA_WIN_ROWS, NA_ROWS), np.float32)
    row_ok = np.zeros((3, NA_WIN_ROWS, NA_ROWS), bool)
    for c, rb in enumerate((0, 1, n_blocks - 1)):
        w0 = int(np.clip(NA_ROWS * rb - NA_WIN_H // 2, 0, rows - NA_WIN_ROWS))
        r = NA_ROWS * rb + qr
        r0 = np.clip(r - NA_WIN_H // 2, 0, rows - NA_WIN_H)
        krow = w0 + kr
        row_ok[c] = (krow[:, None] >= r0[None, :]) & (krow[:, None] < r0[None, :] + NA_WIN_H)
        row_off = np.clip(krow[:, None] - r[None, :] + (NA_WIN_H - 1), 0, n_ro - 1)
        oh_r[c, row_off, kr[:, None], qr[None, :]] = 1.0
    hi = lax.Precision.HIGHEST
    a = jnp.einsum('hrc,zrkq->hzkqc', rpb.astype(F32), oh_r, precision=hi)
    b = jnp.einsum('hzkqc,cxy->hzkxqy', a, oh_c, precision=hi)
    ok = row_ok[:, :, None, :, None] & col_ok[None, None, :, None, :]
    tab = jnp.where(ok[None], b * LOG2E, NEG_BIG)
    return tab.reshape(rpb.shape[0], 3, NA_KB, NA_QB)


def _na_kernel(q_ref, k_ref, vt_ref, tab_ref, o_ref):
    n_blocks = q_ref.shape[2] // NA_QB
    rows = n_blocks * NA_ROWS

    def body(rb, carry):
        w0 = jnp.clip(NA_ROWS * rb - NA_WIN_H // 2, 0, rows - NA_WIN_ROWS)
        tok0 = pl.multiple_of(w0 * GRID_W, NA_QB)
        q0 = pl.multiple_of(rb * NA_QB, NA_QB)
        cls = jnp.where(rb == 0, 0, jnp.where(rb == n_blocks - 1, 2, 1))
        qb = q_ref[0, 0, pl.ds(q0, NA_QB), :]
        kw = k_ref[0, 0, pl.ds(tok0, NA_KB), :]
        st = _dot_nt(kw, qb) + tab_ref[0, cls]
        m = jnp.max(st, axis=0, keepdims=True)
        p = jnp.exp2(st - m)
        l = jnp.sum(p, axis=0, keepdims=True)
        vw = vt_ref[0, 0, :, pl.ds(tok0, NA_KB)]
        ot = _dot(vw, p.astype(BF16)) / l
        o_ref[0, pl.ds(q0, NA_QB), :] = ot.T.astype(o_ref.dtype)
        return carry

    lax.fori_loop(0, n_blocks, body, 0, unroll=2)


def _na(q, k, vt, tab):
    B, H, S, Dh = q.shape
    return pl.pallas_call(
        _na_kernel,
        out_shape=jax.ShapeDtypeStruct((B, S, H * Dh), BF16),
        grid=(H, B),
        in_specs=[
            pl.BlockSpec((1, 1, S, Dh), lambda h, b: (b, h, 0, 0)),
            pl.BlockSpec((1, 1, S, Dh), lambda h, b: (b, h, 0, 0)),
            pl.BlockSpec((1, 1, Dh, S), lambda h, b: (b, h, 0, 0)),
            pl.BlockSpec((1, 3, NA_KB, NA_QB), lambda h, b: (h, 0, 0, 0)),
        ],
        out_specs=pl.BlockSpec((1, S, Dh), lambda h, b: (b, 0, h)),
        compiler_params=_params("arbitrary", "arbitrary"),
        name="na",
    )(q, k, vt, tab)


def _mla_kernel(q_ref, k_ref, vt_ref, o_ref, st_ref):
    q = q_ref[0, 0]
    n_chunks = k_ref.shape[2] // MLA_KC
    m = None
    for c in range(n_chunks):
        ks = slice(c * MLA_KC, (c + 1) * MLA_KC)
        st = _dot_nt(k_ref[0, 0, ks, :], q)
        st_ref[ks, :] = st
        mc = jnp.max(st, axis=0, keepdims=True)
        m = mc if c == 0 else jnp.maximum(m, mc)
    l = acc = None
    for c in range(n_chunks):
        ks = slice(c * MLA_KC, (c + 1) * MLA_KC)
        p = jnp.exp2(st_ref[ks, :] - m)
        ps = jnp.sum(p, axis=0, keepdims=True)
        pv = _dot(vt_ref[0, 0, :, ks], p.astype(BF16))
        l = ps if c == 0 else l + ps
        acc = pv if c == 0 else acc + pv
    o_ref[0] = (acc / l).T.astype(o_ref.dtype)


def _mla(q, k, vt):
    B, H, S, Dq = q.shape
    Dv = MLA_V_DIM
    tq = MLA_TQ
    return pl.pallas_call(
        _mla_kernel,
        out_shape=jax.ShapeDtypeStruct((B, S, H * Dv), BF16),
        grid=(B, H, S // tq),
        in_specs=[
            pl.BlockSpec((1, 1, tq, Dq), lambda b, h, i: (b, h, i, 0)),
            pl.BlockSpec((1, 1, S, Dq), lambda b, h, i: (b, h, 0, 0)),
            pl.BlockSpec((1, 1, Dv, S), lambda b, h, i: (b, h, 0, 0)),
        ],
        out_specs=pl.BlockSpec((1, tq, Dv), lambda b, h, i: (b, i, h)),
        scratch_shapes=[pltpu.VMEM((S, tq), F32)],
        compiler_params=_params("arbitrary", "arbitrary", "arbitrary"),
        name="mla",
    )(q, k, vt)


def _layer_norm(z, g, b):
    mu = jnp.mean(z, axis=-1, keepdims=True)
    zc = z - mu
    var = jnp.mean(zc * zc, axis=-1, keepdims=True)
    return zc * lax.rsqrt(var + LN_EPS) * g + b


def _first_lane_where(cond, lane):
    return jnp.min(jnp.where(cond, lane, LANES), axis=-1, keepdims=True)


def _mix_kernel(ona_ref, omla_ref, x_ref, wo_ref, g_ref, b_ref, wr_ref, br_ref,
                x1_ref, ri_ref, gate_ref, cnt_ref, carry_ref):
    step = pl.program_id(0)

    @pl.when(step == 0)
    def _():
        carry_ref[...] = jnp.zeros_like(carry_ref)

    mix = _dot(ona_ref[...], wo_ref[0:NA_WIDTH, :]) + _dot(omla_ref[...], wo_ref[NA_WIDTH:, :])
    x1 = _layer_norm(ALPHA * x_ref[...] + mix, g_ref[...], b_ref[...])
    x1_ref[...] = x1

    tm = x1.shape[0]
    logit = _dot(x1.astype(BF16), wr_ref[...]) + br_ref[...]
    lane = lax.broadcasted_iota(jnp.int32, (tm, LANES), 1)
    g_mask = (lane >= N_EXPERTS) & (lane < N_EXPERTS + N_GROUPS)
    gl = jnp.where(g_mask, logit, NEG_BIG)
    gmax = jnp.max(gl, axis=-1, keepdims=True)
    gex = jnp.where(g_mask, jnp.exp(gl - gmax), 0.0)
    gprob = gex / jnp.sum(gex, axis=-1, keepdims=True)
    g_val = jnp.max(gprob, axis=-1, keepdims=True)
    g_idx = _first_lane_where(g_mask & (gprob == g_val), lane) - N_EXPERTS

    e_mask = (lane < N_EXPERTS) & ((lane // EXPERTS_PER_GROUP) == g_idx)
    el = jnp.where(e_mask, logit, NEG_BIG)
    emax = jnp.max(el, axis=-1, keepdims=True)
    eex = jnp.where(e_mask, jnp.exp(el - emax), 0.0)
    eprob = jnp.where(e_mask, eex / jnp.sum(eex, axis=-1, keepdims=True), -1.0)
    v1 = jnp.max(eprob, axis=-1, keepdims=True)
    i1 = _first_lane_where(eprob == v1, lane)
    eprob2 = jnp.where(lane == i1, -1.0, eprob)
    v2 = jnp.max(eprob2, axis=-1, keepdims=True)
    i2 = _first_lane_where(eprob2 == v2, lane)
    vsum = v1 + v2
    w1 = g_val * (v1 / vsum)
    w2 = g_val * (v2 / vsum)

    oh1 = lane == i1
    oh2 = lane == i2
    oh = jnp.where(oh1 | oh2, 1.0, 0.0)
    r_i = lax.broadcasted_iota(jnp.int32, (tm, tm), 0)
    c_i = lax.broadcasted_iota(jnp.int32, (tm, tm), 1)
    tri = jnp.where(c_i < r_i, 1.0, 0.0).astype(BF16)
    before = _dot(tri, oh.astype(BF16)) + carry_ref[...]
    rank1 = jnp.sum(jnp.where(oh1, before, 0.0), axis=-1, keepdims=True).astype(jnp.int32)
    rank2 = jnp.sum(jnp.where(oh2, before, 0.0), axis=-1, keepdims=True).astype(jnp.int32)
    carry_ref[...] = carry_ref[...] + jnp.sum(oh, axis=0, keepdims=True)
    cnt_ref[...] = carry_ref[...]

    ri_ref[...] = jnp.where(lane == 0, i1, jnp.where(lane == 1, i2,
                            jnp.where(lane == 2, rank1, jnp.where(lane == 3, rank2, 0))))
    gate_ref[...] = jnp.where(lane == 0, w1, jnp.where(lane == 1, w2, 0.0))


def _mix_out(o_na, o_mla, x, wo, g, b, wr, br):
    N, D = x.shape
    tm = MIX_TM
    row = lambda i: (i, 0)
    return pl.pallas_call(
        _mix_kernel,
        out_shape=(
            jax.ShapeDtypeStruct((N, D), F32),
            jax.ShapeDtypeStruct((N, LANES), jnp.int32),
            jax.ShapeDtypeStruct((N, LANES), F32),
            jax.ShapeDtypeStruct((1, LANES), F32),
        ),
        grid=(N // tm,),
        in_specs=[
            pl.BlockSpec((tm, NA_WIDTH), row),
            pl.BlockSpec((tm, NA_WIDTH), row),
            pl.BlockSpec((tm, D), row),
            _resident(wo.shape), _resident(g.shape), _resident(b.shape),
            _resident(wr.shape), _resident(br.shape),
        ],
        out_specs=(
            pl.BlockSpec((tm, D), row),
            pl.BlockSpec((tm, LANES), row),
            pl.BlockSpec((tm, LANES), row),
            pl.BlockSpec((1, LANES), lambda i: (0, 0)),
        ),
        scratch_shapes=[pltpu.VMEM((1, LANES), F32)],
        compiler_params=_params("arbitrary"),
        name="mix_out",
    )(o_na, o_mla, x, wo, g, b, wr, br)


def _row_copy(src, src_row, dst, dst_row, sem):
    return pltpu.make_async_copy(src.at[pl.ds(src_row, 1), :], dst.at[pl.ds(dst_row, 1), :], sem)


_PAD_CHUNKS = tuple(1 << b for b in reversed(range(3, MOE_TB.bit_length() - 1)))


def _dispatch_kernel(pos_ref, fill_ref, x_ref, xs_ref, zero_ref, sem, zsem):
    tm = x_ref.shape[0]
    base = pl.program_id(0) * (tm * TOP_K_EXPERT)

    def pad_copies(fn):
        def per_expert(e, carry):
            start = fill_ref[e]
            n = fill_ref[N_EXPERTS + e]
            head = (-start) & (SUBLANES - 1)
            for r in range(SUBLANES - 1):
                @pl.when(r < head)
                def _(r=r):
                    fn(_row_copy(zero_ref, 0, xs_ref, start + r, zsem))
            off = start + head
            rem = n - head
            for c in _PAD_CHUNKS:
                hit = (rem & c) != 0

                @pl.when(hit)
                def _(off=off, c=c):
                    fn(pltpu.make_async_copy(zero_ref.at[pl.ds(0, c), :],
                                             xs_ref.at[pl.ds(pl.multiple_of(off, SUBLANES), c), :],
                                             zsem))
                off = off + jnp.where(hit, c, 0)
            return carry
        lax.fori_loop(0, N_EXPERTS, per_expert, 0)

        zrows = zero_ref.shape[0]

        def tail(t, carry):
            row = pl.multiple_of(fill_ref[2 * N_EXPERTS] + t * zrows, zrows)
            fn(pltpu.make_async_copy(zero_ref, xs_ref.at[pl.ds(row, zrows), :], zsem))
            return carry
        lax.fori_loop(0, (xs_ref.shape[0] - fill_ref[2 * N_EXPERTS]) // zrows, tail, 0)

    @pl.when(pl.program_id(0) == 0)
    def _():
        zero_ref[...] = jnp.zeros_like(zero_ref)
        pad_copies(lambda d: d.start())

    def issue(j, carry):
        for kk in range(TOP_K_EXPERT):
            _row_copy(x_ref, j, xs_ref, pos_ref[base + j * TOP_K_EXPERT + kk], sem).start()
        return carry

    lax.fori_loop(0, tm, issue, 0, unroll=8)

    @pl.when(pl.program_id(0) == 0)
    def _():
        pad_copies(lambda d: d.wait())

    for kk in range(TOP_K_EXPERT):
        pltpu.make_async_copy(x_ref, xs_ref.at[pl.ds(0, tm), :], sem).wait()


def _dispatch(pos, fill, x1, n_rows):
    N, C = x1.shape
    tm = DISP_TM
    return pl.pallas_call(
        _dispatch_kernel,
        out_shape=jax.ShapeDtypeStruct((n_rows, C), x1.dtype),
        grid_spec=pltpu.PrefetchScalarGridSpec(
            num_scalar_prefetch=2,
            grid=(N // tm,),
            in_specs=[pl.BlockSpec((tm, C), lambda i, pos, fill: (i, 0))],
            out_specs=pl.BlockSpec(memory_space=pl.ANY),
            scratch_shapes=[pltpu.VMEM((_PAD_CHUNKS[0], C), x1.dtype),
                            pltpu.SemaphoreType.DMA, pltpu.SemaphoreType.DMA],
        ),
        compiler_params=_params("arbitrary"),
        name="dispatch",
    )(pos, fill, x1)


def _experts_kernel(be_ref, nxt_ref, nb_ref, xs_ref, wg_hbm, wu_hbm, wd_hbm, ys_ref,
                    wg_f, wu_f, wd_f, wg_s, wu_s, wd_s, sem):
    i = pl.program_id(0)
    active = i < nb_ref[0]
    e = be_ref[i]
    fresh = active & ((i == 0) | (e != be_ref[jnp.maximum(i - 1, 0)]))

    def fetch(ex):
        return (pltpu.make_async_copy(wg_hbm.at[ex], wg_f, sem.at[0]),
                pltpu.make_async_copy(wu_hbm.at[ex], wu_f, sem.at[1]),
                pltpu.make_async_copy(wd_hbm.at[ex], wd_f, sem.at[2]))

    @pl.when(i == 0)
    def _():
        for d in fetch(e):
            d.start()

    @pl.when(fresh)
    def _():
        for d in fetch(e):
            d.wait()
        wg_s[...] = wg_f[...].astype(BF16)
        wu_s[...] = wu_f[...].astype(BF16)
        wd_s[...] = wd_f[...].astype(BF16)
        nx = nxt_ref[i]

        @pl.when(nx >= 0)
        def _():
            for d in fetch(nx):
                d.start()

    @pl.when(active)
    def _():
        xb = xs_ref[...].astype(BF16)
        gp = _dot(xb, wg_s[...])
        up = _dot(xb, wu_s[...])
        hdn = (gp * jax.nn.sigmoid(gp) * up).astype(BF16)
        ys_ref[...] = _dot(hdn, wd_s[...])

    @pl.when(jnp.logical_not(active))
    def _():
        ys_ref[...] = jnp.zeros_like(ys_ref)


def _experts(block_e, next_e, n_used, xs, w_gate, w_up, w_down):
    P, C = xs.shape
    E, D, F = w_gate.shape
    tb = MOE_TB
    n_blocks = P // tb
    hbm = pl.BlockSpec(memory_space=pl.ANY)
    return pl.pallas_call(
        _experts_kernel,
        out_shape=jax.ShapeDtypeStruct((P, C), F32),
        grid_spec=pltpu.PrefetchScalarGridSpec(
            num_scalar_prefetch=3,
            grid=(n_blocks,),
            in_specs=[
                pl.BlockSpec((tb, C), lambda i, be, nx, nb: (jnp.minimum(i, nb[0] - 1), 0)),
                hbm, hbm, hbm,
            ],
            out_specs=pl.BlockSpec((tb, C), lambda i, be, nx, nb: (i, 0)),
            scratch_shapes=[pltpu.VMEM((D, F), F32), pltpu.VMEM((D, F), F32), pltpu.VMEM((F, D), F32),
                            pltpu.VMEM((D, F), BF16), pltpu.VMEM((D, F), BF16),
                            pltpu.VMEM((F, D), BF16), pltpu.SemaphoreType.DMA((3,))],
        ),
        compiler_params=_params("arbitrary"),
        name="experts",
    )(block_e, next_e, n_used, xs, w_gate, w_up, w_down)


def _final_kernel(pos_ref, x1_ref, gate_ref, ys_ref, p_ref, g_ref, b_ref, wpg_ref, wpe_ref,
                  gp_ref, o_ref, buf, sem):
    tm = x1_ref.shape[0]
    base = pl.program_id(0) * (tm * TOP_K_EXPERT)

    def issue(j, carry):
        for kk in range(TOP_K_EXPERT):
            _row_copy(ys_ref, pos_ref[base + j * TOP_K_EXPERT + kk], buf.at[kk], j, sem).start()
        return carry

    lax.fori_loop(0, tm, issue, 0, unroll=8)
    for kk in range(TOP_K_EXPERT):
        pltpu.make_async_copy(ys_ref.at[pl.ds(0, tm), :], buf.at[kk], sem).wait()

    gates = gate_ref[...]
    g1 = gates[:, 0:1]
    g2 = gates[:, 1:2]
    ffn = buf[0] * g1 + buf[1] * g2
    x2 = _layer_norm(ALPHA * x1_ref[...] + ffn, g_ref[...], b_ref[...])

    gate = jax.nn.sigmoid(_dot(x2.astype(BF16), wpg_ref[...]))
    e = _dot(p_ref[...].astype(BF16), wpe_ref[...])
    t = e * gate
    ple = t * lax.rsqrt(jnp.mean(t * t, axis=-1, keepdims=True) + RMS_EPS) * gp_ref[...]
    o_ref[...] = x2 + ple


def _final(pos, x1, gates, ys, p, g, b, wpg, wpe, gp):
    N, D = x1.shape
    tm = FIN_TM
    row = lambda i, pos: (i, 0)
    return pl.pallas_call(
        _final_kernel,
        out_shape=jax.ShapeDtypeStruct((N, D), F32),
        grid_spec=pltpu.PrefetchScalarGridSpec(
            num_scalar_prefetch=1,
            grid=(N // tm,),
            in_specs=[
                pl.BlockSpec((tm, D), row),
                pl.BlockSpec((tm, LANES), row),
                pl.BlockSpec(memory_space=pl.ANY),
                pl.BlockSpec((tm, PLE_DIM), row),
                _resident(g.shape), _resident(b.shape), _resident(wpg.shape),
                _resident(wpe.shape), _resident(gp.shape),
            ],
            out_specs=pl.BlockSpec((tm, D), row),
            scratch_shapes=[pltpu.VMEM((TOP_K_EXPERT, tm, D), F32),
                            pltpu.SemaphoreType.DMA],
        ),
        compiler_params=_params("arbitrary"),
        name="final",
    )(pos, x1, gates, ys, p, g, b, wpg, wpe, gp)


def _rope_table(positions):
    inv_freq = 1.0 / (ROPE_THETA ** (jnp.arange(0, MLA_ROPE_DIM, 2, dtype=F32) / MLA_ROPE_DIM))
    ang = positions.astype(F32)[..., None] * inv_freq
    cos, sin = jnp.cos(ang), jnp.sin(ang)
    return jnp.concatenate([cos, cos, sin, sin], axis=-1)


def _layer(x, p_i, rope_t, w_in, rpb, q_norm_g, kv_norm_g, w_uq, w_uk, w_uv, w_o, ln1_g, ln1_b,
           w_group, b_group, w_router, b_router, w_gate, w_up, w_down, ln2_g, ln2_b,
           w_ple, w_ple_gate, ple_norm_g):
    B, S, D = x.shape
    N = B * S
    s3 = 3 * NA_WIDTH
    s5 = s3 + Q_LORA_RANK + KV_LORA_RANK
    half = MLA_ROPE_DIM // 2

    y1, y2 = w_in[:, s5:s5 + half], w_in[:, s5 + half:]
    w_main = jnp.concatenate([w_in[:, :2 * NA_WIDTH], w_in[:, s3:s5], y1, y2, y2, y1],
                             axis=1).astype(BF16)
    wvt = w_in[:, 2 * NA_WIDTH:s3].T.astype(BF16)
    uq = w_uq.reshape(Q_LORA_RANK, MLA_HEADS, MLA_QK_DIM)
    nope, x1c, x2c = (uq[..., :MLA_NOPE_DIM], uq[..., MLA_NOPE_DIM:MLA_NOPE_DIM + half],
                      uq[..., MLA_NOPE_DIM + half:])
    wuq = jnp.concatenate([nope, x1c, x2c, x2c, x1c], axis=-1).reshape(Q_LORA_RANK, -1).astype(BF16)
    wuk = w_uk.astype(BF16)
    wuvt = w_uv.T.astype(BF16)
    row = lambda v: v.reshape(1, -1).astype(F32)

    q_na, k_na, v_nat, q_m, k_m, v_mt = _proj(x, rope_t, w_main, wvt, wuq, wuk, wuvt,
                                               row(q_norm_g), row(kv_norm_g))
    o_na = _na(q_na, k_na, v_nat, _na_bias_table(rpb))
    o_mla = _mla(q_m, k_m, v_mt)

    pad = LANES - N_EXPERTS - N_GROUPS
    wr = jnp.concatenate([w_router, w_group, jnp.zeros((D, pad), F32)], axis=1).astype(BF16)
    br = jnp.concatenate([b_router.reshape(-1), b_group, jnp.zeros((pad,), F32)]).reshape(1, -1)
    x1, ri, gates, cnt = _mix_out(o_na.reshape(N, -1), o_mla.reshape(N, -1), x.reshape(N, D),
                                       w_o.astype(BF16), row(ln1_g), row(ln1_b), wr, br)

    tb = MOE_TB
    counts = cnt[0, :N_EXPERTS].astype(jnp.int32)
    padded = ((counts + tb - 1) // tb) * tb
    pad_ends = jnp.cumsum(padded)
    pad_starts = pad_ends - padded
    pos = (pad_starts[ri[:, 0:2]] + ri[:, 2:4]).reshape(-1).astype(jnp.int32)
    fill = jnp.concatenate([pad_starts + counts, padded - counts, pad_ends[-1:]]).astype(jnp.int32)
    n_blocks = (N * TOP_K_EXPERT + N_EXPERTS * (tb - 1) + tb - 1) // tb
    blk_row = jnp.arange(n_blocks, dtype=jnp.int32) * tb
    block_e = jnp.minimum(jnp.sum(pad_ends[None, :] <= blk_row[:, None], axis=1),
                          N_EXPERTS - 1).astype(jnp.int32)
    n_used = (pad_ends[-1:] // tb).astype(jnp.int32)
    nxt_blk = pad_ends[block_e] // tb
    next_e = jnp.where(nxt_blk < n_used[0], block_e[jnp.minimum(nxt_blk, n_blocks - 1)],
                       -1).astype(jnp.int32)

    xs = _dispatch(pos, fill, x1, n_blocks * tb)
    ys = _experts(block_e, next_e, n_used, xs, w_gate, w_up, w_down)
    out = _final(pos, x1, gates, ys, p_i.reshape(N, -1), row(ln2_g), row(ln2_b),
                 w_ple_gate.astype(BF16), w_ple.astype(BF16), row(ple_norm_g))
    return out.reshape(B, S, D)


def kernel(x, p, positions, w_in, rpb, q_norm_g, kv_norm_g, w_uq, w_uk, w_uv, w_o, ln1_g, ln1_b,
           w_group, b_group, w_router, b_router, w_gate, w_up, w_down, ln2_g, ln2_b,
           w_ple, w_ple_gate, ple_norm_g):
    rope_t = _rope_table(positions)
    for i in range(DEPTH):
        x = _layer(x, p[i], rope_t, w_in[i], rpb[i], q_norm_g[i], kv_norm_g[i], w_uq[i], w_uk[i],
                   w_uv[i], w_o[i], ln1_g[i], ln1_b[i], w_group[i], b_group[i], w_router[i],
                   b_router[i], w_gate[i], w_up[i], w_down[i], ln2_g[i], ln2_b[i],
                   w_ple[i], w_ple_gate[i], ple_norm_g[i])
    return x
```

```python
import numpy as np
import jax
import jax.numpy as jnp
from jax import lax
from jax.experimental import pallas as pl
from jax.experimental.pallas import tpu as pltpu

D_MODEL = 2048
DEPTH = 1
GRID_W = 64
PLE_DIM = 256
NA_HEADS = 8
NA_HEAD_DIM = 128
NA_WIN_H = 8
NA_WIN_W = 16
NA_WIDTH = NA_HEADS * NA_HEAD_DIM
MLA_HEADS = 8
MLA_NOPE_DIM = 128
MLA_ROPE_DIM = 64
MLA_V_DIM = 128
MLA_QK_DIM = MLA_NOPE_DIM + MLA_ROPE_DIM
Q_LORA_RANK = 512
KV_LORA_RANK = 512
ROPE_THETA = 10000.0
N_GROUPS = 8
EXPERTS_PER_GROUP = 8
N_EXPERTS = N_GROUPS * EXPERTS_PER_GROUP
TOP_K_EXPERT = 2
D_EXPERT = 512
ALPHA = (2 * DEPTH) ** 0.25
LN_EPS = 1e-5
RMS_EPS = 1e-6

LANES = 128
SUBLANES = 8
VMEM_LIMIT_BYTES = 56 * 1024 * 1024

PROJ_TM = 256
NA_ROWS = 4
NA_QB = NA_ROWS * GRID_W
NA_WIN_ROWS = NA_ROWS + NA_WIN_H
NA_KB = NA_WIN_ROWS * GRID_W
MLA_TQ = 512
MLA_KC = 512
MIX_TM = 256
MOE_TB = 256
DISP_TM = 256
FIN_TM = 256
NEG_BIG = -1e30
LOG2E = 1.4426950408889634

BF16 = jnp.bfloat16
F32 = jnp.float32

_NT = (((1,), (1,)), ((), ()))


def _dot(a, b):
    return jnp.dot(a, b, preferred_element_type=F32)


def _dot_nt(a, b):
    return lax.dot_general(a, b, _NT, preferred_element_type=F32)


def _params(*sem):
    return pltpu.CompilerParams(dimension_semantics=sem, vmem_limit_bytes=VMEM_LIMIT_BYTES)


def _resident(shape):
    nd = len(shape)
    return pl.BlockSpec(shape, lambda *_: (0,) * nd, pipeline_mode=pl.Buffered(1))


def _proj_kernel(x_ref, t_ref, w_ref, wvt_ref, wuq_ref, wuk_ref, wuvt_ref, gq_ref, gkv_ref,
                 qna_ref, kna_ref, vnat_ref, qm_ref, km_ref, vmt_ref):
    xb = x_ref[0].astype(BF16)
    na_scale = NA_HEAD_DIM ** -0.5 * LOG2E
    mla_scale = MLA_QK_DIM ** -0.5 * LOG2E
    q = _dot(xb, w_ref[:, 0:NA_WIDTH]) * na_scale
    k = _dot(xb, w_ref[:, NA_WIDTH:2 * NA_WIDTH])
    vt = _dot_nt(wvt_ref[...], xb)
    for h in range(NA_HEADS):
        sl = slice(h * NA_HEAD_DIM, (h + 1) * NA_HEAD_DIM)
        qna_ref[0, h] = q[:, sl].astype(BF16)
        kna_ref[0, h] = k[:, sl].astype(BF16)
        vnat_ref[0, h] = vt[sl, :].astype(BF16)

    c0 = 2 * NA_WIDTH
    cq = _dot(xb, w_ref[:, c0:c0 + Q_LORA_RANK])
    ckv = _dot(xb, w_ref[:, c0 + Q_LORA_RANK:c0 + Q_LORA_RANK + KV_LORA_RANK])
    kr4 = _dot(xb, w_ref[:, c0 + Q_LORA_RANK + KV_LORA_RANK:])

    def rms(c, g):
        return c * lax.rsqrt(jnp.mean(c * c, axis=-1, keepdims=True) + RMS_EPS) * g

    cqn = rms(cq, gq_ref[...]).astype(BF16)
    ckvn = rms(ckv, gkv_ref[...]).astype(BF16)

    t = t_ref[0]
    e = kr4 * t
    lane = lax.broadcasted_iota(jnp.int32, (1, LANES), 1)
    sign = jnp.where((lane // 32) % 2 == 0, -1.0, 1.0).astype(F32)
    kpe = (e + sign * pltpu.roll(e, 64, 1)).astype(BF16)

    qf = _dot(cqn, wuq_ref[...])
    kn = _dot(ckvn, wuk_ref[...])
    vmt = _dot_nt(wuvt_ref[...], ckvn)
    ts = t * mla_scale
    for h in range(MLA_HEADS):
        qm_ref[0, h, :, 0:128] = (qf[:, h * 256:h * 256 + 128] * mla_scale).astype(BF16)
        qm_ref[0, h, :, 128:256] = (qf[:, h * 256 + 128:(h + 1) * 256] * ts).astype(BF16)
        km_ref[0, h, :, 0:128] = kn[:, h * 128:(h + 1) * 128].astype(BF16)
        km_ref[0, h, :, 128:256] = kpe
        vmt_ref[0, h] = vmt[h * 128:(h + 1) * 128, :].astype(BF16)


def _proj(x, rope_t, w_main, wvt, wuq, wuk, wuvt, gq, gkv):
    B, S, D = x.shape
    tm = PROJ_TM
    hm = lambda b, i: (b, 0, i, 0)
    hmt = lambda b, i: (b, 0, 0, i)
    out_shape = (
        jax.ShapeDtypeStruct((B, NA_HEADS, S, NA_HEAD_DIM), BF16),
        jax.ShapeDtypeStruct((B, NA_HEADS, S, NA_HEAD_DIM), BF16),
        jax.ShapeDtypeStruct((B, NA_HEADS, NA_HEAD_DIM, S), BF16),
        jax.ShapeDtypeStruct((B, MLA_HEADS, S, 256), BF16),
        jax.ShapeDtypeStruct((B, MLA_HEADS, S, 256), BF16),
        jax.ShapeDtypeStruct((B, MLA_HEADS, MLA_V_DIM, S), BF16),
    )
    return pl.pallas_call(
        _proj_kernel,
        out_shape=out_shape,
        grid=(B, S // tm),
        in_specs=[
            pl.BlockSpec((1, tm, D), lambda b, i: (b, i, 0)),
            pl.BlockSpec((1, tm, LANES), lambda b, i: (b, i, 0)),
            _resident(w_main.shape), _resident(wvt.shape), _resident(wuq.shape),
            _resident(wuk.shape), _resident(wuvt.shape), _resident(gq.shape), _resident(gkv.shape),
        ],
        out_specs=(
            pl.BlockSpec((1, NA_HEADS, tm, NA_HEAD_DIM), hm),
            pl.BlockSpec((1, NA_HEADS, tm, NA_HEAD_DIM), hm),
            pl.BlockSpec((1, NA_HEADS, NA_HEAD_DIM, tm), hmt),
            pl.BlockSpec((1, MLA_HEADS, tm, 256), hm),
            pl.BlockSpec((1, MLA_HEADS, tm, 256), hm),
            pl.BlockSpec((1, MLA_HEADS, MLA_V_DIM, tm), hmt),
        ),
        compiler_params=_params("arbitrary", "arbitrary"),
        name="proj",
    )(x, rope_t, w_main, wvt, wuq, wuk, wuvt, gq, gkv)


def _na_bias_table(rpb):
    rows = GRID_W
    n_blocks = rows // NA_ROWS
    n_ro, n_co = 2 * NA_WIN_H - 1, 2 * NA_WIN_W - 1
    kr, qr = np.arange(NA_WIN_ROWS), np.arange(NA_ROWS)
    kc, qc = np.arange(GRID_W), np.arange(GRID_W)
    c0 = np.clip(qc - NA_WIN_W // 2, 0, GRID_W - NA_WIN_W)
    col_ok = (kc[:, None] >= c0[None, :]) & (kc[:, None] < c0[None, :] + NA_WIN_W)
    col_off = np.clip(kc[:, None] - qc[None, :] + (NA_WIN_W - 1), 0, n_co - 1)
    oh_c = np.zeros((n_co, GRID_W, GRID_W), np.float32)
    oh_c[col_off, kc[:, None], qc[None, :]] = 1.0
    oh_r = np.zeros((3, n_ro, NA_WIN_ROWS, NA_ROWS), np.float32)
    row_ok = np.zeros((3, NA_WIN_ROWS, NA_ROWS), bool)
    for c, rb in enumerate((0, 1, n_blocks - 1)):
        w0 = int(np.clip(NA_ROWS * rb - NA_WIN_H // 2, 0, rows - NA_WIN_ROWS))
        r = NA_ROWS * rb + qr
        r0 = np.clip(r - NA_WIN_H // 2, 0, rows - NA_WIN_H)
        krow = w0 + kr
        row_ok[c] = (krow[:, None] >= r0[None, :]) & (krow[:, None] < r0[None, :] + NA_WIN_H)
        row_off = np.clip(krow[:, None] - r[None, :] + (NA_WIN_H - 1), 0, n_ro - 1)
        oh_r[c, row_off, kr[:, None], qr[None, :]] = 1.0
    sel = np.zeros((NA_ROWS, n_co, GRID_W, NA_ROWS, GRID_W), np.float32)
    for r in range(NA_ROWS):
        sel[r, :, :, r, :] = oh_c
    sel = sel.reshape(NA_ROWS * n_co, GRID_W, NA_QB)
    ok = row_ok[:, :, None, :, None] & col_ok[None, None, :, None, :]
    mask = np.where(ok, 0.0, NEG_BIG).astype(np.float32).reshape(3, NA_WIN_ROWS, GRID_W, NA_QB)
    hi = lax.Precision.HIGHEST
    a = jnp.einsum('hrc,zrkq->hzkqc', rpb.astype(F32) * LOG2E, oh_r, precision=hi)
    a = a.reshape(rpb.shape[0], 3, NA_WIN_ROWS, NA_ROWS * n_co)
    tab = jnp.einsum('hzkj,jxn->hzkxn', a, sel, precision=hi) + mask[None]
    return tab.reshape(rpb.shape[0], 3, NA_KB, NA_QB)


def _na_kernel(q_ref, k_ref, vt_ref, tab_ref, o_ref):
    n_blocks = q_ref.shape[2] // NA_QB
    rows = n_blocks * NA_ROWS

    def body(rb, carry):
        w0 = jnp.clip(NA_ROWS * rb - NA_WIN_H // 2, 0, rows - NA_WIN_ROWS)
        tok0 = pl.multiple_of(w0 * GRID_W, NA_QB)
        q0 = pl.multiple_of(rb * NA_QB, NA_QB)
        cls = jnp.where(rb == 0, 0, jnp.where(rb == n_blocks - 1, 2, 1))
        qb = q_ref[0, 0, pl.ds(q0, NA_QB), :]
        kw = k_ref[0, 0, pl.ds(tok0, NA_KB), :]
        st = _dot_nt(kw, qb) + tab_ref[0, cls]
        m = jnp.max(st, axis=0, keepdims=True)
        p = jnp.exp2(st - m)
        l = jnp.sum(p, axis=0, keepdims=True)
        vw = vt_ref[0, 0, :, pl.ds(tok0, NA_KB)]
        ot = _dot(vw, p.astype(BF16)) / l
        o_ref[0, pl.ds(q0, NA_QB), :] = ot.T.astype(o_ref.dtype)
        return carry

    lax.fori_loop(0, n_blocks, body, 0, unroll=2)


def _na(q, k, vt, tab):
    B, H, S, Dh = q.shape
    return pl.pallas_call(
        _na_kernel,
        out_shape=jax.ShapeDtypeStruct((B, S, H * Dh), BF16),
        grid=(H, B),
        in_specs=[
            pl.BlockSpec((1, 1, S, Dh), lambda h, b: (b, h, 0, 0)),
            pl.BlockSpec((1, 1, S, Dh), lambda h, b: (b, h, 0, 0)),
            pl.BlockSpec((1, 1, Dh, S), lambda h, b: (b, h, 0, 0)),
            pl.BlockSpec((1, 3, NA_KB, NA_QB), lambda h, b: (h, 0, 0, 0)),
        ],
        out_specs=pl.BlockSpec((1, S, Dh), lambda h, b: (b, 0, h)),
        compiler_params=_params("arbitrary", "arbitrary"),
        name="na",
    )(q, k, vt, tab)


def _mla_kernel(q_ref, k_ref, vt_ref, o_ref, st_ref):
    q = q_ref[0, 0]
    n_chunks = k_ref.shape[2] // MLA_KC
    m = None
    for c in range(n_chunks):
        ks = slice(c * MLA_KC, (c + 1) * MLA_KC)
        st = _dot_nt(k_ref[0, 0, ks, :], q)
        st_ref[ks, :] = st
        mc = jnp.max(st, axis=0, keepdims=True)
        m = mc if c == 0 else jnp.maximum(m, mc)
    l = acc = None
    for c in range(n_chunks):
        ks = slice(c * MLA_KC, (c + 1) * MLA_KC)
        p = jnp.exp2(st_ref[ks, :] - m)
        ps = jnp.sum(p, axis=0, keepdims=True)
        pv = _dot(vt_ref[0, 0, :, ks], p.astype(BF16))
        l = ps if c == 0 else l + ps
        acc = pv if c == 0 else acc + pv
    o_ref[0] = (acc / l).T.astype(o_ref.dtype)


def _mla(q, k, vt):
    B, H, S, Dq = q.shape
    Dv = MLA_V_DIM
    tq = MLA_TQ
    return pl.pallas_call(
        _mla_kernel,
        out_shape=jax.ShapeDtypeStruct((B, S, H * Dv), BF16),
        grid=(B, H, S // tq),
        in_specs=[
            pl.BlockSpec((1, 1, tq, Dq), lambda b, h, i: (b, h, i, 0)),
            pl.BlockSpec((1, 1, S, Dq), lambda b, h, i: (b, h, 0, 0)),
            pl.BlockSpec((1, 1, Dv, S), lambda b, h, i: (b, h, 0, 0)),
        ],
        out_specs=pl.BlockSpec((1, tq, Dv), lambda b, h, i: (b, i, h)),
        scratch_shapes=[pltpu.VMEM((S, tq), F32)],
        compiler_params=_params("arbitrary", "arbitrary", "arbitrary"),
        name="mla",
    )(q, k, vt)


def _layer_norm(z, g, b):
    mu = jnp.mean(z, axis=-1, keepdims=True)
    zc = z - mu
    var = jnp.mean(zc * zc, axis=-1, keepdims=True)
    return zc * lax.rsqrt(var + LN_EPS) * g + b


def _first_lane_where(cond, lane):
    return jnp.min(jnp.where(cond, lane, LANES), axis=-1, keepdims=True)


def _mix_kernel(ona_ref, omla_ref, x_ref, wo_ref, g_ref, b_ref, wr_ref, br_ref,
                x1_ref, ri_ref, gate_ref, cnt_ref, carry_ref):
    step = pl.program_id(0)

    @pl.when(step == 0)
    def _():
        carry_ref[...] = jnp.zeros_like(carry_ref)

    mix = _dot(ona_ref[...], wo_ref[0:NA_WIDTH, :]) + _dot(omla_ref[...], wo_ref[NA_WIDTH:, :])
    x1 = _layer_norm(ALPHA * x_ref[...] + mix, g_ref[...], b_ref[...])
    x1_ref[...] = x1

    tm = x1.shape[0]
    logit = _dot(x1.astype(BF16), wr_ref[...]) + br_ref[...]
    lane = lax.broadcasted_iota(jnp.int32, (tm, LANES), 1)
    g_mask = (lane >= N_EXPERTS) & (lane < N_EXPERTS + N_GROUPS)
    gl = jnp.where(g_mask, logit, NEG_BIG)
    gmax = jnp.max(gl, axis=-1, keepdims=True)
    gex = jnp.where(g_mask, jnp.exp(gl - gmax), 0.0)
    gprob = gex / jnp.sum(gex, axis=-1, keepdims=True)
    g_val = jnp.max(gprob, axis=-1, keepdims=True)
    g_idx = _first_lane_where(g_mask & (gprob == g_val), lane) - N_EXPERTS

    e_mask = (lane < N_EXPERTS) & ((lane // EXPERTS_PER_GROUP) == g_idx)
    el = jnp.where(e_mask, logit, NEG_BIG)
    emax = jnp.max(el, axis=-1, keepdims=True)
    eex = jnp.where(e_mask, jnp.exp(el - emax), 0.0)
    eprob = jnp.where(e_mask, eex / jnp.sum(eex, axis=-1, keepdims=True), -1.0)
    v1 = jnp.max(eprob, axis=-1, keepdims=True)
    i1 = _first_lane_where(eprob == v1, lane)
    eprob2 = jnp.where(lane == i1, -1.0, eprob)
    v2 = jnp.max(eprob2, axis=-1, keepdims=True)
    i2 = _first_lane_where(eprob2 == v2, lane)
    vsum = v1 + v2
    w1 = g_val * (v1 / vsum)
    w2 = g_val * (v2 / vsum)

    oh1 = lane == i1
    oh2 = lane == i2
    oh = jnp.where(oh1 | oh2, 1.0, 0.0)
    r_i = lax.broadcasted_iota(jnp.int32, (tm, tm), 0)
    c_i = lax.broadcasted_iota(jnp.int32, (tm, tm), 1)
    tri = jnp.where(c_i < r_i, 1.0, 0.0).astype(BF16)
    before = _dot(tri, oh.astype(BF16)) + carry_ref[...]
    rank1 = jnp.sum(jnp.where(oh1, before, 0.0), axis=-1, keepdims=True).astype(jnp.int32)
    rank2 = jnp.sum(jnp.where(oh2, before, 0.0), axis=-1, keepdims=True).astype(jnp.int32)
    carry_ref[...] = carry_ref[...] + jnp.sum(oh, axis=0, keepdims=True)
    cnt_ref[...] = carry_ref[...]

    ri_ref[...] = jnp.where(lane == 0, i1, jnp.where(lane == 1, i2,
                            jnp.where(lane == 2, rank1, jnp.where(lane == 3, rank2, 0))))
    gate_ref[...] = jnp.where(lane == 0, w1, jnp.where(lane == 1, w2, 0.0))


def _mix_out(o_na, o_mla, x, wo, g, b, wr, br):
    N, D = x.shape
    tm = MIX_TM
    row = lambda i: (i, 0)
    return pl.pallas_call(
        _mix_kernel,
        out_shape=(
            jax.ShapeDtypeStruct((N, D), F32),
            jax.ShapeDtypeStruct((N, LANES), jnp.int32),
            jax.ShapeDtypeStruct((N, LANES), F32),
            jax.ShapeDtypeStruct((1, LANES), F32),
        ),
        grid=(N // tm,),
        in_specs=[
            pl.BlockSpec((tm, NA_WIDTH), row),
            pl.BlockSpec((tm, NA_WIDTH), row),
            pl.BlockSpec((tm, D), row),
            _resident(wo.shape), _resident(g.shape), _resident(b.shape),
            _resident(wr.shape), _resident(br.shape),
        ],
        out_specs=(
            pl.BlockSpec((tm, D), row),
            pl.BlockSpec((tm, LANES), row),
            pl.BlockSpec((tm, LANES), row),
            pl.BlockSpec((1, LANES), lambda i: (0, 0)),
        ),
        scratch_shapes=[pltpu.VMEM((1, LANES), F32)],
        compiler_params=_params("arbitrary"),
        name="mix_out",
    )(o_na, o_mla, x, wo, g, b, wr, br)


def _row_copy(src, src_row, dst, dst_row, sem):
    return pltpu.make_async_copy(src.at[pl.ds(src_row, 1), :], dst.at[pl.ds(dst_row, 1), :], sem)


_PAD_CHUNKS = tuple(1 << b for b in reversed(range(3, MOE_TB.bit_length() - 1)))


def _dispatch_kernel(pos_ref, fill_ref, x_ref, xs_ref, zero_ref, sem, zsem):
    tm = x_ref.shape[0]
    base = pl.program_id(0) * (tm * TOP_K_EXPERT)

    def pad_copies(fn):
        def per_expert(e, carry):
            start = fill_ref[e]
            n = fill_ref[N_EXPERTS + e]
            head = (-start) & (SUBLANES - 1)
            for r in range(SUBLANES - 1):
                @pl.when(r < head)
                def _(r=r):
                    fn(_row_copy(zero_ref, 0, xs_ref, start + r, zsem))
            off = start + head
            rem = n - head
            for c in _PAD_CHUNKS:
                hit = (rem & c) != 0

                @pl.when(hit)
                def _(off=off, c=c):
                    fn(pltpu.make_async_copy(zero_ref.at[pl.ds(0, c), :],
                                             xs_ref.at[pl.ds(pl.multiple_of(off, SUBLANES), c), :],
                                             zsem))
                off = off + jnp.where(hit, c, 0)
            return carry
        lax.fori_loop(0, N_EXPERTS, per_expert, 0)

        zrows = zero_ref.shape[0]

        def tail(t, carry):
            row = pl.multiple_of(fill_ref[2 * N_EXPERTS] + t * zrows, zrows)
            fn(pltpu.make_async_copy(zero_ref, xs_ref.at[pl.ds(row, zrows), :], zsem))
            return carry
        lax.fori_loop(0, (xs_ref.shape[0] - fill_ref[2 * N_EXPERTS]) // zrows, tail, 0)

    @pl.when(pl.program_id(0) == 0)
    def _():
        zero_ref[...] = jnp.zeros_like(zero_ref)
        pad_copies(lambda d: d.start())

    def issue(j, carry):
        for kk in range(TOP_K_EXPERT):
            _row_copy(x_ref, j, xs_ref, pos_ref[base + j * TOP_K_EXPERT + kk], sem).start()
        return carry

    lax.fori_loop(0, tm, issue, 0, unroll=8)

    @pl.when(pl.program_id(0) == 0)
    def _():
        pad_copies(lambda d: d.wait())

    for kk in range(TOP_K_EXPERT):
        pltpu.make_async_copy(x_ref, xs_ref.at[pl.ds(0, tm), :], sem).wait()


def _dispatch(pos, fill, x1, n_rows):
    N, C = x1.shape
    tm = DISP_TM
    return pl.pallas_call(
        _dispatch_kernel,
        out_shape=jax.ShapeDtypeStruct((n_rows, C), x1.dtype),
        grid_spec=pltpu.PrefetchScalarGridSpec(
            num_scalar_prefetch=2,
            grid=(N // tm,),
            in_specs=[pl.BlockSpec((tm, C), lambda i, pos, fill: (i, 0))],
            out_specs=pl.BlockSpec(memory_space=pl.ANY),
            scratch_shapes=[pltpu.VMEM((_PAD_CHUNKS[0], C), x1.dtype),
                            pltpu.SemaphoreType.DMA, pltpu.SemaphoreType.DMA],
        ),
        compiler_params=_params("arbitrary"),
        name="dispatch",
    )(pos, fill, x1)


def _experts_kernel(be_ref, nxt_ref, nb_ref, xs_ref, wg_hbm, wu_hbm, wd_hbm, ys_ref,
                    wg_f, wu_f, wd_f, wg_s, wu_s, wd_s, sem):
    i = pl.program_id(0)
    active = i < nb_ref[0]
    e = be_ref[i]
    fresh = active & ((i == 0) | (e != be_ref[jnp.maximum(i - 1, 0)]))

    def fetch(ex):
        return (pltpu.make_async_copy(wg_hbm.at[ex], wg_f, sem.at[0]),
                pltpu.make_async_copy(wu_hbm.at[ex], wu_f, sem.at[1]),
                pltpu.make_async_copy(wd_hbm.at[ex], wd_f, sem.at[2]))

    @pl.when(i == 0)
    def _():
        for d in fetch(e):
            d.start()

    @pl.when(fresh)
    def _():
        for d in fetch(e):
            d.wait()
        wg_s[...] = wg_f[...].astype(BF16)
        wu_s[...] = wu_f[...].astype(BF16)
        wd_s[...] = wd_f[...].astype(BF16)
        nx = nxt_ref[i]

        @pl.when(nx >= 0)
        def _():
            for d in fetch(nx):
                d.start()

    @pl.when(active)
    def _():
        xb = xs_ref[...].astype(BF16)
        gp = _dot(xb, wg_s[...])
        up = _dot(xb, wu_s[...])
        hdn = (gp * jax.nn.sigmoid(gp) * up).astype(BF16)
        ys_ref[...] = _dot(hdn, wd_s[...])

    @pl.when(jnp.logical_not(active))
    def _():
        ys_ref[...] = jnp.zeros_like(ys_ref)


def _experts(block_e, next_e, n_used, xs, w_gate, w_up, w_down):
    P, C = xs.shape
    E, D, F = w_gate.shape
    tb = MOE_TB
    n_blocks = P // tb
    hbm = pl.BlockSpec(memory_space=pl.ANY)
    return pl.pallas_call(
        _experts_kernel,
        out_shape=jax.ShapeDtypeStruct((P, C), F32),
        grid_spec=pltpu.PrefetchScalarGridSpec(
            num_scalar_prefetch=3,
            grid=(n_blocks,),
            in_specs=[
                pl.BlockSpec((tb, C), lambda i, be, nx, nb: (jnp.minimum(i, nb[0] - 1), 0)),
                hbm, hbm, hbm,
            ],
            out_specs=pl.BlockSpec((tb, C), lambda i, be, nx, nb: (i, 0)),
            scratch_shapes=[pltpu.VMEM((D, F), F32), pltpu.VMEM((D, F), F32), pltpu.VMEM((F, D), F32),
                            pltpu.VMEM((D, F), BF16), pltpu.VMEM((D, F), BF16),
                            pltpu.VMEM((F, D), BF16), pltpu.SemaphoreType.DMA((3,))],
        ),
        compiler_params=_params("arbitrary"),
        name="experts",
    )(block_e, next_e, n_used, xs, w_gate, w_up, w_down)


def _final_kernel(pos_ref, x1_ref, gate_ref, ys_ref, p_ref, g_ref, b_ref, wpg_ref, wpe_ref,
                  gp_ref, o_ref, buf, sem):
    tm = x1_ref.shape[0]
    i = pl.program_id(0)
    last = pl.num_programs(0) - 1
    slot = i % 2

    def issue(tile, s, unroll):
        base = tile * (tm * TOP_K_EXPERT)

        def body(j, carry):
            for kk in range(TOP_K_EXPERT):
                _row_copy(ys_ref, pos_ref[base + j * TOP_K_EXPERT + kk],
                          buf.at[s, kk], j, sem.at[s]).start()
            return carry
        lax.fori_loop(0, tm, body, 0, unroll=unroll)

    def drain(s):
        for kk in range(TOP_K_EXPERT):
            pltpu.make_async_copy(ys_ref.at[pl.ds(0, tm), :], buf.at[s, kk], sem.at[s]).wait()

    @pl.when(i == 0)
    def _():
        issue(0, 0, 8)

    drain(slot)
    issue(jnp.minimum(i + 1, last), 1 - slot, True)

    gates = gate_ref[...]
    g1 = gates[:, 0:1]
    g2 = gates[:, 1:2]
    ffn = buf[slot, 0] * g1 + buf[slot, 1] * g2
    x2 = _layer_norm(ALPHA * x1_ref[...] + ffn, g_ref[...], b_ref[...])

    gate = jax.nn.sigmoid(_dot(x2.astype(BF16), wpg_ref[...]))
    e = _dot(p_ref[...].astype(BF16), wpe_ref[...])
    t = e * gate
    ple = t * lax.rsqrt(jnp.mean(t * t, axis=-1, keepdims=True) + RMS_EPS) * gp_ref[...]
    o_ref[...] = x2 + ple

    @pl.when(i == last)
    def _():
        drain(1 - slot)


def _final(pos, x1, gates, ys, p, g, b, wpg, wpe, gp):
    N, D = x1.shape
    tm = FIN_TM
    row = lambda i, pos: (i, 0)
    return pl.pallas_call(
        _final_kernel,
        out_shape=jax.ShapeDtypeStruct((N, D), F32),
        grid_spec=pltpu.PrefetchScalarGridSpec(
            num_scalar_prefetch=1,
            grid=(N // tm,),
            in_specs=[
                pl.BlockSpec((tm, D), row),
                pl.BlockSpec((tm, LANES), row),
                pl.BlockSpec(memory_space=pl.ANY),
                pl.BlockSpec((tm, PLE_DIM), row),
                _resident(g.shape), _resident(b.shape), _resident(wpg.shape),
                _resident(wpe.shape), _resident(gp.shape),
            ],
            out_specs=pl.BlockSpec((tm, D), row),
            scratch_shapes=[pltpu.VMEM((2, TOP_K_EXPERT, tm, D), F32),
                            pltpu.SemaphoreType.DMA((2,))],
        ),
        compiler_params=_params("arbitrary"),
        name="final",
    )(pos, x1, gates, ys, p, g, b, wpg, wpe, gp)


def _rope_table(positions):
    inv_freq = 1.0 / (ROPE_THETA ** (jnp.arange(0, MLA_ROPE_DIM, 2, dtype=F32) / MLA_ROPE_DIM))
    ang = positions.astype(F32)[..., None] * inv_freq
    cos, sin = jnp.cos(ang), jnp.sin(ang)
    return jnp.concatenate([cos, cos, sin, sin], axis=-1)


def _layer(x, p_i, rope_t, w_in, rpb, q_norm_g, kv_norm_g, w_uq, w_uk, w_uv, w_o, ln1_g, ln1_b,
           w_group, b_group, w_router, b_router, w_gate, w_up, w_down, ln2_g, ln2_b,
           w_ple, w_ple_gate, ple_norm_g):
    B, S, D = x.shape
    N = B * S
    s3 = 3 * NA_WIDTH
    s5 = s3 + Q_LORA_RANK + KV_LORA_RANK
    half = MLA_ROPE_DIM // 2

    y1, y2 = w_in[:, s5:s5 + half], w_in[:, s5 + half:]
    w_main = jnp.concatenate([w_in[:, :2 * NA_WIDTH], w_in[:, s3:s5], y1, y2, y2, y1],
                             axis=1).astype(BF16)
    wvt = w_in[:, 2 * NA_WIDTH:s3].T.astype(BF16)
    uq = w_uq.reshape(Q_LORA_RANK, MLA_HEADS, MLA_QK_DIM)
    nope, x1c, x2c = (uq[..., :MLA_NOPE_DIM], uq[..., MLA_NOPE_DIM:MLA_NOPE_DIM + half],
                      uq[..., MLA_NOPE_DIM + half:])
    wuq = jnp.concatenate([nope, x1c, x2c, x2c, x1c], axis=-1).reshape(Q_LORA_RANK, -1).astype(BF16)
    wuk = w_uk.astype(BF16)
    wuvt = w_uv.T.astype(BF16)
    row = lambda v: v.reshape(1, -1).astype(F32)

    q_na, k_na, v_nat, q_m, k_m, v_mt = _proj(x, rope_t, w_main, wvt, wuq, wuk, wuvt,
                                               row(q_norm_g), row(kv_norm_g))
    o_na = _na(q_na, k_na, v_nat, _na_bias_table(rpb))
    o_mla = _mla(q_m, k_m, v_mt)

    pad = LANES - N_EXPERTS - N_GROUPS
    wr = jnp.concatenate([w_router, w_group, jnp.zeros((D, pad), F32)], axis=1).astype(BF16)
    br = jnp.concatenate([b_router.reshape(-1), b_group, jnp.zeros((pad,), F32)]).reshape(1, -1)
    x1, ri, gates, cnt = _mix_out(o_na.reshape(N, -1), o_mla.reshape(N, -1), x.reshape(N, D),
                                       w_o.astype(BF16), row(ln1_g), row(ln1_b), wr, br)

    tb = MOE_TB
    counts = cnt[0, :N_EXPERTS].astype(jnp.int32)
    padded = ((counts + tb - 1) // tb) * tb
    pad_ends = jnp.cumsum(padded)
    pad_starts = pad_ends - padded
    pos = (pad_starts[ri[:, 0:2]] + ri[:, 2:4]).reshape(-1).astype(jnp.int32)
    fill = jnp.concatenate([pad_starts + counts, padded - counts, pad_ends[-1:]]).astype(jnp.int32)
    n_blocks = (N * TOP_K_EXPERT + N_EXPERTS * (tb - 1) + tb - 1) // tb
    blk_row = jnp.arange(n_blocks, dtype=jnp.int32) * tb
    block_e = jnp.minimum(jnp.sum(pad_ends[None, :] <= blk_row[:, None], axis=1),
                          N_EXPERTS - 1).astype(jnp.int32)
    n_used = (pad_ends[-1:] // tb).astype(jnp.int32)
    nxt_blk = pad_ends[block_e] // tb
    next_e = jnp.where(nxt_blk < n_used[0], block_e[jnp.minimum(nxt_blk, n_blocks - 1)],
                       -1).astype(jnp.int32)

    xs = _dispatch(pos, fill, x1, n_blocks * tb)
    ys = _experts(block_e, next_e, n_used, xs, w_gate, w_up, w_down)
    out = _final(pos, x1, gates, ys, p_i.reshape(N, -1), row(ln2_g), row(ln2_b),
                 w_ple_gate.astype(BF16), w_ple.astype(BF16), row(ple_norm_g))
    return out.reshape(B, S, D)


def kernel(x, p, positions, w_in, rpb, q_norm_g, kv_norm_g, w_uq, w_uk, w_uv, w_o, ln1_g, ln1_b,
           w_group, b_group, w_router, b_router, w_gate, w_up, w_down, ln2_g, ln2_b,
           w_ple, w_ple_gate, ple_norm_g):
    rope_t = _rope_table(positions)
    for i in range(DEPTH):
        x = _layer(x, p[i], rope_t, w_in[i], rpb[i], q_norm_g[i], kv_norm_g[i], w_uq[i], w_uk[i],
                   w_uv[i], w_o[i], ln1_g[i], ln1_b[i], w_group[i], b_group[i], w_router[i],
                   b_router[i], w_gate[i], w_up[i], w_down[i], ln2_g[i], ln2_b[i],
                   w_ple[i], w_ple_gate[i], ple_norm_g[i])
    return x
```

```python
import numpy as np
import jax
import jax.numpy as jnp
from jax import lax
from jax.experimental import pallas as pl
from jax.experimental.pallas import tpu as pltpu

D_MODEL = 2048
DEPTH = 1
GRID_W = 64
PLE_DIM = 256
NA_HEADS = 8
NA_HEAD_DIM = 128
NA_WIN_H = 8
NA_WIN_W = 16
NA_WIDTH = NA_HEADS * NA_HEAD_DIM
MLA_HEADS = 8
MLA_NOPE_DIM = 128
MLA_ROPE_DIM = 64
MLA_V_DIM = 128
MLA_QK_DIM = MLA_NOPE_DIM + MLA_ROPE_DIM
Q_LORA_RANK = 512
KV_LORA_RANK = 512
ROPE_THETA = 10000.0
N_GROUPS = 8
EXPERTS_PER_GROUP = 8
N_EXPERTS = N_GROUPS * EXPERTS_PER_GROUP
TOP_K_EXPERT = 2
D_EXPERT = 512
ALPHA = (2 * DEPTH) ** 0.25
LN_EPS = 1e-5
RMS_EPS = 1e-6

LANES = 128
SUBLANES = 8
VMEM_LIMIT_BYTES = 56 * 1024 * 1024

PROJ_TM = 256
NA_ROWS = 4
NA_QB = NA_ROWS * GRID_W
NA_WIN_ROWS = NA_ROWS + NA_WIN_H
NA_KB = NA_WIN_ROWS * GRID_W
MLA_TQ = 512
MLA_KC = 512
MIX_TM = 256
MOE_TB = 256
DISP_TM = 256
FIN_TM = 256
NEG_BIG = -1e30
LOG2E = 1.4426950408889634

BF16 = jnp.bfloat16
F32 = jnp.float32

_NT = (((1,), (1,)), ((), ()))


def _dot(a, b):
    return jnp.dot(a, b, preferred_element_type=F32)


def _dot_nt(a, b):
    return lax.dot_general(a, b, _NT, preferred_element_type=F32)


def _params(*sem):
    return pltpu.CompilerParams(dimension_semantics=sem, vmem_limit_bytes=VMEM_LIMIT_BYTES)


def _resident(shape):
    nd = len(shape)
    return pl.BlockSpec(shape, lambda *_: (0,) * nd, pipeline_mode=pl.Buffered(1))


def _proj_kernel(x_ref, t_ref, wqk_ref, wvt_ref, wc_ref, wuq_ref, wuk_ref, wuvt_ref, gq_ref, gkv_ref,
                 qna_ref, kna_ref, vnat_ref, qm_ref, km_ref, vmt_ref):
    xb = x_ref[0].astype(BF16)
    na_scale = NA_HEAD_DIM ** -0.5 * LOG2E
    mla_scale = MLA_QK_DIM ** -0.5 * LOG2E
    q = _dot(xb, wqk_ref[:, 0:NA_WIDTH]) * na_scale
    k = _dot(xb, wqk_ref[:, NA_WIDTH:2 * NA_WIDTH])
    vt = _dot_nt(wvt_ref[...], xb)
    for h in range(NA_HEADS):
        sl = slice(h * NA_HEAD_DIM, (h + 1) * NA_HEAD_DIM)
        qna_ref[0, h] = q[:, sl].astype(BF16)
        kna_ref[0, h] = k[:, sl].astype(BF16)
        vnat_ref[0, h] = vt[sl, :].astype(BF16)

    half = MLA_ROPE_DIM // 2
    cq = _dot(xb, wc_ref[:, 0:Q_LORA_RANK])
    ckv = _dot(xb, wc_ref[:, Q_LORA_RANK:Q_LORA_RANK + KV_LORA_RANK])
    kr = _dot(xb, wc_ref[:, Q_LORA_RANK + KV_LORA_RANK:])
    kr4 = jnp.concatenate([kr, kr[:, half:], kr[:, :half]], axis=1)

    def rms(c, g):
        return c * lax.rsqrt(jnp.mean(c * c, axis=-1, keepdims=True) + RMS_EPS) * g

    cqn = rms(cq, gq_ref[...]).astype(BF16)
    ckvn = rms(ckv, gkv_ref[...]).astype(BF16)

    t = t_ref[0]
    e = kr4 * t
    lane = lax.broadcasted_iota(jnp.int32, (1, LANES), 1)
    sign = jnp.where((lane // 32) % 2 == 0, -1.0, 1.0).astype(F32)
    kpe = (e + sign * pltpu.roll(e, 64, 1)).astype(BF16)

    qf = _dot(cqn, wuq_ref[...])
    kn = _dot(ckvn, wuk_ref[...])
    vmt = _dot_nt(wuvt_ref[...], ckvn)
    ts = t * mla_scale
    for h in range(MLA_HEADS):
        qm_ref[0, h, :, 0:128] = (qf[:, h * 256:h * 256 + 128] * mla_scale).astype(BF16)
        qm_ref[0, h, :, 128:256] = (qf[:, h * 256 + 128:(h + 1) * 256] * ts).astype(BF16)
        km_ref[0, h, :, 0:128] = kn[:, h * 128:(h + 1) * 128].astype(BF16)
        km_ref[0, h, :, 128:256] = kpe
        vmt_ref[0, h] = vmt[h * 128:(h + 1) * 128, :].astype(BF16)


def _proj(x, rope_t, wqk, wvt, wc, wuq, wuk, wuvt, gq, gkv):
    B, S, D = x.shape
    tm = PROJ_TM
    hm = lambda b, i: (b, 0, i, 0)
    hmt = lambda b, i: (b, 0, 0, i)
    out_shape = (
        jax.ShapeDtypeStruct((B, NA_HEADS, S, NA_HEAD_DIM), BF16),
        jax.ShapeDtypeStruct((B, NA_HEADS, S, NA_HEAD_DIM), BF16),
        jax.ShapeDtypeStruct((B, NA_HEADS, NA_HEAD_DIM, S), BF16),
        jax.ShapeDtypeStruct((B, MLA_HEADS, S, 256), BF16),
        jax.ShapeDtypeStruct((B, MLA_HEADS, S, 256), BF16),
        jax.ShapeDtypeStruct((B, MLA_HEADS, MLA_V_DIM, S), BF16),
    )
    return pl.pallas_call(
        _proj_kernel,
        out_shape=out_shape,
        grid=(B, S // tm),
        in_specs=[
            pl.BlockSpec((1, tm, D), lambda b, i: (b, i, 0)),
            pl.BlockSpec((1, tm, LANES), lambda b, i: (b, i, 0)),
            _resident(wqk.shape), _resident(wvt.shape), _resident(wc.shape), _resident(wuq.shape),
            _resident(wuk.shape), _resident(wuvt.shape), _resident(gq.shape), _resident(gkv.shape),
        ],
        out_specs=(
            pl.BlockSpec((1, NA_HEADS, tm, NA_HEAD_DIM), hm),
            pl.BlockSpec((1, NA_HEADS, tm, NA_HEAD_DIM), hm),
            pl.BlockSpec((1, NA_HEADS, NA_HEAD_DIM, tm), hmt),
            pl.BlockSpec((1, MLA_HEADS, tm, 256), hm),
            pl.BlockSpec((1, MLA_HEADS, tm, 256), hm),
            pl.BlockSpec((1, MLA_HEADS, MLA_V_DIM, tm), hmt),
        ),
        compiler_params=_params("arbitrary", "arbitrary"),
        name="proj",
    )(x, rope_t, wqk, wvt, wc, wuq, wuk, wuvt, gq, gkv)


def _na_bias_table(rpb):
    rows = GRID_W
    n_blocks = rows // NA_ROWS
    n_ro, n_co = 2 * NA_WIN_H - 1, 2 * NA_WIN_W - 1
    kr, qr = np.arange(NA_WIN_ROWS), np.arange(NA_ROWS)
    kc, qc = np.arange(GRID_W), np.arange(GRID_W)
    c0 = np.clip(qc - NA_WIN_W // 2, 0, GRID_W - NA_WIN_W)
    col_ok = (kc[:, None] >= c0[None, :]) & (kc[:, None] < c0[None, :] + NA_WIN_W)
    col_off = np.clip(kc[:, None] - qc[None, :] + (NA_WIN_W - 1), 0, n_co - 1)
    oh_c = np.zeros((n_co, GRID_W, GRID_W), np.float32)
    oh_c[col_off, kc[:, None], qc[None, :]] = 1.0
    oh_r = np.zeros((3, n_ro, NA_WIN_ROWS, NA_ROWS), np.float32)
    row_ok = np.zeros((3, NA_WIN_ROWS, NA_ROWS), bool)
    for c, rb in enumerate((0, 1, n_blocks - 1)):
        w0 = int(np.clip(NA_ROWS * rb - NA_WIN_H // 2, 0, rows - NA_WIN_ROWS))
        r = NA_ROWS * rb + qr
        r0 = np.clip(r - NA_WIN_H // 2, 0, rows - NA_WIN_H)
        krow = w0 + kr
        row_ok[c] = (krow[:, None] >= r0[None, :]) & (krow[:, None] < r0[None, :] + NA_WIN_H)
        row_off = np.clip(krow[:, None] - r[None, :] + (NA_WIN_H - 1), 0, n_ro - 1)
        oh_r[c, row_off, kr[:, None], qr[None, :]] = 1.0
    sel = np.zeros((NA_ROWS, n_co, GRID_W, NA_ROWS, GRID_W), np.float32)
    for r in range(NA_ROWS):
        sel[r, :, :, r, :] = oh_c
    sel = sel.reshape(NA_ROWS * n_co, GRID_W, NA_QB)
    ok = row_ok[:, :, None, :, None] & col_ok[None, None, :, None, :]
    mask = np.where(ok, 0.0, NEG_BIG).astype(np.float32).reshape(3, NA_WIN_ROWS, GRID_W, NA_QB)
    hi = lax.Precision.HIGHEST
    a = jnp.einsum('hrc,zrkq->hzkqc', rpb.astype(F32) * LOG2E, oh_r, precision=hi)
    a = a.reshape(rpb.shape[0], 3, NA_WIN_ROWS, NA_ROWS * n_co)
    tab = jnp.einsum('hzkj,jxn->hzkxn', a, sel, precision=hi) + mask[None]
    return tab.reshape(rpb.shape[0], 3, NA_KB, NA_QB)


def _na_kernel(q_ref, k_ref, vt_ref, tab_ref, o_ref):
    n_blocks = q_ref.shape[2] // NA_QB
    rows = n_blocks * NA_ROWS

    def body(rb, carry):
        w0 = jnp.clip(NA_ROWS * rb - NA_WIN_H // 2, 0, rows - NA_WIN_ROWS)
        tok0 = pl.multiple_of(w0 * GRID_W, NA_QB)
        q0 = pl.multiple_of(rb * NA_QB, NA_QB)
        cls = jnp.where(rb == 0, 0, jnp.where(rb == n_blocks - 1, 2, 1))
        qb = q_ref[0, 0, pl.ds(q0, NA_QB), :]
        kw = k_ref[0, 0, pl.ds(tok0, NA_KB), :]
        st = _dot_nt(kw, qb) + tab_ref[0, cls]
        m = jnp.max(st, axis=0, keepdims=True)
        p = jnp.exp2(st - m)
        l = jnp.sum(p, axis=0, keepdims=True)
        vw = vt_ref[0, 0, :, pl.ds(tok0, NA_KB)]
        ot = _dot(vw, p.astype(BF16)) / l
        o_ref[0, pl.ds(q0, NA_QB), :] = ot.T.astype(o_ref.dtype)
        return carry

    lax.fori_loop(0, n_blocks, body, 0, unroll=2)


def _na(q, k, vt, tab):
    B, H, S, Dh = q.shape
    return pl.pallas_call(
        _na_kernel,
        out_shape=jax.ShapeDtypeStruct((B, S, H * Dh), BF16),
        grid=(H, B),
        in_specs=[
            pl.BlockSpec((1, 1, S, Dh), lambda h, b: (b, h, 0, 0)),
            pl.BlockSpec((1, 1, S, Dh), lambda h, b: (b, h, 0, 0)),
            pl.BlockSpec((1, 1, Dh, S), lambda h, b: (b, h, 0, 0)),
            pl.BlockSpec((1, 3, NA_KB, NA_QB), lambda h, b: (h, 0, 0, 0)),
        ],
        out_specs=pl.BlockSpec((1, S, Dh), lambda h, b: (b, 0, h)),
        compiler_params=_params("arbitrary", "arbitrary"),
        name="na",
    )(q, k, vt, tab)


def _mla_kernel(q_ref, k_ref, vt_ref, o_ref, st_ref):
    q = q_ref[0, 0]
    n_chunks = k_ref.shape[2] // MLA_KC
    m = None
    for c in range(n_chunks):
        ks = slice(c * MLA_KC, (c + 1) * MLA_KC)
        st = _dot_nt(k_ref[0, 0, ks, :], q)
        st_ref[ks, :] = st
        mc = jnp.max(st, axis=0, keepdims=True)
        m = mc if c == 0 else jnp.maximum(m, mc)
    l = acc = None
    for c in range(n_chunks):
        ks = slice(c * MLA_KC, (c + 1) * MLA_KC)
        p = jnp.exp2(st_ref[ks, :] - m)
        ps = jnp.sum(p, axis=0, keepdims=True)
        pv = _dot(vt_ref[0, 0, :, ks], p.astype(BF16))
        l = ps if c == 0 else l + ps
        acc = pv if c == 0 else acc + pv
    o_ref[0] = (acc / l).T.astype(o_ref.dtype)


def _mla(q, k, vt):
    B, H, S, Dq = q.shape
    Dv = MLA_V_DIM
    tq = MLA_TQ
    return pl.pallas_call(
        _mla_kernel,
        out_shape=jax.ShapeDtypeStruct((B, S, H * Dv), BF16),
        grid=(B, H, S // tq),
        in_specs=[
            pl.BlockSpec((1, 1, tq, Dq), lambda b, h, i: (b, h, i, 0)),
            pl.BlockSpec((1, 1, S, Dq), lambda b, h, i: (b, h, 0, 0)),
            pl.BlockSpec((1, 1, Dv, S), lambda b, h, i: (b, h, 0, 0)),
        ],
        out_specs=pl.BlockSpec((1, tq, Dv), lambda b, h, i: (b, i, h)),
        scratch_shapes=[pltpu.VMEM((S, tq), F32)],
        compiler_params=_params("arbitrary", "arbitrary", "arbitrary"),
        name="mla",
    )(q, k, vt)


def _layer_norm(z, g, b):
    mu = jnp.mean(z, axis=-1, keepdims=True)
    zc = z - mu
    var = jnp.mean(zc * zc, axis=-1, keepdims=True)
    return zc * lax.rsqrt(var + LN_EPS) * g + b


def _first_lane_where(cond, lane):
    return jnp.min(jnp.where(cond, lane, LANES), axis=-1, keepdims=True)


def _mix_kernel(ona_ref, omla_ref, x_ref, wo_ref, g_ref, b_ref, wr_ref, br_ref,
                x1_ref, ri_ref, gate_ref, cnt_ref, carry_ref):
    step = pl.program_id(0)

    @pl.when(step == 0)
    def _():
        carry_ref[...] = jnp.zeros_like(carry_ref)

    mix = _dot(ona_ref[...], wo_ref[0:NA_WIDTH, :]) + _dot(omla_ref[...], wo_ref[NA_WIDTH:, :])
    x1 = _layer_norm(ALPHA * x_ref[...] + mix, g_ref[...], b_ref[...])
    x1_ref[...] = x1

    tm = x1.shape[0]
    logit = _dot(x1.astype(BF16), wr_ref[...]) + br_ref[...]
    lane = lax.broadcasted_iota(jnp.int32, (tm, LANES), 1)
    g_mask = (lane >= N_EXPERTS) & (lane < N_EXPERTS + N_GROUPS)
    gl = jnp.where(g_mask, logit, NEG_BIG)
    gmax = jnp.max(gl, axis=-1, keepdims=True)
    gsum = jnp.sum(jnp.where(g_mask, jnp.exp(gl - gmax), 0.0), axis=-1, keepdims=True)
    g_val = 1.0 / gsum
    g_idx = _first_lane_where(gl == gmax, lane) - N_EXPERTS

    e_mask = (lane < N_EXPERTS) & ((lane // EXPERTS_PER_GROUP) == g_idx)
    el = jnp.where(e_mask, logit, NEG_BIG)
    emax = jnp.max(el, axis=-1, keepdims=True)
    esum = jnp.sum(jnp.where(e_mask, jnp.exp(el - emax), 0.0), axis=-1, keepdims=True)
    i1 = _first_lane_where(el == emax, lane)
    el2 = jnp.where(lane == i1, NEG_BIG, el)
    emax2 = jnp.max(el2, axis=-1, keepdims=True)
    i2 = _first_lane_where(el2 == emax2, lane)
    v1 = 1.0 / esum
    v2 = jnp.exp(emax2 - emax) / esum
    vsum = v1 + v2
    w1 = g_val * (v1 / vsum)
    w2 = g_val * (v2 / vsum)

    oh1 = lane == i1
    oh2 = lane == i2
    oh = jnp.where(oh1 | oh2, 1.0, 0.0)
    r_i = lax.broadcasted_iota(jnp.int32, (tm, tm), 0)
    c_i = lax.broadcasted_iota(jnp.int32, (tm, tm), 1)
    tri = jnp.where(c_i < r_i, 1.0, 0.0).astype(BF16)
    before = _dot(tri, oh.astype(BF16)) + carry_ref[...]
    rank1 = jnp.sum(jnp.where(oh1, before, 0.0), axis=-1, keepdims=True).astype(jnp.int32)
    rank2 = jnp.sum(jnp.where(oh2, before, 0.0), axis=-1, keepdims=True).astype(jnp.int32)
    carry_ref[...] = carry_ref[...] + jnp.sum(oh, axis=0, keepdims=True)
    cnt_ref[...] = carry_ref[...]

    ri = jnp.where(lane == 0, i1, jnp.where(lane == 1, i2,
                   jnp.where(lane == 2, rank1, jnp.where(lane == 3, rank2, 0))))
    ri_ref[...] = ri.T[0:SUBLANES, :]
    gate_ref[...] = jnp.where(lane == 0, w1, jnp.where(lane == 1, w2, 0.0))


def _mix_out(o_na, o_mla, x, wo, g, b, wr, br):
    N, D = x.shape
    tm = MIX_TM
    row = lambda i: (i, 0)
    return pl.pallas_call(
        _mix_kernel,
        out_shape=(
            jax.ShapeDtypeStruct((N, D), F32),
            jax.ShapeDtypeStruct((SUBLANES, N), jnp.int32),
            jax.ShapeDtypeStruct((N, LANES), F32),
            jax.ShapeDtypeStruct((1, LANES), F32),
        ),
        grid=(N // tm,),
        in_specs=[
            pl.BlockSpec((tm, NA_WIDTH), row),
            pl.BlockSpec((tm, NA_WIDTH), row),
            pl.BlockSpec((tm, D), row),
            _resident(wo.shape), _resident(g.shape), _resident(b.shape),
            _resident(wr.shape), _resident(br.shape),
        ],
        out_specs=(
            pl.BlockSpec((tm, D), row),
            pl.BlockSpec((SUBLANES, tm), lambda i: (0, i)),
            pl.BlockSpec((tm, LANES), row),
            pl.BlockSpec((1, LANES), lambda i: (0, 0)),
        ),
        scratch_shapes=[pltpu.VMEM((1, LANES), F32)],
        compiler_params=_params("arbitrary"),
        name="mix_out",
    )(o_na, o_mla, x, wo, g, b, wr, br)


def _row_copy(src, src_row, dst, dst_row, sem):
    return pltpu.make_async_copy(src.at[pl.ds(src_row, 1), :], dst.at[pl.ds(dst_row, 1), :], sem)


_PAD_CHUNKS = tuple(1 << b for b in reversed(range(3, MOE_TB.bit_length() - 1)))


def _dispatch_kernel(pos_ref, fill_ref, x_ref, xs_ref, zero_ref, sem, zsem):
    tm = x_ref.shape[0]
    n_tok = pl.num_programs(0) * tm
    base = pl.program_id(0) * tm

    def pad_copies(fn):
        def per_expert(e, carry):
            start = fill_ref[e]
            n = fill_ref[N_EXPERTS + e]
            head = (-start) & (SUBLANES - 1)
            for r in range(SUBLANES - 1):
                @pl.when(r < head)
                def _(r=r):
                    fn(_row_copy(zero_ref, 0, xs_ref, start + r, zsem))
            off = start + head
            rem = n - head
            for c in _PAD_CHUNKS:
                hit = (rem & c) != 0

                @pl.when(hit)
                def _(off=off, c=c):
                    fn(pltpu.make_async_copy(zero_ref.at[pl.ds(0, c), :],
                                             xs_ref.at[pl.ds(pl.multiple_of(off, SUBLANES), c), :],
                                             zsem))
                off = off + jnp.where(hit, c, 0)
            return carry
        lax.fori_loop(0, N_EXPERTS, per_expert, 0)

        zrows = zero_ref.shape[0]

        def tail(t, carry):
            row = pl.multiple_of(fill_ref[2 * N_EXPERTS] + t * zrows, zrows)
            fn(pltpu.make_async_copy(zero_ref, xs_ref.at[pl.ds(row, zrows), :], zsem))
            return carry
        lax.fori_loop(0, (xs_ref.shape[0] - fill_ref[2 * N_EXPERTS]) // zrows, tail, 0)

    @pl.when(pl.program_id(0) == 0)
    def _():
        zero_ref[...] = jnp.zeros_like(zero_ref)
        pad_copies(lambda d: d.start())

    def issue(j, carry):
        for kk in range(TOP_K_EXPERT):
            _row_copy(x_ref, j, xs_ref, pos_ref[kk * n_tok + base + j], sem).start()
        return carry

    lax.fori_loop(0, tm, issue, 0, unroll=8)

    @pl.when(pl.program_id(0) == 0)
    def _():
        pad_copies(lambda d: d.wait())

    for kk in range(TOP_K_EXPERT):
        pltpu.make_async_copy(x_ref, xs_ref.at[pl.ds(0, tm), :], sem).wait()


def _dispatch(pos, fill, x1, n_rows):
    N, C = x1.shape
    tm = DISP_TM
    return pl.pallas_call(
        _dispatch_kernel,
        out_shape=jax.ShapeDtypeStruct((n_rows, C), x1.dtype),
        grid_spec=pltpu.PrefetchScalarGridSpec(
            num_scalar_prefetch=2,
            grid=(N // tm,),
            in_specs=[pl.BlockSpec((tm, C), lambda i, pos, fill: (i, 0))],
            out_specs=pl.BlockSpec(memory_space=pl.ANY),
            scratch_shapes=[pltpu.VMEM((_PAD_CHUNKS[0], C), x1.dtype),
                            pltpu.SemaphoreType.DMA, pltpu.SemaphoreType.DMA],
        ),
        compiler_params=_params("arbitrary"),
        name="dispatch",
    )(pos, fill, x1)


def _experts_kernel(be_ref, nxt_ref, nb_ref, xs_ref, wg_hbm, wu_hbm, wd_hbm, ys_ref,
                    wg_f, wu_f, wd_f, wg_s, wu_s, wd_s, sem):
    i = pl.program_id(0)
    active = i < nb_ref[0]
    e = be_ref[i]
    fresh = active & ((i == 0) | (e != be_ref[jnp.maximum(i - 1, 0)]))

    def fetch(ex):
        return (pltpu.make_async_copy(wg_hbm.at[ex], wg_f, sem.at[0]),
                pltpu.make_async_copy(wu_hbm.at[ex], wu_f, sem.at[1]),
                pltpu.make_async_copy(wd_hbm.at[ex], wd_f, sem.at[2]))

    @pl.when(i == 0)
    def _():
        for d in fetch(e):
            d.start()

    @pl.when(fresh)
    def _():
        for d in fetch(e):
            d.wait()
        wg_s[...] = wg_f[...].astype(BF16)
        wu_s[...] = wu_f[...].astype(BF16)
        wd_s[...] = wd_f[...].astype(BF16)
        nx = nxt_ref[i]

        @pl.when(nx >= 0)
        def _():
            for d in fetch(nx):
                d.start()

    @pl.when(active)
    def _():
        xb = xs_ref[...].astype(BF16)
        gp = _dot(xb, wg_s[...])
        up = _dot(xb, wu_s[...])
        hdn = (gp * jax.nn.sigmoid(gp) * up).astype(BF16)
        ys_ref[...] = _dot(hdn, wd_s[...])

    @pl.when(jnp.logical_not(active))
    def _():
        ys_ref[...] = jnp.zeros_like(ys_ref)


def _experts(block_e, next_e, n_used, xs, w_gate, w_up, w_down):
    P, C = xs.shape
    E, D, F = w_gate.shape
    tb = MOE_TB
    n_blocks = P // tb
    hbm = pl.BlockSpec(memory_space=pl.ANY)
    return pl.pallas_call(
        _experts_kernel,
        out_shape=jax.ShapeDtypeStruct((P, C), F32),
        grid_spec=pltpu.PrefetchScalarGridSpec(
            num_scalar_prefetch=3,
            grid=(n_blocks,),
            in_specs=[
                pl.BlockSpec((tb, C), lambda i, be, nx, nb: (jnp.minimum(i, nb[0] - 1), 0)),
                hbm, hbm, hbm,
            ],
            out_specs=pl.BlockSpec((tb, C), lambda i, be, nx, nb: (i, 0)),
            scratch_shapes=[pltpu.VMEM((D, F), F32), pltpu.VMEM((D, F), F32), pltpu.VMEM((F, D), F32),
                            pltpu.VMEM((D, F), BF16), pltpu.VMEM((D, F), BF16),
                            pltpu.VMEM((F, D), BF16), pltpu.SemaphoreType.DMA((3,))],
        ),
        compiler_params=_params("arbitrary"),
        name="experts",
    )(block_e, next_e, n_used, xs, w_gate, w_up, w_down)


def _final_kernel(pos_ref, x1_ref, gate_ref, ys_ref, p_ref, g_ref, b_ref, wpg_ref, wpe_ref,
                  gp_ref, o_ref, buf, sem):
    tm = x1_ref.shape[0]
    i = pl.program_id(0)
    last = pl.num_programs(0) - 1
    slot = i % 2

    def issue(tile, s, unroll):
        n_tok = pl.num_programs(0) * tm
        base = tile * tm

        def body(j, carry):
            for kk in range(TOP_K_EXPERT):
                _row_copy(ys_ref, pos_ref[kk * n_tok + base + j],
                          buf.at[s, kk], j, sem.at[s]).start()
            return carry
        lax.fori_loop(0, tm, body, 0, unroll=unroll)

    def drain(s):
        for kk in range(TOP_K_EXPERT):
            pltpu.make_async_copy(ys_ref.at[pl.ds(0, tm), :], buf.at[s, kk], sem.at[s]).wait()

    @pl.when(i == 0)
    def _():
        issue(0, 0, 8)

    drain(slot)
    issue(jnp.minimum(i + 1, last), 1 - slot, True)

    gates = gate_ref[...]
    g1 = gates[:, 0:1]
    g2 = gates[:, 1:2]
    ffn = buf[slot, 0] * g1 + buf[slot, 1] * g2
    x2 = _layer_norm(ALPHA * x1_ref[...] + ffn, g_ref[...], b_ref[...])

    gate = jax.nn.sigmoid(_dot(x2.astype(BF16), wpg_ref[...]))
    e = _dot(p_ref[...].astype(BF16), wpe_ref[...])
    t = e * gate
    ple = t * lax.rsqrt(jnp.mean(t * t, axis=-1, keepdims=True) + RMS_EPS) * gp_ref[...]
    o_ref[...] = x2 + ple

    @pl.when(i == last)
    def _():
        drain(1 - slot)


def _final(pos, x1, gates, ys, p, g, b, wpg, wpe, gp):
    N, D = x1.shape
    tm = FIN_TM
    row = lambda i, pos: (i, 0)
    return pl.pallas_call(
        _final_kernel,
        out_shape=jax.ShapeDtypeStruct((N, D), F32),
        grid_spec=pltpu.PrefetchScalarGridSpec(
            num_scalar_prefetch=1,
            grid=(N // tm,),
            in_specs=[
                pl.BlockSpec((tm, D), row),
                pl.BlockSpec((tm, LANES), row),
                pl.BlockSpec(memory_space=pl.ANY),
                pl.BlockSpec((tm, PLE_DIM), row),
                _resident(g.shape), _resident(b.shape), _resident(wpg.shape),
                _resident(wpe.shape), _resident(gp.shape),
            ],
            out_specs=pl.BlockSpec((tm, D), row),
            scratch_shapes=[pltpu.VMEM((2, TOP_K_EXPERT, tm, D), F32),
                            pltpu.SemaphoreType.DMA((2,))],
        ),
        compiler_params=_params("arbitrary"),
        name="final",
    )(pos, x1, gates, ys, p, g, b, wpg, wpe, gp)


def _rope_table(positions):
    inv_freq = 1.0 / (ROPE_THETA ** (jnp.arange(0, MLA_ROPE_DIM, 2, dtype=F32) / MLA_ROPE_DIM))
    ang = positions.astype(F32)[..., None] * inv_freq
    cos, sin = jnp.cos(ang), jnp.sin(ang)
    return jnp.concatenate([cos, cos, sin, sin], axis=-1)


def _layer(x, p_i, rope_t, w_in, rpb, q_norm_g, kv_norm_g, w_uq, w_uk, w_uv, w_o, ln1_g, ln1_b,
           w_group, b_group, w_router, b_router, w_gate, w_up, w_down, ln2_g, ln2_b,
           w_ple, w_ple_gate, ple_norm_g):
    B, S, D = x.shape
    N = B * S
    s3 = 3 * NA_WIDTH
    half = MLA_ROPE_DIM // 2

    wqk = w_in[:, :2 * NA_WIDTH].astype(BF16)
    wc = w_in[:, s3:].astype(BF16)
    wvt = w_in[:, 2 * NA_WIDTH:s3].T.astype(BF16)
    uq = w_uq.reshape(Q_LORA_RANK, MLA_HEADS, MLA_QK_DIM)
    nope, x1c, x2c = (uq[..., :MLA_NOPE_DIM], uq[..., MLA_NOPE_DIM:MLA_NOPE_DIM + half],
                      uq[..., MLA_NOPE_DIM + half:])
    wuq = jnp.concatenate([nope, x1c, x2c, x2c, x1c], axis=-1).reshape(Q_LORA_RANK, -1).astype(BF16)
    wuk = w_uk.astype(BF16)
    wuvt = w_uv.T.astype(BF16)
    row = lambda v: v.reshape(1, -1).astype(F32)

    q_na, k_na, v_nat, q_m, k_m, v_mt = _proj(x, rope_t, wqk, wvt, wc, wuq, wuk, wuvt,
                                               row(q_norm_g), row(kv_norm_g))
    o_na = _na(q_na, k_na, v_nat, _na_bias_table(rpb))
    o_mla = _mla(q_m, k_m, v_mt)

    pad = LANES - N_EXPERTS - N_GROUPS
    wr = jnp.concatenate([w_router, w_group, jnp.zeros((D, pad), F32)], axis=1).astype(BF16)
    br = jnp.concatenate([b_router.reshape(-1), b_group, jnp.zeros((pad,), F32)]).reshape(1, -1)
    x1, ri, gates, cnt = _mix_out(o_na.reshape(N, -1), o_mla.reshape(N, -1), x.reshape(N, D),
                                       w_o.astype(BF16), row(ln1_g), row(ln1_b), wr, br)

    tb = MOE_TB
    counts = cnt[0, :N_EXPERTS].astype(jnp.int32)
    padded = ((counts + tb - 1) // tb) * tb
    pad_ends = jnp.cumsum(padded)
    pad_starts = pad_ends - padded
    pos = (pad_starts[ri[0:2]] + ri[2:4]).reshape(-1).astype(jnp.int32)
    fill = jnp.concatenate([pad_starts + counts, padded - counts, pad_ends[-1:]]).astype(jnp.int32)
    n_blocks = (N * TOP_K_EXPERT + N_EXPERTS * (tb - 1) + tb - 1) // tb
    blk_row = jnp.arange(n_blocks, dtype=jnp.int32) * tb
    block_e = jnp.minimum(jnp.sum(pad_ends[None, :] <= blk_row[:, None], axis=1),
                          N_EXPERTS - 1).astype(jnp.int32)
    n_used = (pad_ends[-1:] // tb).astype(jnp.int32)
    nxt_blk = pad_ends[block_e] // tb
    next_e = jnp.where(nxt_blk < n_used[0], block_e[jnp.minimum(nxt_blk, n_blocks - 1)],
                       -1).astype(jnp.int32)

    xs = _dispatch(pos, fill, x1, n_blocks * tb)
    ys = _experts(block_e, next_e, n_used, xs, w_gate, w_up, w_down)
    out = _final(pos, x1, gates, ys, p_i.reshape(N, -1), row(ln2_g), row(ln2_b),
                 w_ple_gate.astype(BF16), w_ple.astype(BF16), row(ple_norm_g))
    return out.reshape(B, S, D)


def kernel(x, p, positions, w_in, rpb, q_norm_g, kv_norm_g, w_uq, w_uk, w_uv, w_o, ln1_g, ln1_b,
           w_group, b_group, w_router, b_router, w_gate, w_up, w_down, ln2_g, ln2_b,
           w_ple, w_ple_gate, ple_norm_g):
    rope_t = _rope_table(positions)
    for i in range(DEPTH):
        x = _layer(x, p[i], rope_t, w_in[i], rpb[i], q_norm_g[i], kv_norm_g[i], w_uq[i], w_uk[i],
                   w_uv[i], w_o[i], ln1_g[i], ln1_b[i], w_group[i], b_group[i], w_router[i],
                   b_router[i], w_gate[i], w_up[i], w_down[i], ln2_g[i], ln2_b[i],
                   w_ple[i], w_ple_gate[i], ple_norm_g[i])
    return x
```

```python
import numpy as np
import jax
import jax.numpy as jnp
from jax import lax
from jax.experimental import pallas as pl
from jax.experimental.pallas import tpu as pltpu

D_MODEL = 2048
DEPTH = 1
GRID_W = 64
PLE_DIM = 256
NA_HEADS = 8
NA_HEAD_DIM = 128
NA_WIN_H = 8
NA_WIN_W = 16
NA_WIDTH = NA_HEADS * NA_HEAD_DIM
MLA_HEADS = 8
MLA_NOPE_DIM = 128
MLA_ROPE_DIM = 64
MLA_V_DIM = 128
MLA_QK_DIM = MLA_NOPE_DIM + MLA_ROPE_DIM
Q_LORA_RANK = 512
KV_LORA_RANK = 512
ROPE_THETA = 10000.0
N_GROUPS = 8
EXPERTS_PER_GROUP = 8
N_EXPERTS = N_GROUPS * EXPERTS_PER_GROUP
TOP_K_EXPERT = 2
D_EXPERT = 512
ALPHA = (2 * DEPTH) ** 0.25
LN_EPS = 1e-5
RMS_EPS = 1e-6

LANES = 128
SUBLANES = 8
VMEM_LIMIT_BYTES = 56 * 1024 * 1024

PROJ_TM = 256
NA_ROWS = 4
NA_QB = NA_ROWS * GRID_W
NA_WIN_ROWS = NA_ROWS + NA_WIN_H
NA_KB = NA_WIN_ROWS * GRID_W
MLA_TQ = 512
MLA_KC = 512
MLA_TILES = 4
MIX_TM = 256
MOE_TB = 256
DISP_TM = 256
FIN_TM = 256
NEG_BIG = -1e30
LOG2E = 1.4426950408889634

BF16 = jnp.bfloat16
F32 = jnp.float32

_NT = (((1,), (1,)), ((), ()))


def _dot(a, b):
    return jnp.dot(a, b, preferred_element_type=F32)


def _dot_nt(a, b):
    return lax.dot_general(a, b, _NT, preferred_element_type=F32)


def _params(*sem):
    return pltpu.CompilerParams(dimension_semantics=sem, vmem_limit_bytes=VMEM_LIMIT_BYTES)


def _resident(shape):
    nd = len(shape)
    return pl.BlockSpec(shape, lambda *_: (0,) * nd, pipeline_mode=pl.Buffered(1))


def _proj_kernel(x_ref, t_ref, wqk_ref, wvt_ref, wc_ref, wuq_ref, wuk_ref, wuvt_ref, gq_ref, gkv_ref,
                 qna_ref, kna_ref, vnat_ref, qm_ref, km_ref, vmt_ref):
    xb = x_ref[0].astype(BF16)
    na_scale = NA_HEAD_DIM ** -0.5 * LOG2E
    mla_scale = MLA_QK_DIM ** -0.5 * LOG2E
    q = _dot(xb, wqk_ref[:, 0:NA_WIDTH]) * na_scale
    k = _dot(xb, wqk_ref[:, NA_WIDTH:2 * NA_WIDTH])
    vt = _dot_nt(wvt_ref[...], xb)
    for h in range(NA_HEADS):
        sl = slice(h * NA_HEAD_DIM, (h + 1) * NA_HEAD_DIM)
        qna_ref[0, h] = q[:, sl].astype(BF16)
        kna_ref[0, h] = k[:, sl].astype(BF16)
        vnat_ref[0, h] = vt[sl, :].astype(BF16)

    half = MLA_ROPE_DIM // 2
    cq = _dot(xb, wc_ref[:, 0:Q_LORA_RANK])
    ckv = _dot(xb, wc_ref[:, Q_LORA_RANK:Q_LORA_RANK + KV_LORA_RANK])
    kr = _dot(xb, wc_ref[:, Q_LORA_RANK + KV_LORA_RANK:])
    kr4 = jnp.concatenate([kr, kr[:, half:], kr[:, :half]], axis=1)

    def rms(c, g):
        return c * lax.rsqrt(jnp.mean(c * c, axis=-1, keepdims=True) + RMS_EPS) * g

    cqn = rms(cq, gq_ref[...]).astype(BF16)
    ckvn = rms(ckv, gkv_ref[...]).astype(BF16)

    t = t_ref[0]
    e = kr4 * t
    lane = lax.broadcasted_iota(jnp.int32, (1, LANES), 1)
    sign = jnp.where((lane // 32) % 2 == 0, -1.0, 1.0).astype(F32)
    kpe = (e + sign * pltpu.roll(e, 64, 1)).astype(BF16)

    qf = _dot(cqn, wuq_ref[...])
    kn = _dot(ckvn, wuk_ref[...])
    vmt = _dot_nt(wuvt_ref[...], ckvn)
    ts = t * mla_scale
    for h in range(MLA_HEADS):
        qm_ref[0, h, :, 0:128] = (qf[:, h * 256:h * 256 + 128] * mla_scale).astype(BF16)
        qm_ref[0, h, :, 128:256] = (qf[:, h * 256 + 128:(h + 1) * 256] * ts).astype(BF16)
        km_ref[0, h, :, 0:128] = kn[:, h * 128:(h + 1) * 128].astype(BF16)
        km_ref[0, h, :, 128:256] = kpe
        vmt_ref[0, h] = vmt[h * 128:(h + 1) * 128, :].astype(BF16)


def _proj(x, rope_t, wqk, wvt, wc, wuq, wuk, wuvt, gq, gkv):
    B, S, D = x.shape
    tm = PROJ_TM
    hm = lambda b, i: (b, 0, i, 0)
    hmt = lambda b, i: (b, 0, 0, i)
    out_shape = (
        jax.ShapeDtypeStruct((B, NA_HEADS, S, NA_HEAD_DIM), BF16),
        jax.ShapeDtypeStruct((B, NA_HEADS, S, NA_HEAD_DIM), BF16),
        jax.ShapeDtypeStruct((B, NA_HEADS, NA_HEAD_DIM, S), BF16),
        jax.ShapeDtypeStruct((B, MLA_HEADS, S, 256), BF16),
        jax.ShapeDtypeStruct((B, MLA_HEADS, S, 256), BF16),
        jax.ShapeDtypeStruct((B, MLA_HEADS, MLA_V_DIM, S), BF16),
    )
    return pl.pallas_call(
        _proj_kernel,
        out_shape=out_shape,
        grid=(B, S // tm),
        in_specs=[
            pl.BlockSpec((1, tm, D), lambda b, i: (b, i, 0)),
            pl.BlockSpec((1, tm, LANES), lambda b, i: (b, i, 0)),
            _resident(wqk.shape), _resident(wvt.shape), _resident(wc.shape), _resident(wuq.shape),
            _resident(wuk.shape), _resident(wuvt.shape), _resident(gq.shape), _resident(gkv.shape),
        ],
        out_specs=(
            pl.BlockSpec((1, NA_HEADS, tm, NA_HEAD_DIM), hm),
            pl.BlockSpec((1, NA_HEADS, tm, NA_HEAD_DIM), hm),
            pl.BlockSpec((1, NA_HEADS, NA_HEAD_DIM, tm), hmt),
            pl.BlockSpec((1, MLA_HEADS, tm, 256), hm),
            pl.BlockSpec((1, MLA_HEADS, tm, 256), hm),
            pl.BlockSpec((1, MLA_HEADS, MLA_V_DIM, tm), hmt),
        ),
        compiler_params=_params("arbitrary", "arbitrary"),
        name="proj",
    )(x, rope_t, wqk, wvt, wc, wuq, wuk, wuvt, gq, gkv)


def _na_bias_table(rpb):
    rows = GRID_W
    n_blocks = rows // NA_ROWS
    n_ro, n_co = 2 * NA_WIN_H - 1, 2 * NA_WIN_W - 1
    kr, qr = np.arange(NA_WIN_ROWS), np.arange(NA_ROWS)
    kc, qc = np.arange(GRID_W), np.arange(GRID_W)
    c0 = np.clip(qc - NA_WIN_W // 2, 0, GRID_W - NA_WIN_W)
    col_ok = (kc[:, None] >= c0[None, :]) & (kc[:, None] < c0[None, :] + NA_WIN_W)
    col_off = np.clip(kc[:, None] - qc[None, :] + (NA_WIN_W - 1), 0, n_co - 1)
    oh_c = np.zeros((n_co, GRID_W, GRID_W), np.float32)
    oh_c[col_off, kc[:, None], qc[None, :]] = 1.0
    oh_r = np.zeros((3, n_ro, NA_WIN_ROWS, NA_ROWS), np.float32)
    row_ok = np.zeros((3, NA_WIN_ROWS, NA_ROWS), bool)
    for c, rb in enumerate((0, 1, n_blocks - 1)):
        w0 = int(np.clip(NA_ROWS * rb - NA_WIN_H // 2, 0, rows - NA_WIN_ROWS))
        r = NA_ROWS * rb + qr
        r0 = np.clip(r - NA_WIN_H // 2, 0, rows - NA_WIN_H)
        krow = w0 + kr
        row_ok[c] = (krow[:, None] >= r0[None, :]) & (krow[:, None] < r0[None, :] + NA_WIN_H)
        row_off = np.clip(krow[:, None] - r[None, :] + (NA_WIN_H - 1), 0, n_ro - 1)
        oh_r[c, row_off, kr[:, None], qr[None, :]] = 1.0
    sel = np.zeros((NA_ROWS, n_co, GRID_W, NA_ROWS, GRID_W), np.float32)
    for r in range(NA_ROWS):
        sel[r, :, :, r, :] = oh_c
    sel = sel.reshape(NA_ROWS * n_co, GRID_W, NA_QB)
    ok = row_ok[:, :, None, :, None] & col_ok[None, None, :, None, :]
    mask = np.where(ok, 0.0, NEG_BIG).astype(np.float32).reshape(3, NA_WIN_ROWS, GRID_W, NA_QB)
    hi = lax.Precision.HIGHEST
    a = jnp.einsum('hrc,zrkq->hzkqc', rpb.astype(F32) * LOG2E, oh_r, precision=hi)
    a = a.reshape(rpb.shape[0], 3, NA_WIN_ROWS, NA_ROWS * n_co)
    tab = jnp.einsum('hzkj,jxn->hzkxn', a, sel, precision=hi) + mask[None]
    return tab.reshape(rpb.shape[0], 3, NA_KB, NA_QB)


def _na_kernel(q_ref, k_ref, vt_ref, tab_ref, o_ref, st_ref, m_ref):
    n_blocks = q_ref.shape[2] // NA_QB
    rows = n_blocks * NA_ROWS

    def window(rb):
        w0 = jnp.clip(NA_ROWS * rb - NA_WIN_H // 2, 0, rows - NA_WIN_ROWS)
        return pl.multiple_of(w0 * GRID_W, NA_QB)

    def scores(rb, slot):
        tok0 = window(rb)
        q0 = pl.multiple_of(rb * NA_QB, NA_QB)
        cls = jnp.where(rb == 0, 0, jnp.where(rb == n_blocks - 1, 2, 1))
        st = _dot_nt(k_ref[0, 0, pl.ds(tok0, NA_KB), :],
                     q_ref[0, 0, pl.ds(q0, NA_QB), :]) + tab_ref[0, cls]
        st_ref[slot] = st
        m_ref[slot] = jnp.max(st, axis=0, keepdims=True)

    def output(rb, slot):
        tok0 = window(rb)
        q0 = pl.multiple_of(rb * NA_QB, NA_QB)
        p = jnp.exp2(st_ref[slot] - m_ref[slot])
        l = jnp.sum(p, axis=0, keepdims=True)
        ot = _dot(vt_ref[0, 0, :, pl.ds(tok0, NA_KB)], p.astype(BF16)) / l
        o_ref[0, pl.ds(q0, NA_QB), :] = ot.T.astype(o_ref.dtype)

    scores(0, 0)

    def body(t, carry):
        rb = 2 * t + 1
        scores(rb, 1)
        output(rb - 1, 0)
        scores(rb + 1, 0)
        output(rb, 1)
        return carry

    lax.fori_loop(0, n_blocks // 2 - 1, body, 0)
    scores(n_blocks - 1, 1)
    output(n_blocks - 2, 0)
    output(n_blocks - 1, 1)


def _na(q, k, vt, tab):
    B, H, S, Dh = q.shape
    return pl.pallas_call(
        _na_kernel,
        out_shape=jax.ShapeDtypeStruct((B, S, H * Dh), BF16),
        grid=(H, B),
        in_specs=[
            pl.BlockSpec((1, 1, S, Dh), lambda h, b: (b, h, 0, 0)),
            pl.BlockSpec((1, 1, S, Dh), lambda h, b: (b, h, 0, 0)),
            pl.BlockSpec((1, 1, Dh, S), lambda h, b: (b, h, 0, 0)),
            pl.BlockSpec((1, 3, NA_KB, NA_QB), lambda h, b: (h, 0, 0, 0)),
        ],
        out_specs=pl.BlockSpec((1, S, Dh), lambda h, b: (b, 0, h)),
        scratch_shapes=[pltpu.VMEM((2, NA_KB, NA_QB), F32), pltpu.VMEM((2, 1, NA_QB), F32)],
        compiler_params=_params("arbitrary", "arbitrary"),
        name="na",
    )(q, k, vt, tab)


def _mla_kernel(q_ref, k_ref, vt_ref, o_ref, st_ref):
    n_chunks = k_ref.shape[2] // MLA_KC
    n_tiles = q_ref.shape[2] // MLA_TQ

    def pass1(j, slot):
        q = q_ref[0, 0, j * MLA_TQ:(j + 1) * MLA_TQ, :]
        m = None
        for c in range(n_chunks):
            ks = slice(c * MLA_KC, (c + 1) * MLA_KC)
            st = _dot_nt(k_ref[0, 0, ks, :], q)
            st_ref[slot, ks, :] = st
            mc = jnp.max(st, axis=0, keepdims=True)
            m = mc if c == 0 else jnp.maximum(m, mc)
        return m

    def pass2(j, slot, m):
        l = acc = None
        for c in range(n_chunks):
            ks = slice(c * MLA_KC, (c + 1) * MLA_KC)
            p = jnp.exp2(st_ref[slot, ks, :] - m)
            ps = jnp.sum(p, axis=0, keepdims=True)
            pv = _dot(vt_ref[0, 0, :, ks], p.astype(BF16))
            l = ps if c == 0 else l + ps
            acc = pv if c == 0 else acc + pv
        o_ref[0, j * MLA_TQ:(j + 1) * MLA_TQ, :] = (acc / l).T.astype(o_ref.dtype)

    m_prev = pass1(0, 0)
    for j in range(1, n_tiles):
        m_cur = pass1(j, j % 2)
        pass2(j - 1, (j - 1) % 2, m_prev)
        m_prev = m_cur
    pass2(n_tiles - 1, (n_tiles - 1) % 2, m_prev)


def _mla(q, k, vt):
    B, H, S, Dq = q.shape
    Dv = MLA_V_DIM
    tq = MLA_TQ * MLA_TILES
    return pl.pallas_call(
        _mla_kernel,
        out_shape=jax.ShapeDtypeStruct((B, S, H * Dv), BF16),
        grid=(B, H, S // tq),
        in_specs=[
            pl.BlockSpec((1, 1, tq, Dq), lambda b, h, i: (b, h, i, 0)),
            pl.BlockSpec((1, 1, S, Dq), lambda b, h, i: (b, h, 0, 0)),
            pl.BlockSpec((1, 1, Dv, S), lambda b, h, i: (b, h, 0, 0)),
        ],
        out_specs=pl.BlockSpec((1, tq, Dv), lambda b, h, i: (b, i, h)),
        scratch_shapes=[pltpu.VMEM((2, S, MLA_TQ), F32)],
        compiler_params=_params("arbitrary", "arbitrary", "arbitrary"),
        name="mla",
    )(q, k, vt)


def _layer_norm(z, g, b):
    mu = jnp.mean(z, axis=-1, keepdims=True)
    zc = z - mu
    var = jnp.mean(zc * zc, axis=-1, keepdims=True)
    return zc * lax.rsqrt(var + LN_EPS) * g + b


def _first_lane_where(cond, lane):
    return jnp.min(jnp.where(cond, lane, LANES), axis=-1, keepdims=True)


def _mix_kernel(ona_ref, omla_ref, x_ref, wo_ref, g_ref, b_ref, wr_ref, br_ref,
                x1_ref, ri_ref, gate_ref, cnt_ref, carry_ref):
    step = pl.program_id(0)

    @pl.when(step == 0)
    def _():
        carry_ref[...] = jnp.zeros_like(carry_ref)

    mix = _dot(ona_ref[...], wo_ref[0:NA_WIDTH, :]) + _dot(omla_ref[...], wo_ref[NA_WIDTH:, :])
    x1 = _layer_norm(ALPHA * x_ref[...] + mix, g_ref[...], b_ref[...])
    x1_ref[...] = x1

    tm = x1.shape[0]
    logit = _dot(x1.astype(BF16), wr_ref[...]) + br_ref[...]
    lane = lax.broadcasted_iota(jnp.int32, (tm, LANES), 1)
    g_mask = (lane >= N_EXPERTS) & (lane < N_EXPERTS + N_GROUPS)
    gl = jnp.where(g_mask, logit, NEG_BIG)
    gmax = jnp.max(gl, axis=-1, keepdims=True)
    gsum = jnp.sum(jnp.where(g_mask, jnp.exp(gl - gmax), 0.0), axis=-1, keepdims=True)
    g_val = 1.0 / gsum
    g_idx = _first_lane_where(gl == gmax, lane) - N_EXPERTS

    e_mask = (lane < N_EXPERTS) & ((lane // EXPERTS_PER_GROUP) == g_idx)
    el = jnp.where(e_mask, logit, NEG_BIG)
    emax = jnp.max(el, axis=-1, keepdims=True)
    esum = jnp.sum(jnp.where(e_mask, jnp.exp(el - emax), 0.0), axis=-1, keepdims=True)
    i1 = _first_lane_where(el == emax, lane)
    el2 = jnp.where(lane == i1, NEG_BIG, el)
    emax2 = jnp.max(el2, axis=-1, keepdims=True)
    i2 = _first_lane_where(el2 == emax2, lane)
    v1 = 1.0 / esum
    v2 = jnp.exp(emax2 - emax) / esum
    vsum = v1 + v2
    w1 = g_val * (v1 / vsum)
    w2 = g_val * (v2 / vsum)

    oh1 = lane == i1
    oh2 = lane == i2
    oh = jnp.where(oh1 | oh2, 1.0, 0.0)
    r_i = lax.broadcasted_iota(jnp.int32, (tm, tm), 0)
    c_i = lax.broadcasted_iota(jnp.int32, (tm, tm), 1)
    tri = jnp.where(c_i < r_i, 1.0, 0.0).astype(BF16)
    before = _dot(tri, oh.astype(BF16)) + carry_ref[...]
    rank1 = jnp.sum(jnp.where(oh1, before, 0.0), axis=-1, keepdims=True).astype(jnp.int32)
    rank2 = jnp.sum(jnp.where(oh2, before, 0.0), axis=-1, keepdims=True).astype(jnp.int32)
    carry_ref[...] = carry_ref[...] + jnp.sum(oh, axis=0, keepdims=True)
    cnt_ref[...] = carry_ref[...]

    ri = jnp.where(lane == 0, i1, jnp.where(lane == 1, i2,
                   jnp.where(lane == 2, rank1, jnp.where(lane == 3, rank2, 0))))
    ri_ref[...] = ri.T[0:SUBLANES, :]
    gate_ref[...] = jnp.where(lane == 0, w1, jnp.where(lane == 1, w2, 0.0))


def _mix_out(o_na, o_mla, x, wo, g, b, wr, br):
    N, D = x.shape
    tm = MIX_TM
    row = lambda i: (i, 0)
    return pl.pallas_call(
        _mix_kernel,
        out_shape=(
            jax.ShapeDtypeStruct((N, D), F32),
            jax.ShapeDtypeStruct((SUBLANES, N), jnp.int32),
            jax.ShapeDtypeStruct((N, LANES), F32),
            jax.ShapeDtypeStruct((1, LANES), F32),
        ),
        grid=(N // tm,),
        in_specs=[
            pl.BlockSpec((tm, NA_WIDTH), row),
            pl.BlockSpec((tm, NA_WIDTH), row),
            pl.BlockSpec((tm, D), row),
            _resident(wo.shape), _resident(g.shape), _resident(b.shape),
            _resident(wr.shape), _resident(br.shape),
        ],
        out_specs=(
            pl.BlockSpec((tm, D), row),
            pl.BlockSpec((SUBLANES, tm), lambda i: (0, i)),
            pl.BlockSpec((tm, LANES), row),
            pl.BlockSpec((1, LANES), lambda i: (0, 0)),
        ),
        scratch_shapes=[pltpu.VMEM((1, LANES), F32)],
        compiler_params=_params("arbitrary"),
        name="mix_out",
    )(o_na, o_mla, x, wo, g, b, wr, br)


def _row_copy(src, src_row, dst, dst_row, sem):
    return pltpu.make_async_copy(src.at[pl.ds(src_row, 1), :], dst.at[pl.ds(dst_row, 1), :], sem)


_PAD_CHUNKS = tuple(1 << b for b in reversed(range(3, MOE_TB.bit_length() - 1)))


def _dispatch_kernel(pos_ref, fill_ref, x_ref, xs_ref, zero_ref, sem, zsem):
    tm = x_ref.shape[0]
    n_tok = pl.num_programs(0) * tm
    base = pl.program_id(0) * tm

    def pad_copies(fn):
        def per_expert(e, carry):
            start = fill_ref[e]
            n = fill_ref[N_EXPERTS + e]
            head = (-start) & (SUBLANES - 1)
            for r in range(SUBLANES - 1):
                @pl.when(r < head)
                def _(r=r):
                    fn(_row_copy(zero_ref, 0, xs_ref, start + r, zsem))
            off = start + head
            rem = n - head
            for c in _PAD_CHUNKS:
                hit = (rem & c) != 0

                @pl.when(hit)
                def _(off=off, c=c):
                    fn(pltpu.make_async_copy(zero_ref.at[pl.ds(0, c), :],
                                             xs_ref.at[pl.ds(pl.multiple_of(off, SUBLANES), c), :],
                                             zsem))
                off = off + jnp.where(hit, c, 0)
            return carry
        lax.fori_loop(0, N_EXPERTS, per_expert, 0)

        zrows = zero_ref.shape[0]

        def tail(t, carry):
            row = pl.multiple_of(fill_ref[2 * N_EXPERTS] + t * zrows, zrows)
            fn(pltpu.make_async_copy(zero_ref, xs_ref.at[pl.ds(row, zrows), :], zsem))
            return carry
        lax.fori_loop(0, (xs_ref.shape[0] - fill_ref[2 * N_EXPERTS]) // zrows, tail, 0)

    @pl.when(pl.program_id(0) == 0)
    def _():
        zero_ref[...] = jnp.zeros_like(zero_ref)
        pad_copies(lambda d: d.start())

    def issue(j, carry):
        for kk in range(TOP_K_EXPERT):
            _row_copy(x_ref, j, xs_ref, pos_ref[kk * n_tok + base + j], sem).start()
        return carry

    lax.fori_loop(0, tm, issue, 0, unroll=8)

    @pl.when(pl.program_id(0) == 0)
    def _():
        pad_copies(lambda d: d.wait())

    for kk in range(TOP_K_EXPERT):
        pltpu.make_async_copy(x_ref, xs_ref.at[pl.ds(0, tm), :], sem).wait()


def _dispatch(pos, fill, x1, n_rows):
    N, C = x1.shape
    tm = DISP_TM
    return pl.pallas_call(
        _dispatch_kernel,
        out_shape=jax.ShapeDtypeStruct((n_rows, C), x1.dtype),
        grid_spec=pltpu.PrefetchScalarGridSpec(
            num_scalar_prefetch=2,
            grid=(N // tm,),
            in_specs=[pl.BlockSpec((tm, C), lambda i, pos, fill: (i, 0))],
            out_specs=pl.BlockSpec(memory_space=pl.ANY),
            scratch_shapes=[pltpu.VMEM((_PAD_CHUNKS[0], C), x1.dtype),
                            pltpu.SemaphoreType.DMA, pltpu.SemaphoreType.DMA],
        ),
        compiler_params=_params("arbitrary"),
        name="dispatch",
    )(pos, fill, x1)


def _experts_kernel(be_ref, nxt_ref, nb_ref, xs_ref, wg_hbm, wu_hbm, wd_hbm, ys_ref,
                    wg_f, wu_f, wd_f, wg_s, wu_s, wd_s, sem):
    i = pl.program_id(0)
    active = i < nb_ref[0]
    e = be_ref[i]
    fresh = active & ((i == 0) | (e != be_ref[jnp.maximum(i - 1, 0)]))

    def fetch(ex):
        return (pltpu.make_async_copy(wg_hbm.at[ex], wg_f, sem.at[0]),
                pltpu.make_async_copy(wu_hbm.at[ex], wu_f, sem.at[1]),
                pltpu.make_async_copy(wd_hbm.at[ex], wd_f, sem.at[2]))

    @pl.when(i == 0)
    def _():
        for d in fetch(e):
            d.start()

    @pl.when(fresh)
    def _():
        for d in fetch(e):
            d.wait()
        wg_s[...] = wg_f[...].astype(BF16)
        wu_s[...] = wu_f[...].astype(BF16)
        wd_s[...] = wd_f[...].astype(BF16)
        nx = nxt_ref[i]

        @pl.when(nx >= 0)
        def _():
            for d in fetch(nx):
                d.start()

    @pl.when(active)
    def _():
        xb = xs_ref[...].astype(BF16)
        gp = _dot(xb, wg_s[...])
        up = _dot(xb, wu_s[...])
        hdn = (gp * jax.nn.sigmoid(gp) * up).astype(BF16)
        ys_ref[...] = _dot(hdn, wd_s[...])

    @pl.when(jnp.logical_not(active))
    def _():
        ys_ref[...] = jnp.zeros_like(ys_ref)


def _experts(block_e, next_e, n_used, xs, w_gate, w_up, w_down):
    P, C = xs.shape
    E, D, F = w_gate.shape
    tb = MOE_TB
    n_blocks = P // tb
    hbm = pl.BlockSpec(memory_space=pl.ANY)
    return pl.pallas_call(
        _experts_kernel,
        out_shape=jax.ShapeDtypeStruct((P, C), F32),
        grid_spec=pltpu.PrefetchScalarGridSpec(
            num_scalar_prefetch=3,
            grid=(n_blocks,),
            in_specs=[
                pl.BlockSpec((tb, C), lambda i, be, nx, nb: (jnp.minimum(i, nb[0] - 1), 0)),
                hbm, hbm, hbm,
            ],
            out_specs=pl.BlockSpec((tb, C), lambda i, be, nx, nb: (i, 0)),
            scratch_shapes=[pltpu.VMEM((D, F), F32), pltpu.VMEM((D, F), F32), pltpu.VMEM((F, D), F32),
                            pltpu.VMEM((D, F), BF16), pltpu.VMEM((D, F), BF16),
                            pltpu.VMEM((F, D), BF16), pltpu.SemaphoreType.DMA((3,))],
        ),
        compiler_params=_params("arbitrary"),
        name="experts",
    )(block_e, next_e, n_used, xs, w_gate, w_up, w_down)


def _final_kernel(pos_ref, x1_ref, gate_ref, ys_ref, p_ref, g_ref, b_ref, wpg_ref, wpe_ref,
                  gp_ref, o_ref, buf, sem):
    tm = x1_ref.shape[0]
    i = pl.program_id(0)
    last = pl.num_programs(0) - 1
    slot = i % 2

    def issue(tile, s, unroll):
        n_tok = pl.num_programs(0) * tm
        base = tile * tm

        def body(j, carry):
            for kk in range(TOP_K_EXPERT):
                _row_copy(ys_ref, pos_ref[kk * n_tok + base + j],
                          buf.at[s, kk], j, sem.at[s]).start()
            return carry
        lax.fori_loop(0, tm, body, 0, unroll=unroll)

    def drain(s):
        for kk in range(TOP_K_EXPERT):
            pltpu.make_async_copy(ys_ref.at[pl.ds(0, tm), :], buf.at[s, kk], sem.at[s]).wait()

    @pl.when(i == 0)
    def _():
        issue(0, 0, 8)

    drain(slot)
    issue(jnp.minimum(i + 1, last), 1 - slot, True)

    gates = gate_ref[...]
    g1 = gates[:, 0:1]
    g2 = gates[:, 1:2]
    ffn = buf[slot, 0] * g1 + buf[slot, 1] * g2
    x2 = _layer_norm(ALPHA * x1_ref[...] + ffn, g_ref[...], b_ref[...])

    gate = jax.nn.sigmoid(_dot(x2.astype(BF16), wpg_ref[...]))
    e = _dot(p_ref[...].astype(BF16), wpe_ref[...])
    t = e * gate
    ple = t * lax.rsqrt(jnp.mean(t * t, axis=-1, keepdims=True) + RMS_EPS) * gp_ref[...]
    o_ref[...] = x2 + ple

    @pl.when(i == last)
    def _():
        drain(1 - slot)


def _final(pos, x1, gates, ys, p, g, b, wpg, wpe, gp):
    N, D = x1.shape
    tm = FIN_TM
    row = lambda i, pos: (i, 0)
    return pl.pallas_call(
        _final_kernel,
        out_shape=jax.ShapeDtypeStruct((N, D), F32),
        grid_spec=pltpu.PrefetchScalarGridSpec(
            num_scalar_prefetch=1,
            grid=(N // tm,),
            in_specs=[
                pl.BlockSpec((tm, D), row),
                pl.BlockSpec((tm, LANES), row),
                pl.BlockSpec(memory_space=pl.ANY),
                pl.BlockSpec((tm, PLE_DIM), row),
                _resident(g.shape), _resident(b.shape), _resident(wpg.shape),
                _resident(wpe.shape), _resident(gp.shape),
            ],
            out_specs=pl.BlockSpec((tm, D), row),
            scratch_shapes=[pltpu.VMEM((2, TOP_K_EXPERT, tm, D), F32),
                            pltpu.SemaphoreType.DMA((2,))],
        ),
        compiler_params=_params("arbitrary"),
        name="final",
    )(pos, x1, gates, ys, p, g, b, wpg, wpe, gp)


def _rope_table(positions):
    inv_freq = 1.0 / (ROPE_THETA ** (jnp.arange(0, MLA_ROPE_DIM, 2, dtype=F32) / MLA_ROPE_DIM))
    ang = positions.astype(F32)[..., None] * inv_freq
    cos, sin = jnp.cos(ang), jnp.sin(ang)
    return jnp.concatenate([cos, cos, sin, sin], axis=-1)


def _layer(x, p_i, rope_t, w_in, rpb, q_norm_g, kv_norm_g, w_uq, w_uk, w_uv, w_o, ln1_g, ln1_b,
           w_group, b_group, w_router, b_router, w_gate, w_up, w_down, ln2_g, ln2_b,
           w_ple, w_ple_gate, ple_norm_g):
    B, S, D = x.shape
    N = B * S
    s3 = 3 * NA_WIDTH
    half = MLA_ROPE_DIM // 2

    wqk = w_in[:, :2 * NA_WIDTH].astype(BF16)
    wc = w_in[:, s3:].astype(BF16)
    wvt = w_in[:, 2 * NA_WIDTH:s3].T.astype(BF16)
    uq = w_uq.reshape(Q_LORA_RANK, MLA_HEADS, MLA_QK_DIM)
    nope, x1c, x2c = (uq[..., :MLA_NOPE_DIM], uq[..., MLA_NOPE_DIM:MLA_NOPE_DIM + half],
                      uq[..., MLA_NOPE_DIM + half:])
    wuq = jnp.concatenate([nope, x1c, x2c, x2c, x1c], axis=-1).reshape(Q_LORA_RANK, -1).astype(BF16)
    wuk = w_uk.astype(BF16)
    wuvt = w_uv.T.astype(BF16)
    row = lambda v: v.reshape(1, -1).astype(F32)

    q_na, k_na, v_nat, q_m, k_m, v_mt = _proj(x, rope_t, wqk, wvt, wc, wuq, wuk, wuvt,
                                               row(q_norm_g), row(kv_norm_g))
    o_na = _na(q_na, k_na, v_nat, _na_bias_table(rpb))
    o_mla = _mla(q_m, k_m, v_mt)

    pad = LANES - N_EXPERTS - N_GROUPS
    wr = jnp.concatenate([w_router, w_group, jnp.zeros((D, pad), F32)], axis=1).astype(BF16)
    br = jnp.concatenate([b_router.reshape(-1), b_group, jnp.zeros((pad,), F32)]).reshape(1, -1)
    x1, ri, gates, cnt = _mix_out(o_na.reshape(N, -1), o_mla.reshape(N, -1), x.reshape(N, D),
                                       w_o.astype(BF16), row(ln1_g), row(ln1_b), wr, br)

    tb = MOE_TB
    counts = cnt[0, :N_EXPERTS].astype(jnp.int32)
    padded = ((counts + tb - 1) // tb) * tb
    pad_ends = jnp.cumsum(padded)
    pad_starts = pad_ends - padded
    eid = ri[0:2]
    pos = ri[2:4]
    for e in range(N_EXPERTS):
        pos = pos + jnp.where(eid == e, pad_starts[e], 0)
    pos = pos.reshape(-1).astype(jnp.int32)
    fill = jnp.concatenate([pad_starts + counts, padded - counts, pad_ends[-1:]]).astype(jnp.int32)
    n_blocks = (N * TOP_K_EXPERT + N_EXPERTS * (tb - 1) + tb - 1) // tb
    blk_row = jnp.arange(n_blocks, dtype=jnp.int32) * tb
    block_e = jnp.minimum(jnp.sum(pad_ends[None, :] <= blk_row[:, None], axis=1),
                          N_EXPERTS - 1).astype(jnp.int32)
    n_used = (pad_ends[-1:] // tb).astype(jnp.int32)
    nxt_blk = pad_ends[block_e] // tb
    next_e = jnp.where(nxt_blk < n_used[0], block_e[jnp.minimum(nxt_blk, n_blocks - 1)],
                       -1).astype(jnp.int32)

    xs = _dispatch(pos, fill, x1, n_blocks * tb)
    ys = _experts(block_e, next_e, n_used, xs, w_gate, w_up, w_down)
    out = _final(pos, x1, gates, ys, p_i.reshape(N, -1), row(ln2_g), row(ln2_b),
                 w_ple_gate.astype(BF16), w_ple.astype(BF16), row(ple_norm_g))
    return out.reshape(B, S, D)


def kernel(x, p, positions, w_in, rpb, q_norm_g, kv_norm_g, w_uq, w_uk, w_uv, w_o, ln1_g, ln1_b,
           w_group, b_group, w_router, b_router, w_gate, w_up, w_down, ln2_g, ln2_b,
           w_ple, w_ple_gate, ple_norm_g):
    rope_t = _rope_table(positions)
    for i in range(DEPTH):
        x = _layer(x, p[i], rope_t, w_in[i], rpb[i], q_norm_g[i], kv_norm_g[i], w_uq[i], w_uk[i],
                   w_uv[i], w_o[i], ln1_g[i], ln1_b[i], w_group[i], b_group[i], w_router[i],
                   b_router[i], w_gate[i], w_up[i], w_down[i], ln2_g[i], ln2_b[i],
                   w_ple[i], w_ple_gate[i], ple_norm_g[i])
    return x
```

```python
import numpy as np
import jax
import jax.numpy as jnp
from jax import lax
from jax.experimental import pallas as pl
from jax.experimental.pallas import tpu as pltpu

D_MODEL = 2048
DEPTH = 1
GRID_W = 64
PLE_DIM = 256
NA_HEADS = 8
NA_HEAD_DIM = 128
NA_WIN_H = 8
NA_WIN_W = 16
NA_WIDTH = NA_HEADS * NA_HEAD_DIM
MLA_HEADS = 8
MLA_NOPE_DIM = 128
MLA_ROPE_DIM = 64
MLA_V_DIM = 128
MLA_QK_DIM = MLA_NOPE_DIM + MLA_ROPE_DIM
Q_LORA_RANK = 512
KV_LORA_RANK = 512
ROPE_THETA = 10000.0
N_GROUPS = 8
EXPERTS_PER_GROUP = 8
N_EXPERTS = N_GROUPS * EXPERTS_PER_GROUP
TOP_K_EXPERT = 2
D_EXPERT = 512
ALPHA = (2 * DEPTH) ** 0.25
LN_EPS = 1e-5
RMS_EPS = 1e-6

LANES = 128
SUBLANES = 8
VMEM_LIMIT_BYTES = 56 * 1024 * 1024

PROJ_TM = 256
NA_ROWS = 4
NA_QB = NA_ROWS * GRID_W
NA_WIN_ROWS = NA_ROWS + NA_WIN_H
NA_KB = NA_WIN_ROWS * GRID_W
MLA_TQ = 256
MLA_KC = 1024
MLA_TILES = 8
MIX_TM = 512
MIX_SUB = 256
MOE_TB = 256
DISP_TM = 256
FIN_TM = 256
NEG_BIG = -1e30
LOG2E = 1.4426950408889634

BF16 = jnp.bfloat16
F32 = jnp.float32

_NT = (((1,), (1,)), ((), ()))


def _dot(a, b):
    return jnp.dot(a, b, preferred_element_type=F32)


def _dot_nt(a, b):
    return lax.dot_general(a, b, _NT, preferred_element_type=F32)


def _params(*sem):
    return pltpu.CompilerParams(dimension_semantics=sem, vmem_limit_bytes=VMEM_LIMIT_BYTES)


def _resident(shape):
    nd = len(shape)
    return pl.BlockSpec(shape, lambda *_: (0,) * nd, pipeline_mode=pl.Buffered(1))


def _proj_kernel(x_ref, t_ref, wqk_ref, wvt_ref, wc_ref, wuq_ref, wuk_ref, wuvt_ref, gq_ref, gkv_ref,
                 qna_ref, kna_ref, vnat_ref, qm_ref, km_ref, vmt_ref):
    xb = x_ref[0].astype(BF16)
    na_scale = NA_HEAD_DIM ** -0.5 * LOG2E
    mla_scale = MLA_QK_DIM ** -0.5 * LOG2E
    q = _dot(xb, wqk_ref[:, 0:NA_WIDTH]) * na_scale
    k = _dot(xb, wqk_ref[:, NA_WIDTH:2 * NA_WIDTH])
    vt = _dot_nt(wvt_ref[...], xb)
    for h in range(NA_HEADS):
        sl = slice(h * NA_HEAD_DIM, (h + 1) * NA_HEAD_DIM)
        qna_ref[0, h] = q[:, sl].astype(BF16)
        kna_ref[0, h] = k[:, sl].astype(BF16)
        vnat_ref[0, h] = vt[sl, :].astype(BF16)

    half = MLA_ROPE_DIM // 2
    cq = _dot(xb, wc_ref[:, 0:Q_LORA_RANK])
    ckv = _dot(xb, wc_ref[:, Q_LORA_RANK:Q_LORA_RANK + KV_LORA_RANK])
    kr = _dot(xb, wc_ref[:, Q_LORA_RANK + KV_LORA_RANK:])
    kr4 = jnp.concatenate([kr, kr[:, half:], kr[:, :half]], axis=1)

    def rms(c, g):
        return c * lax.rsqrt(jnp.mean(c * c, axis=-1, keepdims=True) + RMS_EPS) * g

    cqn = rms(cq, gq_ref[...]).astype(BF16)
    ckvn = rms(ckv, gkv_ref[...]).astype(BF16)

    t = t_ref[0]
    e = kr4 * t
    lane = lax.broadcasted_iota(jnp.int32, (1, LANES), 1)
    sign = jnp.where((lane // 32) % 2 == 0, -1.0, 1.0).astype(F32)
    kpe = (e + sign * pltpu.roll(e, 64, 1)).astype(BF16)

    qf = _dot(cqn, wuq_ref[...])
    kn = _dot(ckvn, wuk_ref[...])
    vmt = _dot_nt(wuvt_ref[...], ckvn)
    ts = t * mla_scale
    for h in range(MLA_HEADS):
        qm_ref[0, h, :, 0:128] = (qf[:, h * 256:h * 256 + 128] * mla_scale).astype(BF16)
        qm_ref[0, h, :, 128:256] = (qf[:, h * 256 + 128:(h + 1) * 256] * ts).astype(BF16)
        km_ref[0, h, :, 0:128] = kn[:, h * 128:(h + 1) * 128].astype(BF16)
        km_ref[0, h, :, 128:256] = kpe
        vmt_ref[0, h] = vmt[h * 128:(h + 1) * 128, :].astype(BF16)


def _proj(x, rope_t, wqk, wvt, wc, wuq, wuk, wuvt, gq, gkv):
    B, S, D = x.shape
    tm = PROJ_TM
    hm = lambda b, i: (b, 0, i, 0)
    hmt = lambda b, i: (b, 0, 0, i)
    out_shape = (
        jax.ShapeDtypeStruct((B, NA_HEADS, S, NA_HEAD_DIM), BF16),
        jax.ShapeDtypeStruct((B, NA_HEADS, S, NA_HEAD_DIM), BF16),
        jax.ShapeDtypeStruct((B, NA_HEADS, NA_HEAD_DIM, S), BF16),
        jax.ShapeDtypeStruct((B, MLA_HEADS, S, 256), BF16),
        jax.ShapeDtypeStruct((B, MLA_HEADS, S, 256), BF16),
        jax.ShapeDtypeStruct((B, MLA_HEADS, MLA_V_DIM, S), BF16),
    )
    return pl.pallas_call(
        _proj_kernel,
        out_shape=out_shape,
        grid=(B, S // tm),
        in_specs=[
            pl.BlockSpec((1, tm, D), lambda b, i: (b, i, 0)),
            pl.BlockSpec((1, tm, LANES), lambda b, i: (b, i, 0)),
            _resident(wqk.shape), _resident(wvt.shape), _resident(wc.shape), _resident(wuq.shape),
            _resident(wuk.shape), _resident(wuvt.shape), _resident(gq.shape), _resident(gkv.shape),
        ],
        out_specs=(
            pl.BlockSpec((1, NA_HEADS, tm, NA_HEAD_DIM), hm),
            pl.BlockSpec((1, NA_HEADS, tm, NA_HEAD_DIM), hm),
            pl.BlockSpec((1, NA_HEADS, NA_HEAD_DIM, tm), hmt),
            pl.BlockSpec((1, MLA_HEADS, tm, 256), hm),
            pl.BlockSpec((1, MLA_HEADS, tm, 256), hm),
            pl.BlockSpec((1, MLA_HEADS, MLA_V_DIM, tm), hmt),
        ),
        compiler_params=_params("arbitrary", "arbitrary"),
        name="proj",
    )(x, rope_t, wqk, wvt, wc, wuq, wuk, wuvt, gq, gkv)


def _na_bias_table(rpb):
    rows = GRID_W
    n_blocks = rows // NA_ROWS
    n_ro, n_co = 2 * NA_WIN_H - 1, 2 * NA_WIN_W - 1
    kr, qr = np.arange(NA_WIN_ROWS), np.arange(NA_ROWS)
    kc, qc = np.arange(GRID_W), np.arange(GRID_W)
    c0 = np.clip(qc - NA_WIN_W // 2, 0, GRID_W - NA_WIN_W)
    col_ok = (kc[:, None] >= c0[None, :]) & (kc[:, None] < c0[None, :] + NA_WIN_W)
    col_off = np.clip(kc[:, None] - qc[None, :] + (NA_WIN_W - 1), 0, n_co - 1)
    oh_c = np.zeros((n_co, GRID_W, GRID_W), np.float32)
    oh_c[col_off, kc[:, None], qc[None, :]] = 1.0
    oh_r = np.zeros((3, n_ro, NA_WIN_ROWS, NA_ROWS), np.float32)
    row_ok = np.zeros((3, NA_WIN_ROWS, NA_ROWS), bool)
    for c, rb in enumerate((0, 1, n_blocks - 1)):
        w0 = int(np.clip(NA_ROWS * rb - NA_WIN_H // 2, 0, rows - NA_WIN_ROWS))
        r = NA_ROWS * rb + qr
        r0 = np.clip(r - NA_WIN_H // 2, 0, rows - NA_WIN_H)
        krow = w0 + kr
        row_ok[c] = (krow[:, None] >= r0[None, :]) & (krow[:, None] < r0[None, :] + NA_WIN_H)
        row_off = np.clip(krow[:, None] - r[None, :] + (NA_WIN_H - 1), 0, n_ro - 1)
        oh_r[c, row_off, kr[:, None], qr[None, :]] = 1.0
    sel = np.zeros((NA_ROWS, n_co, GRID_W, NA_ROWS, GRID_W), np.float32)
    for r in range(NA_ROWS):
        sel[r, :, :, r, :] = oh_c
    sel = sel.reshape(NA_ROWS * n_co, GRID_W, NA_QB)
    ok = row_ok[:, :, None, :, None] & col_ok[None, None, :, None, :]
    mask = np.where(ok, 0.0, NEG_BIG).astype(np.float32).reshape(3, NA_WIN_ROWS, GRID_W, NA_QB)
    hi = lax.Precision.HIGHEST
    a = jnp.einsum('hrc,zrkq->hzkqc', rpb.astype(F32) * LOG2E, oh_r, precision=hi)
    a = a.reshape(rpb.shape[0], 3, NA_WIN_ROWS, NA_ROWS * n_co)
    tab = jnp.einsum('hzkj,jxn->hzkxn', a, sel, precision=hi) + mask[None]
    return tab.reshape(rpb.shape[0], 3, NA_KB, NA_QB)


def _na_kernel(q_ref, k_ref, vt_ref, tab_ref, o_ref, st_ref, m_ref):
    n_blocks = q_ref.shape[2] // NA_QB
    rows = n_blocks * NA_ROWS

    def window(rb):
        w0 = jnp.clip(NA_ROWS * rb - NA_WIN_H // 2, 0, rows - NA_WIN_ROWS)
        return pl.multiple_of(w0 * GRID_W, NA_QB)

    def scores(rb, slot):
        tok0 = window(rb)
        q0 = pl.multiple_of(rb * NA_QB, NA_QB)
        cls = jnp.where(rb == 0, 0, jnp.where(rb == n_blocks - 1, 2, 1))
        st = _dot_nt(k_ref[0, 0, pl.ds(tok0, NA_KB), :],
                     q_ref[0, 0, pl.ds(q0, NA_QB), :]) + tab_ref[0, cls]
        st_ref[slot] = st
        m_ref[slot] = jnp.max(st, axis=0, keepdims=True)

    def output(rb, slot):
        tok0 = window(rb)
        q0 = pl.multiple_of(rb * NA_QB, NA_QB)
        p = jnp.exp2(st_ref[slot] - m_ref[slot])
        l = jnp.sum(p, axis=0, keepdims=True)
        ot = _dot(vt_ref[0, 0, :, pl.ds(tok0, NA_KB)], p.astype(BF16)) / l
        o_ref[0, pl.ds(q0, NA_QB), :] = ot.T.astype(o_ref.dtype)

    scores(0, 0)

    def body(t, carry):
        rb = 2 * t + 1
        scores(rb, 1)
        output(rb - 1, 0)
        scores(rb + 1, 0)
        output(rb, 1)
        return carry

    lax.fori_loop(0, n_blocks // 2 - 1, body, 0)
    scores(n_blocks - 1, 1)
    output(n_blocks - 2, 0)
    output(n_blocks - 1, 1)


def _na(q, k, vt, tab):
    B, H, S, Dh = q.shape
    return pl.pallas_call(
        _na_kernel,
        out_shape=jax.ShapeDtypeStruct((B, S, H * Dh), BF16),
        grid=(H, B),
        in_specs=[
            pl.BlockSpec((1, 1, S, Dh), lambda h, b: (b, h, 0, 0)),
            pl.BlockSpec((1, 1, S, Dh), lambda h, b: (b, h, 0, 0)),
            pl.BlockSpec((1, 1, Dh, S), lambda h, b: (b, h, 0, 0)),
            pl.BlockSpec((1, 3, NA_KB, NA_QB), lambda h, b: (h, 0, 0, 0)),
        ],
        out_specs=pl.BlockSpec((1, S, Dh), lambda h, b: (b, 0, h)),
        scratch_shapes=[pltpu.VMEM((2, NA_KB, NA_QB), F32), pltpu.VMEM((2, 1, NA_QB), F32)],
        compiler_params=_params("arbitrary", "arbitrary"),
        name="na",
    )(q, k, vt, tab)


def _mla_kernel(q_ref, k_ref, vt_ref, o_ref, st_ref):
    n_chunks = k_ref.shape[2] // MLA_KC
    n_tiles = q_ref.shape[2] // MLA_TQ

    def pass1(j, slot):
        q = q_ref[0, 0, j * MLA_TQ:(j + 1) * MLA_TQ, :]
        m = None
        for c in range(n_chunks):
            ks = slice(c * MLA_KC, (c + 1) * MLA_KC)
            st = _dot_nt(k_ref[0, 0, ks, :], q)
            st_ref[slot, ks, :] = st
            mc = jnp.max(st, axis=0, keepdims=True)
            m = mc if c == 0 else jnp.maximum(m, mc)
        return m

    def pass2(j, slot, m):
        l = acc = None
        for c in range(n_chunks):
            ks = slice(c * MLA_KC, (c + 1) * MLA_KC)
            p = jnp.exp2(st_ref[slot, ks, :] - m)
            ps = jnp.sum(p, axis=0, keepdims=True)
            pv = _dot(vt_ref[0, 0, :, ks], p.astype(BF16))
            l = ps if c == 0 else l + ps
            acc = pv if c == 0 else acc + pv
        o_ref[0, j * MLA_TQ:(j + 1) * MLA_TQ, :] = (acc / l).T.astype(o_ref.dtype)

    m_prev = pass1(0, 0)
    for j in range(1, n_tiles):
        m_cur = pass1(j, j % 2)
        pass2(j - 1, (j - 1) % 2, m_prev)
        m_prev = m_cur
    pass2(n_tiles - 1, (n_tiles - 1) % 2, m_prev)


def _mla(q, k, vt):
    B, H, S, Dq = q.shape
    Dv = MLA_V_DIM
    tq = MLA_TQ * MLA_TILES
    return pl.pallas_call(
        _mla_kernel,
        out_shape=jax.ShapeDtypeStruct((B, S, H * Dv), BF16),
        grid=(B, H, S // tq),
        in_specs=[
            pl.BlockSpec((1, 1, tq, Dq), lambda b, h, i: (b, h, i, 0)),
            pl.BlockSpec((1, 1, S, Dq), lambda b, h, i: (b, h, 0, 0)),
            pl.BlockSpec((1, 1, Dv, S), lambda b, h, i: (b, h, 0, 0)),
        ],
        out_specs=pl.BlockSpec((1, tq, Dv), lambda b, h, i: (b, i, h)),
        scratch_shapes=[pltpu.VMEM((2, S, MLA_TQ), F32)],
        compiler_params=_params("arbitrary", "arbitrary", "arbitrary"),
        name="mla",
    )(q, k, vt)


def _layer_norm(z, g, b):
    mu = jnp.mean(z, axis=-1, keepdims=True)
    zc = z - mu
    var = jnp.mean(zc * zc, axis=-1, keepdims=True)
    return zc * lax.rsqrt(var + LN_EPS) * g + b


def _first_lane_where(cond, lane):
    return jnp.min(jnp.where(cond, lane, LANES), axis=-1, keepdims=True)


def _mix_kernel(ona_ref, omla_ref, x_ref, wo_ref, g_ref, b_ref, wr_ref, br_ref,
                x1_ref, ri_ref, gate_ref, cnt_ref, carry_ref):
    step = pl.program_id(0)

    @pl.when(step == 0)
    def _():
        carry_ref[...] = jnp.zeros_like(carry_ref)

    ts = MIX_SUB
    subs = [slice(j * ts, (j + 1) * ts) for j in range(x_ref.shape[0] // ts)]
    lane = lax.broadcasted_iota(jnp.int32, (ts, LANES), 1)
    g_mask = (lane >= N_EXPERTS) & (lane < N_EXPERTS + N_GROUPS)
    r_i = lax.broadcasted_iota(jnp.int32, (ts, ts), 0)
    c_i = lax.broadcasted_iota(jnp.int32, (ts, ts), 1)
    tri = jnp.where(c_i < r_i, 1.0, 0.0).astype(BF16)

    def project(rows):
        return (_dot(ona_ref[rows, :], wo_ref[0:NA_WIDTH, :])
                + _dot(omla_ref[rows, :], wo_ref[NA_WIDTH:, :]))

    mix = project(subs[0])
    for j, rows in enumerate(subs):
        nxt = project(subs[j + 1]) if j + 1 < len(subs) else None
        _norm_route(rows, mix, lane, g_mask, tri, x_ref, g_ref, b_ref, wr_ref, br_ref,
                    x1_ref, ri_ref, gate_ref, carry_ref)
        mix = nxt
    cnt_ref[...] = carry_ref[...]


def _norm_route(rows, mix, lane, g_mask, tri, x_ref, g_ref, b_ref, wr_ref, br_ref,
                x1_ref, ri_ref, gate_ref, carry_ref):
    x1 = _layer_norm(ALPHA * x_ref[rows, :] + mix, g_ref[...], b_ref[...])
    x1_ref[rows, :] = x1

    logit = _dot(x1.astype(BF16), wr_ref[...]) + br_ref[...]
    gl = jnp.where(g_mask, logit, NEG_BIG)
    gmax = jnp.max(gl, axis=-1, keepdims=True)
    gsum = jnp.sum(jnp.where(g_mask, jnp.exp(gl - gmax), 0.0), axis=-1, keepdims=True)
    g_val = 1.0 / gsum
    g_idx = _first_lane_where(gl == gmax, lane) - N_EXPERTS

    e_mask = (lane < N_EXPERTS) & ((lane // EXPERTS_PER_GROUP) == g_idx)
    el = jnp.where(e_mask, logit, NEG_BIG)
    emax = jnp.max(el, axis=-1, keepdims=True)
    esum = jnp.sum(jnp.where(e_mask, jnp.exp(el - emax), 0.0), axis=-1, keepdims=True)
    i1 = _first_lane_where(el == emax, lane)
    el2 = jnp.where(lane == i1, NEG_BIG, el)
    emax2 = jnp.max(el2, axis=-1, keepdims=True)
    i2 = _first_lane_where(el2 == emax2, lane)
    v1 = 1.0 / esum
    v2 = jnp.exp(emax2 - emax) / esum
    vsum = v1 + v2
    w1 = g_val * (v1 / vsum)
    w2 = g_val * (v2 / vsum)

    oh1 = lane == i1
    oh2 = lane == i2
    oh = jnp.where(oh1 | oh2, 1.0, 0.0)
    before = _dot(tri, oh.astype(BF16)) + carry_ref[...]
    rank1 = jnp.sum(jnp.where(oh1, before, 0.0), axis=-1, keepdims=True).astype(jnp.int32)
    rank2 = jnp.sum(jnp.where(oh2, before, 0.0), axis=-1, keepdims=True).astype(jnp.int32)
    carry_ref[...] = carry_ref[...] + jnp.sum(oh, axis=0, keepdims=True)

    ri = jnp.where(lane == 0, i1, jnp.where(lane == 1, i2,
                   jnp.where(lane == 2, rank1, jnp.where(lane == 3, rank2, 0))))
    ri_ref[:, rows] = ri.T[0:SUBLANES, :]
    gate_ref[rows, :] = jnp.where(lane == 0, w1, jnp.where(lane == 1, w2, 0.0))


def _mix_out(o_na, o_mla, x, wo, g, b, wr, br):
    N, D = x.shape
    tm = MIX_TM
    row = lambda i: (i, 0)
    return pl.pallas_call(
        _mix_kernel,
        out_shape=(
            jax.ShapeDtypeStruct((N, D), F32),
            jax.ShapeDtypeStruct((SUBLANES, N), jnp.int32),
            jax.ShapeDtypeStruct((N, LANES), F32),
            jax.ShapeDtypeStruct((1, LANES), F32),
        ),
        grid=(N // tm,),
        in_specs=[
            pl.BlockSpec((tm, NA_WIDTH), row),
            pl.BlockSpec((tm, NA_WIDTH), row),
            pl.BlockSpec((tm, D), row),
            _resident(wo.shape), _resident(g.shape), _resident(b.shape),
            _resident(wr.shape), _resident(br.shape),
        ],
        out_specs=(
            pl.BlockSpec((tm, D), row),
            pl.BlockSpec((SUBLANES, tm), lambda i: (0, i)),
            pl.BlockSpec((tm, LANES), row),
            pl.BlockSpec((1, LANES), lambda i: (0, 0)),
        ),
        scratch_shapes=[pltpu.VMEM((1, LANES), F32)],
        compiler_params=_params("arbitrary"),
        name="mix_out",
    )(o_na, o_mla, x, wo, g, b, wr, br)


def _pos_kernel(start_ref, ri_ref, pos_ref):
    eid = ri_ref[0:TOP_K_EXPERT, :]
    pos = ri_ref[TOP_K_EXPERT:2 * TOP_K_EXPERT, :]
    for e in range(N_EXPERTS):
        pos = pos + jnp.where(eid == e, start_ref[e], 0)
    pos_ref[...] = pos


def _positions(pad_starts, ri):
    n = ri.shape[1]
    return pl.pallas_call(
        _pos_kernel,
        out_shape=jax.ShapeDtypeStruct((TOP_K_EXPERT, n), jnp.int32),
        grid_spec=pltpu.PrefetchScalarGridSpec(
            num_scalar_prefetch=1,
            grid=(1,),
            in_specs=[pl.BlockSpec(ri.shape, lambda i, s: (0, 0))],
            out_specs=pl.BlockSpec((TOP_K_EXPERT, n), lambda i, s: (0, 0)),
        ),
        compiler_params=_params("arbitrary"),
        name="positions",
    )(pad_starts, ri)


def _row_copy(src, src_row, dst, dst_row, sem):
    return pltpu.make_async_copy(src.at[pl.ds(src_row, 1), :], dst.at[pl.ds(dst_row, 1), :], sem)


_PAD_CHUNKS = tuple(1 << b for b in reversed(range(3, MOE_TB.bit_length() - 1)))


def _dispatch_kernel(pos_ref, fill_ref, x_ref, xs_ref, zero_ref, sem, zsem):
    tm = x_ref.shape[0]
    n_tok = pl.num_programs(0) * tm
    base = pl.program_id(0) * tm

    def pad_copies(fn):
        def per_expert(e, carry):
            start = fill_ref[e]
            n = fill_ref[N_EXPERTS + e]
            head = (-start) & (SUBLANES - 1)
            for r in range(SUBLANES - 1):
                @pl.when(r < head)
                def _(r=r):
                    fn(_row_copy(zero_ref, 0, xs_ref, start + r, zsem))
            off = start + head
            rem = n - head
            for c in _PAD_CHUNKS:
                hit = (rem & c) != 0

                @pl.when(hit)
                def _(off=off, c=c):
                    fn(pltpu.make_async_copy(zero_ref.at[pl.ds(0, c), :],
                                             xs_ref.at[pl.ds(pl.multiple_of(off, SUBLANES), c), :],
                                             zsem))
                off = off + jnp.where(hit, c, 0)
            return carry
        lax.fori_loop(0, N_EXPERTS, per_expert, 0)

        zrows = zero_ref.shape[0]

        def tail(t, carry):
            row = pl.multiple_of(fill_ref[2 * N_EXPERTS] + t * zrows, zrows)
            fn(pltpu.make_async_copy(zero_ref, xs_ref.at[pl.ds(row, zrows), :], zsem))
            return carry
        lax.fori_loop(0, (xs_ref.shape[0] - fill_ref[2 * N_EXPERTS]) // zrows, tail, 0)

    @pl.when(pl.program_id(0) == 0)
    def _():
        zero_ref[...] = jnp.zeros_like(zero_ref)
        pad_copies(lambda d: d.start())

    def issue(j, carry):
        for kk in range(TOP_K_EXPERT):
            _row_copy(x_ref, j, xs_ref, pos_ref[kk * n_tok + base + j], sem).start()
        return carry

    lax.fori_loop(0, tm, issue, 0, unroll=8)

    @pl.when(pl.program_id(0) == 0)
    def _():
        pad_copies(lambda d: d.wait())

    for kk in range(TOP_K_EXPERT):
        pltpu.make_async_copy(x_ref, xs_ref.at[pl.ds(0, tm), :], sem).wait()


def _dispatch(pos, fill, x1, n_rows):
    N, C = x1.shape
    tm = DISP_TM
    return pl.pallas_call(
        _dispatch_kernel,
        out_shape=jax.ShapeDtypeStruct((n_rows, C), x1.dtype),
        grid_spec=pltpu.PrefetchScalarGridSpec(
            num_scalar_prefetch=2,
            grid=(N // tm,),
            in_specs=[pl.BlockSpec((tm, C), lambda i, pos, fill: (i, 0))],
            out_specs=pl.BlockSpec(memory_space=pl.ANY),
            scratch_shapes=[pltpu.VMEM((_PAD_CHUNKS[0], C), x1.dtype),
                            pltpu.SemaphoreType.DMA, pltpu.SemaphoreType.DMA],
        ),
        compiler_params=_params("arbitrary"),
        name="dispatch",
    )(pos, fill, x1)


def _experts_kernel(be_ref, nxt_ref, nb_ref, xs_ref, wg_hbm, wu_hbm, wd_hbm, ys_ref,
                    wg_f, wu_f, wd_f, wg_s, wu_s, wd_s, sem):
    i = pl.program_id(0)
    active = i < nb_ref[0]
    e = be_ref[i]
    fresh = active & ((i == 0) | (e != be_ref[jnp.maximum(i - 1, 0)]))

    def fetch(ex):
        return (pltpu.make_async_copy(wg_hbm.at[ex], wg_f, sem.at[0]),
                pltpu.make_async_copy(wu_hbm.at[ex], wu_f, sem.at[1]),
                pltpu.make_async_copy(wd_hbm.at[ex], wd_f, sem.at[2]))

    @pl.when(i == 0)
    def _():
        for d in fetch(e):
            d.start()

    @pl.when(fresh)
    def _():
        for d in fetch(e):
            d.wait()
        wg_s[...] = wg_f[...].astype(BF16)
        wu_s[...] = wu_f[...].astype(BF16)
        wd_s[...] = wd_f[...].astype(BF16)
        nx = nxt_ref[i]

        @pl.when(nx >= 0)
        def _():
            for d in fetch(nx):
                d.start()

    @pl.when(active)
    def _():
        xb = xs_ref[...].astype(BF16)
        gp = _dot(xb, wg_s[...])
        up = _dot(xb, wu_s[...])
        hdn = (gp * jax.nn.sigmoid(gp) * up).astype(BF16)
        ys_ref[...] = _dot(hdn, wd_s[...])

    @pl.when(jnp.logical_not(active))
    def _():
        ys_ref[...] = jnp.zeros_like(ys_ref)


def _experts(block_e, next_e, n_used, xs, w_gate, w_up, w_down):
    P, C = xs.shape
    E, D, F = w_gate.shape
    tb = MOE_TB
    n_blocks = P // tb
    hbm = pl.BlockSpec(memory_space=pl.ANY)
    return pl.pallas_call(
        _experts_kernel,
        out_shape=jax.ShapeDtypeStruct((P, C), F32),
        grid_spec=pltpu.PrefetchScalarGridSpec(
            num_scalar_prefetch=3,
            grid=(n_blocks,),
            in_specs=[
                pl.BlockSpec((tb, C), lambda i, be, nx, nb: (jnp.minimum(i, nb[0] - 1), 0)),
                hbm, hbm, hbm,
            ],
            out_specs=pl.BlockSpec((tb, C), lambda i, be, nx, nb: (i, 0)),
            scratch_shapes=[pltpu.VMEM((D, F), F32), pltpu.VMEM((D, F), F32), pltpu.VMEM((F, D), F32),
                            pltpu.VMEM((D, F), BF16), pltpu.VMEM((D, F), BF16),
                            pltpu.VMEM((F, D), BF16), pltpu.SemaphoreType.DMA((3,))],
        ),
        compiler_params=_params("arbitrary"),
        name="experts",
    )(block_e, next_e, n_used, xs, w_gate, w_up, w_down)


def _final_kernel(pos_ref, x1_ref, gate_ref, ys_ref, p_ref, g_ref, b_ref, wpg_ref, wpe_ref,
                  gp_ref, o_ref, buf, sem):
    tm = x1_ref.shape[0]
    i = pl.program_id(0)
    last = pl.num_programs(0) - 1
    slot = i % 2

    def issue(tile, s, unroll):
        n_tok = pl.num_programs(0) * tm
        base = tile * tm

        def body(j, carry):
            for kk in range(TOP_K_EXPERT):
                _row_copy(ys_ref, pos_ref[kk * n_tok + base + j],
                          buf.at[s, kk], j, sem.at[s]).start()
            return carry
        lax.fori_loop(0, tm, body, 0, unroll=unroll)

    def drain(s):
        for kk in range(TOP_K_EXPERT):
            pltpu.make_async_copy(ys_ref.at[pl.ds(0, tm), :], buf.at[s, kk], sem.at[s]).wait()

    @pl.when(i == 0)
    def _():
        issue(0, 0, 8)

    drain(slot)
    issue(jnp.minimum(i + 1, last), 1 - slot, True)

    gates = gate_ref[...]
    g1 = gates[:, 0:1]
    g2 = gates[:, 1:2]
    ffn = buf[slot, 0] * g1 + buf[slot, 1] * g2
    x2 = _layer_norm(ALPHA * x1_ref[...] + ffn, g_ref[...], b_ref[...])

    gate = jax.nn.sigmoid(_dot(x2.astype(BF16), wpg_ref[...]))
    e = _dot(p_ref[...].astype(BF16), wpe_ref[...])
    t = e * gate
    ple = t * lax.rsqrt(jnp.mean(t * t, axis=-1, keepdims=True) + RMS_EPS) * gp_ref[...]
    o_ref[...] = x2 + ple

    @pl.when(i == last)
    def _():
        drain(1 - slot)


def _final(pos, x1, gates, ys, p, g, b, wpg, wpe, gp):
    N, D = x1.shape
    tm = FIN_TM
    row = lambda i, pos: (i, 0)
    return pl.pallas_call(
        _final_kernel,
        out_shape=jax.ShapeDtypeStruct((N, D), F32),
        grid_spec=pltpu.PrefetchScalarGridSpec(
            num_scalar_prefetch=1,
            grid=(N // tm,),
            in_specs=[
                pl.BlockSpec((tm, D), row),
                pl.BlockSpec((tm, LANES), row),
                pl.BlockSpec(memory_space=pl.ANY),
                pl.BlockSpec((tm, PLE_DIM), row),
                _resident(g.shape), _resident(b.shape), _resident(wpg.shape),
                _resident(wpe.shape), _resident(gp.shape),
            ],
            out_specs=pl.BlockSpec((tm, D), row),
            scratch_shapes=[pltpu.VMEM((2, TOP_K_EXPERT, tm, D), F32),
                            pltpu.SemaphoreType.DMA((2,))],
        ),
        compiler_params=_params("arbitrary"),
        name="final",
    )(pos, x1, gates, ys, p, g, b, wpg, wpe, gp)


def _rope_table(positions):
    inv_freq = 1.0 / (ROPE_THETA ** (jnp.arange(0, MLA_ROPE_DIM, 2, dtype=F32) / MLA_ROPE_DIM))
    half = MLA_ROPE_DIM // 2
    phase = jnp.concatenate([jnp.zeros((2 * half,), F32), jnp.full((2 * half,), np.pi / 2, F32)])
    return jnp.cos(positions.astype(F32)[..., None] * jnp.tile(inv_freq, 4) - phase)


def _layer(x, p_i, rope_t, w_in, rpb, q_norm_g, kv_norm_g, w_uq, w_uk, w_uv, w_o, ln1_g, ln1_b,
           w_group, b_group, w_router, b_router, w_gate, w_up, w_down, ln2_g, ln2_b,
           w_ple, w_ple_gate, ple_norm_g):
    B, S, D = x.shape
    N = B * S
    s3 = 3 * NA_WIDTH
    half = MLA_ROPE_DIM // 2

    wqk = w_in[:, :2 * NA_WIDTH].astype(BF16)
    wc = w_in[:, s3:].astype(BF16)
    wvt = w_in[:, 2 * NA_WIDTH:s3].T.astype(BF16)
    uq = w_uq.reshape(Q_LORA_RANK, MLA_HEADS, MLA_QK_DIM)
    nope, x1c, x2c = (uq[..., :MLA_NOPE_DIM], uq[..., MLA_NOPE_DIM:MLA_NOPE_DIM + half],
                      uq[..., MLA_NOPE_DIM + half:])
    wuq = jnp.concatenate([nope, x1c, x2c, x2c, x1c], axis=-1).reshape(Q_LORA_RANK, -1).astype(BF16)
    wuk = w_uk.astype(BF16)
    wuvt = w_uv.T.astype(BF16)
    row = lambda v: v.reshape(1, -1).astype(F32)

    q_na, k_na, v_nat, q_m, k_m, v_mt = _proj(x, rope_t, wqk, wvt, wc, wuq, wuk, wuvt,
                                               row(q_norm_g), row(kv_norm_g))
    o_na = _na(q_na, k_na, v_nat, _na_bias_table(rpb))
    o_mla = _mla(q_m, k_m, v_mt)

    pad = LANES - N_EXPERTS - N_GROUPS
    wr = jnp.concatenate([w_router, w_group, jnp.zeros((D, pad), F32)], axis=1).astype(BF16)
    br = jnp.concatenate([b_router.reshape(-1), b_group, jnp.zeros((pad,), F32)]).reshape(1, -1)
    x1, ri, gates, cnt = _mix_out(o_na.reshape(N, -1), o_mla.reshape(N, -1), x.reshape(N, D),
                                       w_o.astype(BF16), row(ln1_g), row(ln1_b), wr, br)

    tb = MOE_TB
    counts = cnt[0, :N_EXPERTS].astype(jnp.int32)
    padded = ((counts + tb - 1) // tb) * tb
    pad_ends = jnp.cumsum(padded)
    pad_starts = pad_ends - padded
    pos = _positions(pad_starts.astype(jnp.int32), ri).reshape(-1)
    fill = jnp.concatenate([pad_starts + counts, padded - counts, pad_ends[-1:]]).astype(jnp.int32)
    n_blocks = (N * TOP_K_EXPERT + N_EXPERTS * (tb - 1) + tb - 1) // tb
    blk_row = jnp.arange(n_blocks, dtype=jnp.int32) * tb
    block_e = jnp.minimum(jnp.sum(pad_ends[None, :] <= blk_row[:, None], axis=1),
                          N_EXPERTS - 1).astype(jnp.int32)
    n_used = (pad_ends[-1:] // tb).astype(jnp.int32)
    nxt_blk = pad_ends[block_e] // tb
    next_e = jnp.where(nxt_blk < n_used[0], block_e[jnp.minimum(nxt_blk, n_blocks - 1)],
                       -1).astype(jnp.int32)

    xs = _dispatch(pos, fill, x1, n_blocks * tb)
    ys = _experts(block_e, next_e, n_used, xs, w_gate, w_up, w_down)
    out = _final(pos, x1, gates, ys, p_i.reshape(N, -1), row(ln2_g), row(ln2_b),
                 w_ple_gate.astype(BF16), w_ple.astype(BF16), row(ple_norm_g))
    return out.reshape(B, S, D)


def kernel(x, p, positions, w_in, rpb, q_norm_g, kv_norm_g, w_uq, w_uk, w_uv, w_o, ln1_g, ln1_b,
           w_group, b_group, w_router, b_router, w_gate, w_up, w_down, ln2_g, ln2_b,
           w_ple, w_ple_gate, ple_norm_g):
    rope_t = _rope_table(positions)
    for i in range(DEPTH):
        x = _layer(x, p[i], rope_t, w_in[i], rpb[i], q_norm_g[i], kv_norm_g[i], w_uq[i], w_uk[i],
                   w_uv[i], w_o[i], ln1_g[i], ln1_b[i], w_group[i], b_group[i], w_router[i],
                   b_router[i], w_gate[i], w_up[i], w_down[i], ln2_g[i], ln2_b[i],
                   w_ple[i], w_ple_gate[i], ple_norm_g[i])
    return x
```

```python
import numpy as np
import jax
import jax.numpy as jnp
from jax import lax
from jax.experimental import pallas as pl
from jax.experimental.pallas import tpu as pltpu

D_MODEL = 2048
DEPTH = 1
GRID_W = 64
PLE_DIM = 256
NA_HEADS = 8
NA_HEAD_DIM = 128
NA_WIN_H = 8
NA_WIN_W = 16
NA_WIDTH = NA_HEADS * NA_HEAD_DIM
MLA_HEADS = 8
MLA_NOPE_DIM = 128
MLA_ROPE_DIM = 64
MLA_V_DIM = 128
MLA_QK_DIM = MLA_NOPE_DIM + MLA_ROPE_DIM
Q_LORA_RANK = 512
KV_LORA_RANK = 512
ROPE_THETA = 10000.0
N_GROUPS = 8
EXPERTS_PER_GROUP = 8
N_EXPERTS = N_GROUPS * EXPERTS_PER_GROUP
TOP_K_EXPERT = 2
D_EXPERT = 512
ALPHA = (2 * DEPTH) ** 0.25
LN_EPS = 1e-5
RMS_EPS = 1e-6

LANES = 128
SUBLANES = 8
VMEM_LIMIT_BYTES = 56 * 1024 * 1024

PROJ_TM = 256
NA_ROWS = 4
NA_QB = NA_ROWS * GRID_W
NA_WIN_ROWS = NA_ROWS + NA_WIN_H
NA_KB = NA_WIN_ROWS * GRID_W
MLA_TQ = 256
MLA_KC = 1024
MLA_TILES = 8
MIX_TM = 512
MIX_SUB = 256
MOE_TB = 256
DISP_TM = 256
FIN_TM = 256
NEG_BIG = -1e30
LOG2E = 1.4426950408889634

BF16 = jnp.bfloat16
F32 = jnp.float32

_NT = (((1,), (1,)), ((), ()))


def _dot(a, b):
    return jnp.dot(a, b, preferred_element_type=F32)


def _dot_nt(a, b):
    return lax.dot_general(a, b, _NT, preferred_element_type=F32)


def _params(*sem):
    return pltpu.CompilerParams(dimension_semantics=sem, vmem_limit_bytes=VMEM_LIMIT_BYTES)


def _resident(shape):
    nd = len(shape)
    return pl.BlockSpec(shape, lambda *_: (0,) * nd, pipeline_mode=pl.Buffered(1))


def _proj_kernel(x_ref, t_ref, wqk_ref, wvt_ref, wc_ref, wuq_ref, wuk_ref, wuvt_ref, gq_ref, gkv_ref,
                 qna_ref, kna_ref, vnat_ref, qm_ref, km_ref, vmt_ref):
    xb = x_ref[0].astype(BF16)
    na_scale = NA_HEAD_DIM ** -0.5 * LOG2E
    mla_scale = MLA_QK_DIM ** -0.5 * LOG2E
    q = _dot(xb, wqk_ref[:, 0:NA_WIDTH]) * na_scale
    k = _dot(xb, wqk_ref[:, NA_WIDTH:2 * NA_WIDTH])
    vt = _dot_nt(wvt_ref[...], xb)
    for h in range(NA_HEADS):
        sl = slice(h * NA_HEAD_DIM, (h + 1) * NA_HEAD_DIM)
        qna_ref[0, h] = q[:, sl].astype(BF16)
        kna_ref[0, h] = k[:, sl].astype(BF16)
        vnat_ref[0, h] = vt[sl, :].astype(BF16)

    half = MLA_ROPE_DIM // 2
    cq = _dot(xb, wc_ref[:, 0:Q_LORA_RANK])
    ckv = _dot(xb, wc_ref[:, Q_LORA_RANK:Q_LORA_RANK + KV_LORA_RANK])
    kr = _dot(xb, wc_ref[:, Q_LORA_RANK + KV_LORA_RANK:])
    kr4 = jnp.concatenate([kr, kr[:, half:], kr[:, :half]], axis=1)

    def rms(c, g):
        return c * lax.rsqrt(jnp.mean(c * c, axis=-1, keepdims=True) + RMS_EPS) * g

    cqn = rms(cq, gq_ref[...]).astype(BF16)
    ckvn = rms(ckv, gkv_ref[...]).astype(BF16)

    t = t_ref[0]
    e = kr4 * t
    lane = lax.broadcasted_iota(jnp.int32, (1, LANES), 1)
    sign = jnp.where((lane // 32) % 2 == 0, -1.0, 1.0).astype(F32)
    kpe = (e + sign * pltpu.roll(e, 64, 1)).astype(BF16)

    qf = _dot(cqn, wuq_ref[...])
    kn = _dot(ckvn, wuk_ref[...])
    vmt = _dot_nt(wuvt_ref[...], ckvn)
    ts = t * mla_scale
    for h in range(MLA_HEADS):
        qm_ref[0, h, :, 0:128] = (qf[:, h * 256:h * 256 + 128] * mla_scale).astype(BF16)
        qm_ref[0, h, :, 128:256] = (qf[:, h * 256 + 128:(h + 1) * 256] * ts).astype(BF16)
        km_ref[0, h, :, 0:128] = kn[:, h * 128:(h + 1) * 128].astype(BF16)
        km_ref[0, h, :, 128:256] = kpe
        vmt_ref[0, h] = vmt[h * 128:(h + 1) * 128, :].astype(BF16)


def _proj(x, rope_t, wqk, wvt, wc, wuq, wuk, wuvt, gq, gkv):
    B, S, D = x.shape
    tm = PROJ_TM
    hm = lambda b, i: (b, 0, i, 0)
    hmt = lambda b, i: (b, 0, 0, i)
    out_shape = (
        jax.ShapeDtypeStruct((B, NA_HEADS, S, NA_HEAD_DIM), BF16),
        jax.ShapeDtypeStruct((B, NA_HEADS, S, NA_HEAD_DIM), BF16),
        jax.ShapeDtypeStruct((B, NA_HEADS, NA_HEAD_DIM, S), BF16),
        jax.ShapeDtypeStruct((B, MLA_HEADS, S, 256), BF16),
        jax.ShapeDtypeStruct((B, MLA_HEADS, S, 256), BF16),
        jax.ShapeDtypeStruct((B, MLA_HEADS, MLA_V_DIM, S), BF16),
    )
    return pl.pallas_call(
        _proj_kernel,
        out_shape=out_shape,
        grid=(B, S // tm),
        in_specs=[
            pl.BlockSpec((1, tm, D), lambda b, i: (b, i, 0)),
            pl.BlockSpec((1, tm, LANES), lambda b, i: (b, i, 0)),
            _resident(wqk.shape), _resident(wvt.shape), _resident(wc.shape), _resident(wuq.shape),
            _resident(wuk.shape), _resident(wuvt.shape), _resident(gq.shape), _resident(gkv.shape),
        ],
        out_specs=(
            pl.BlockSpec((1, NA_HEADS, tm, NA_HEAD_DIM), hm),
            pl.BlockSpec((1, NA_HEADS, tm, NA_HEAD_DIM), hm),
            pl.BlockSpec((1, NA_HEADS, NA_HEAD_DIM, tm), hmt),
            pl.BlockSpec((1, MLA_HEADS, tm, 256), hm),
            pl.BlockSpec((1, MLA_HEADS, tm, 256), hm),
            pl.BlockSpec((1, MLA_HEADS, MLA_V_DIM, tm), hmt),
        ),
        compiler_params=_params("arbitrary", "arbitrary"),
        name="proj",
    )(x, rope_t, wqk, wvt, wc, wuq, wuk, wuvt, gq, gkv)


def _na_bias_table(rpb):
    rows = GRID_W
    n_blocks = rows // NA_ROWS
    n_ro, n_co = 2 * NA_WIN_H - 1, 2 * NA_WIN_W - 1
    kr, qr = np.arange(NA_WIN_ROWS), np.arange(NA_ROWS)
    kc, qc = np.arange(GRID_W), np.arange(GRID_W)
    c0 = np.clip(qc - NA_WIN_W // 2, 0, GRID_W - NA_WIN_W)
    col_ok = (kc[:, None] >= c0[None, :]) & (kc[:, None] < c0[None, :] + NA_WIN_W)
    col_off = np.clip(kc[:, None] - qc[None, :] + (NA_WIN_W - 1), 0, n_co - 1)
    oh_c = np.zeros((n_co, GRID_W, GRID_W), np.float32)
    oh_c[col_off, kc[:, None], qc[None, :]] = 1.0
    oh_r = np.zeros((3, n_ro, NA_WIN_ROWS, NA_ROWS), np.float32)
    row_ok = np.zeros((3, NA_WIN_ROWS, NA_ROWS), bool)
    for c, rb in enumerate((0, 1, n_blocks - 1)):
        w0 = int(np.clip(NA_ROWS * rb - NA_WIN_H // 2, 0, rows - NA_WIN_ROWS))
        r = NA_ROWS * rb + qr
        r0 = np.clip(r - NA_WIN_H // 2, 0, rows - NA_WIN_H)
        krow = w0 + kr
        row_ok[c] = (krow[:, None] >= r0[None, :]) & (krow[:, None] < r0[None, :] + NA_WIN_H)
        row_off = np.clip(krow[:, None] - r[None, :] + (NA_WIN_H - 1), 0, n_ro - 1)
        oh_r[c, row_off, kr[:, None], qr[None, :]] = 1.0
    sel = np.zeros((NA_ROWS, n_co, GRID_W, NA_ROWS, GRID_W), np.float32)
    for r in range(NA_ROWS):
        sel[r, :, :, r, :] = oh_c
    sel = sel.reshape(NA_ROWS * n_co, GRID_W, NA_QB)
    ok = row_ok[:, :, None, :, None] & col_ok[None, None, :, None, :]
    mask = np.where(ok, 0.0, NEG_BIG).astype(np.float32).reshape(3, NA_WIN_ROWS, GRID_W, NA_QB)
    hi = lax.Precision.HIGHEST
    a = jnp.einsum('hrc,zrkq->hzkqc', rpb.astype(F32) * LOG2E, oh_r, precision=hi)
    a = a.reshape(rpb.shape[0], 3, NA_WIN_ROWS, NA_ROWS * n_co)
    tab = jnp.einsum('hzkj,jxn->hzkxn', a, sel, precision=hi) + mask[None]
    return tab.reshape(rpb.shape[0], 3, NA_KB, NA_QB)


def _na_kernel(q_ref, k_ref, vt_ref, tab_ref, o_ref, st_ref, m_ref):
    n_blocks = q_ref.shape[2] // NA_QB
    rows = n_blocks * NA_ROWS

    def window(rb):
        w0 = jnp.clip(NA_ROWS * rb - NA_WIN_H // 2, 0, rows - NA_WIN_ROWS)
        return pl.multiple_of(w0 * GRID_W, NA_QB)

    def scores(rb, slot):
        tok0 = window(rb)
        q0 = pl.multiple_of(rb * NA_QB, NA_QB)
        cls = jnp.where(rb == 0, 0, jnp.where(rb == n_blocks - 1, 2, 1))
        st = _dot_nt(k_ref[0, 0, pl.ds(tok0, NA_KB), :],
                     q_ref[0, 0, pl.ds(q0, NA_QB), :]) + tab_ref[0, cls]
        st_ref[slot] = st
        m_ref[slot] = jnp.max(st, axis=0, keepdims=True)

    def output(rb, slot):
        tok0 = window(rb)
        q0 = pl.multiple_of(rb * NA_QB, NA_QB)
        p = jnp.exp2(st_ref[slot] - m_ref[slot])
        l = jnp.sum(p, axis=0, keepdims=True)
        ot = _dot(vt_ref[0, 0, :, pl.ds(tok0, NA_KB)], p.astype(BF16)) / l
        o_ref[0, pl.ds(q0, NA_QB), :] = ot.T.astype(o_ref.dtype)

    scores(0, 0)

    def body(t, carry):
        rb = 2 * t + 1
        scores(rb, 1)
        output(rb - 1, 0)
        scores(rb + 1, 0)
        output(rb, 1)
        return carry

    lax.fori_loop(0, n_blocks // 2 - 1, body, 0)
    scores(n_blocks - 1, 1)
    output(n_blocks - 2, 0)
    output(n_blocks - 1, 1)


def _na(q, k, vt, tab):
    B, H, S, Dh = q.shape
    return pl.pallas_call(
        _na_kernel,
        out_shape=jax.ShapeDtypeStruct((B, S, H * Dh), BF16),
        grid=(H, B),
        in_specs=[
            pl.BlockSpec((1, 1, S, Dh), lambda h, b: (b, h, 0, 0)),
            pl.BlockSpec((1, 1, S, Dh), lambda h, b: (b, h, 0, 0)),
            pl.BlockSpec((1, 1, Dh, S), lambda h, b: (b, h, 0, 0)),
            pl.BlockSpec((1, 3, NA_KB, NA_QB), lambda h, b: (h, 0, 0, 0)),
        ],
        out_specs=pl.BlockSpec((1, S, Dh), lambda h, b: (b, 0, h)),
        scratch_shapes=[pltpu.VMEM((2, NA_KB, NA_QB), F32), pltpu.VMEM((2, 1, NA_QB), F32)],
        compiler_params=_params("arbitrary", "arbitrary"),
        name="na",
    )(q, k, vt, tab)


def _mla_kernel(q_ref, k_ref, vt_ref, o_ref, st_ref):
    n_chunks = k_ref.shape[2] // MLA_KC
    n_tiles = q_ref.shape[2] // MLA_TQ

    def pass1(j, slot):
        q = q_ref[0, 0, j * MLA_TQ:(j + 1) * MLA_TQ, :]
        m = None
        for c in range(n_chunks):
            ks = slice(c * MLA_KC, (c + 1) * MLA_KC)
            st = _dot_nt(k_ref[0, 0, ks, :], q)
            st_ref[slot, ks, :] = st
            mc = jnp.max(st, axis=0, keepdims=True)
            m = mc if c == 0 else jnp.maximum(m, mc)
        return m

    def pass2(j, slot, m):
        l = acc = None
        for c in range(n_chunks):
            ks = slice(c * MLA_KC, (c + 1) * MLA_KC)
            p = jnp.exp2(st_ref[slot, ks, :] - m)
            ps = jnp.sum(p, axis=0, keepdims=True)
            pv = _dot(vt_ref[0, 0, :, ks], p.astype(BF16))
            l = ps if c == 0 else l + ps
            acc = pv if c == 0 else acc + pv
        o_ref[0, j * MLA_TQ:(j + 1) * MLA_TQ, :] = (acc / l).T.astype(o_ref.dtype)

    m_prev = pass1(0, 0)
    for j in range(1, n_tiles):
        m_cur = pass1(j, j % 2)
        pass2(j - 1, (j - 1) % 2, m_prev)
        m_prev = m_cur
    pass2(n_tiles - 1, (n_tiles - 1) % 2, m_prev)


def _mla(q, k, vt):
    B, H, S, Dq = q.shape
    Dv = MLA_V_DIM
    tq = MLA_TQ * MLA_TILES
    return pl.pallas_call(
        _mla_kernel,
        out_shape=jax.ShapeDtypeStruct((B, S, H * Dv), BF16),
        grid=(B, H, S // tq),
        in_specs=[
            pl.BlockSpec((1, 1, tq, Dq), lambda b, h, i: (b, h, i, 0)),
            pl.BlockSpec((1, 1, S, Dq), lambda b, h, i: (b, h, 0, 0)),
            pl.BlockSpec((1, 1, Dv, S), lambda b, h, i: (b, h, 0, 0)),
        ],
        out_specs=pl.BlockSpec((1, tq, Dv), lambda b, h, i: (b, i, h)),
        scratch_shapes=[pltpu.VMEM((2, S, MLA_TQ), F32)],
        compiler_params=_params("arbitrary", "arbitrary", "arbitrary"),
        name="mla",
    )(q, k, vt)


def _layer_norm(z, g, b):
    mu = jnp.mean(z, axis=-1, keepdims=True)
    zc = z - mu
    var = jnp.mean(zc * zc, axis=-1, keepdims=True)
    return zc * lax.rsqrt(var + LN_EPS) * g + b


def _first_lane_where(cond, lane):
    return jnp.min(jnp.where(cond, lane, LANES), axis=-1, keepdims=True)


def _mix_kernel(ona_ref, omla_ref, x_ref, wo_ref, g_ref, b_ref, wr_ref, br_ref,
                x1_ref, ri_ref, gate_ref, cnt_ref, carry_ref):
    step = pl.program_id(0)

    @pl.when(step == 0)
    def _():
        carry_ref[...] = jnp.zeros_like(carry_ref)

    ts = MIX_SUB
    subs = [slice(j * ts, (j + 1) * ts) for j in range(x_ref.shape[0] // ts)]
    lane = lax.broadcasted_iota(jnp.int32, (ts, LANES), 1)
    g_mask = (lane >= N_EXPERTS) & (lane < N_EXPERTS + N_GROUPS)
    r_i = lax.broadcasted_iota(jnp.int32, (ts, ts), 0)
    c_i = lax.broadcasted_iota(jnp.int32, (ts, ts), 1)
    tri = jnp.where(c_i < r_i, 1.0, 0.0).astype(BF16)

    def project(rows):
        return (_dot(ona_ref[rows, :], wo_ref[0:NA_WIDTH, :])
                + _dot(omla_ref[rows, :], wo_ref[NA_WIDTH:, :]))

    mix = project(subs[0])
    for j, rows in enumerate(subs):
        nxt = project(subs[j + 1]) if j + 1 < len(subs) else None
        _norm_route(rows, mix, lane, g_mask, tri, x_ref, g_ref, b_ref, wr_ref, br_ref,
                    x1_ref, ri_ref, gate_ref, carry_ref)
        mix = nxt
    cnt_ref[...] = carry_ref[...]


def _norm_route(rows, mix, lane, g_mask, tri, x_ref, g_ref, b_ref, wr_ref, br_ref,
                x1_ref, ri_ref, gate_ref, carry_ref):
    x1 = _layer_norm(ALPHA * x_ref[rows, :] + mix, g_ref[...], b_ref[...])
    x1_ref[rows, :] = x1

    logit = _dot(x1.astype(BF16), wr_ref[...]) + br_ref[...]
    gl = jnp.where(g_mask, logit, NEG_BIG)
    gmax = jnp.max(gl, axis=-1, keepdims=True)
    gsum = jnp.sum(jnp.where(g_mask, jnp.exp(gl - gmax), 0.0), axis=-1, keepdims=True)
    g_val = 1.0 / gsum
    g_idx = _first_lane_where(gl == gmax, lane) - N_EXPERTS

    e_mask = (lane < N_EXPERTS) & ((lane // EXPERTS_PER_GROUP) == g_idx)
    el = jnp.where(e_mask, logit, NEG_BIG)
    emax = jnp.max(el, axis=-1, keepdims=True)
    esum = jnp.sum(jnp.where(e_mask, jnp.exp(el - emax), 0.0), axis=-1, keepdims=True)
    i1 = _first_lane_where(el == emax, lane)
    el2 = jnp.where(lane == i1, NEG_BIG, el)
    emax2 = jnp.max(el2, axis=-1, keepdims=True)
    i2 = _first_lane_where(el2 == emax2, lane)
    v1 = 1.0 / esum
    v2 = jnp.exp(emax2 - emax) / esum
    vsum = v1 + v2
    w1 = g_val * (v1 / vsum)
    w2 = g_val * (v2 / vsum)

    oh1 = lane == i1
    oh2 = lane == i2
    oh = jnp.where(oh1 | oh2, 1.0, 0.0)
    before = _dot(tri, oh.astype(BF16)) + carry_ref[...]
    rank1 = jnp.sum(jnp.where(oh1, before, 0.0), axis=-1, keepdims=True).astype(jnp.int32)
    rank2 = jnp.sum(jnp.where(oh2, before, 0.0), axis=-1, keepdims=True).astype(jnp.int32)
    carry_ref[...] = carry_ref[...] + jnp.sum(oh, axis=0, keepdims=True)

    ri = jnp.where(lane == 0, i1, jnp.where(lane == 1, i2,
                   jnp.where(lane == 2, rank1, jnp.where(lane == 3, rank2, 0))))
    ri_ref[:, rows] = ri.T[0:SUBLANES, :]
    gate_ref[rows, :] = jnp.where(lane == 0, w1, jnp.where(lane == 1, w2, 0.0))


def _mix_out(o_na, o_mla, x, wo, g, b, wr, br):
    N, D = x.shape
    tm = MIX_TM
    row = lambda i: (i, 0)
    return pl.pallas_call(
        _mix_kernel,
        out_shape=(
            jax.ShapeDtypeStruct((N, D), F32),
            jax.ShapeDtypeStruct((SUBLANES, N), jnp.int32),
            jax.ShapeDtypeStruct((N, LANES), F32),
            jax.ShapeDtypeStruct((1, LANES), F32),
        ),
        grid=(N // tm,),
        in_specs=[
            pl.BlockSpec((tm, NA_WIDTH), row),
            pl.BlockSpec((tm, NA_WIDTH), row),
            pl.BlockSpec((tm, D), row),
            _resident(wo.shape), _resident(g.shape), _resident(b.shape),
            _resident(wr.shape), _resident(br.shape),
        ],
        out_specs=(
            pl.BlockSpec((tm, D), row),
            pl.BlockSpec((SUBLANES, tm), lambda i: (0, i)),
            pl.BlockSpec((tm, LANES), row),
            pl.BlockSpec((1, LANES), lambda i: (0, 0)),
        ),
        scratch_shapes=[pltpu.VMEM((1, LANES), F32)],
        compiler_params=_params("arbitrary"),
        name="mix_out",
    )(o_na, o_mla, x, wo, g, b, wr, br)


def _pos_kernel(start_ref, ri_ref, pos_ref):
    eid = ri_ref[0:TOP_K_EXPERT, :]
    pos = ri_ref[TOP_K_EXPERT:2 * TOP_K_EXPERT, :]
    for e in range(N_EXPERTS):
        pos = pos + jnp.where(eid == e, start_ref[e], 0)
    pos_ref[...] = pos


def _positions(pad_starts, ri):
    n = ri.shape[1]
    return pl.pallas_call(
        _pos_kernel,
        out_shape=jax.ShapeDtypeStruct((TOP_K_EXPERT, n), jnp.int32),
        grid_spec=pltpu.PrefetchScalarGridSpec(
            num_scalar_prefetch=1,
            grid=(1,),
            in_specs=[pl.BlockSpec(ri.shape, lambda i, s: (0, 0))],
            out_specs=pl.BlockSpec((TOP_K_EXPERT, n), lambda i, s: (0, 0)),
        ),
        compiler_params=_params("arbitrary"),
        name="positions",
    )(pad_starts, ri)


def _row_copy(src, src_row, dst, dst_row, sem):
    return pltpu.make_async_copy(src.at[pl.ds(src_row, 1), :], dst.at[pl.ds(dst_row, 1), :], sem)


_PAD_CHUNKS = tuple(1 << b for b in reversed(range(3, MOE_TB.bit_length() - 1)))


def _dispatch_kernel(pos_ref, fill_ref, x_ref, xs_ref, zero_ref, sem, zsem):
    tm = x_ref.shape[0]
    n_tok = pl.num_programs(0) * tm
    base = pl.program_id(0) * tm

    def pad_copies(fn):
        def per_expert(e, carry):
            start = fill_ref[e]
            n = fill_ref[N_EXPERTS + e]
            head = (-start) & (SUBLANES - 1)
            for r in range(SUBLANES - 1):
                @pl.when(r < head)
                def _(r=r):
                    fn(_row_copy(zero_ref, 0, xs_ref, start + r, zsem))
            off = start + head
            rem = n - head
            for c in _PAD_CHUNKS:
                hit = (rem & c) != 0

                @pl.when(hit)
                def _(off=off, c=c):
                    fn(pltpu.make_async_copy(zero_ref.at[pl.ds(0, c), :],
                                             xs_ref.at[pl.ds(pl.multiple_of(off, SUBLANES), c), :],
                                             zsem))
                off = off + jnp.where(hit, c, 0)
            return carry
        lax.fori_loop(0, N_EXPERTS, per_expert, 0)

        zrows = zero_ref.shape[0]

        def tail(t, carry):
            row = pl.multiple_of(fill_ref[2 * N_EXPERTS] + t * zrows, zrows)
            fn(pltpu.make_async_copy(zero_ref, xs_ref.at[pl.ds(row, zrows), :], zsem))
            return carry
        lax.fori_loop(0, (xs_ref.shape[0] - fill_ref[2 * N_EXPERTS]) // zrows, tail, 0)

    @pl.when(pl.program_id(0) == 0)
    def _():
        zero_ref[...] = jnp.zeros_like(zero_ref)
        pad_copies(lambda d: d.start())

    def issue(j, carry):
        for kk in range(TOP_K_EXPERT):
            _row_copy(x_ref, j, xs_ref, pos_ref[kk * n_tok + base + j], sem).start()
        return carry

    lax.fori_loop(0, tm, issue, 0, unroll=8)

    @pl.when(pl.program_id(0) == 0)
    def _():
        pad_copies(lambda d: d.wait())

    for kk in range(TOP_K_EXPERT):
        pltpu.make_async_copy(x_ref, xs_ref.at[pl.ds(0, tm), :], sem).wait()


def _dispatch(pos, fill, x1, n_rows):
    N, C = x1.shape
    tm = DISP_TM
    return pl.pallas_call(
        _dispatch_kernel,
        out_shape=jax.ShapeDtypeStruct((n_rows, C), x1.dtype),
        grid_spec=pltpu.PrefetchScalarGridSpec(
            num_scalar_prefetch=2,
            grid=(N // tm,),
            in_specs=[pl.BlockSpec((tm, C), lambda i, pos, fill: (i, 0))],
            out_specs=pl.BlockSpec(memory_space=pl.ANY),
            scratch_shapes=[pltpu.VMEM((_PAD_CHUNKS[0], C), x1.dtype),
                            pltpu.SemaphoreType.DMA, pltpu.SemaphoreType.DMA],
        ),
        compiler_params=_params("arbitrary"),
        name="dispatch",
    )(pos, fill, x1)


def _experts_kernel(be_ref, ord_ref, act_ref, nb_ref, xs_ref, wg_hbm, wu_hbm, wd_hbm, ys_ref,
                    wg_f, wu_f, wd_f, wg_s, wu_s, wd_s, sem):
    i = pl.program_id(0)
    active = i < nb_ref[0]
    e = be_ref[i]
    o = ord_ref[i]
    fresh = active & ((i == 0) | (e != be_ref[jnp.maximum(i - 1, 0)]))

    def fetch(ex, s):
        return (pltpu.make_async_copy(wg_hbm.at[ex], wg_f.at[s], sem.at[s, 0]),
                pltpu.make_async_copy(wu_hbm.at[ex], wu_f.at[s], sem.at[s, 1]),
                pltpu.make_async_copy(wd_hbm.at[ex], wd_f.at[s], sem.at[s, 2]))

    def narrow(s):
        wg_s[s] = wg_f[s].astype(BF16)
        wu_s[s] = wu_f[s].astype(BF16)
        wd_s[s] = wd_f[s].astype(BF16)

    def compute(s):
        xb = xs_ref[...].astype(BF16)
        gp = _dot(xb, wg_s[s])
        up = _dot(xb, wu_s[s])
        hdn = (gp * jax.nn.sigmoid(gp) * up).astype(BF16)
        ys_ref[...] = _dot(hdn, wd_s[s])

    def start_if_any(ex, s):
        @pl.when(ex >= 0)
        def _():
            for d in fetch(ex, s):
                d.start()

    @pl.when(i == 0)
    def _():
        for d in fetch(e, 0):
            d.start()
        start_if_any(act_ref[1], 1)
        for d in fetch(e, 0):
            d.wait()
        narrow(0)
        start_if_any(act_ref[2], 0)

    for s in (0, 1):
        t = 1 - s

        @pl.when(fresh & (o % 2 == s))
        def _(s=s, t=t):
            nxt = act_ref[o + 1]

            @pl.when(nxt >= 0)
            def _():
                for d in fetch(nxt, t):
                    d.wait()
            narrow(t)
            compute(s)
            start_if_any(act_ref[o + 3], t)

        @pl.when(active & jnp.logical_not(fresh) & (o % 2 == s))
        def _(s=s):
            compute(s)

    @pl.when(jnp.logical_not(active))
    def _():
        ys_ref[...] = jnp.zeros_like(ys_ref)


def _experts(block_e, block_ord, act, n_used, xs, w_gate, w_up, w_down):
    P, C = xs.shape
    E, D, F = w_gate.shape
    tb = MOE_TB
    n_blocks = P // tb
    hbm = pl.BlockSpec(memory_space=pl.ANY)
    return pl.pallas_call(
        _experts_kernel,
        out_shape=jax.ShapeDtypeStruct((P, C), F32),
        grid_spec=pltpu.PrefetchScalarGridSpec(
            num_scalar_prefetch=4,
            grid=(n_blocks,),
            in_specs=[
                pl.BlockSpec((tb, C), lambda i, be, od, ac, nb: (jnp.minimum(i, nb[0] - 1), 0)),
                hbm, hbm, hbm,
            ],
            out_specs=pl.BlockSpec((tb, C), lambda i, be, od, ac, nb: (i, 0)),
            scratch_shapes=[pltpu.VMEM((2, D, F), F32), pltpu.VMEM((2, D, F), F32),
                            pltpu.VMEM((2, F, D), F32),
                            pltpu.VMEM((2, D, F), BF16), pltpu.VMEM((2, D, F), BF16),
                            pltpu.VMEM((2, F, D), BF16), pltpu.SemaphoreType.DMA((2, 3))],
        ),
        compiler_params=_params("arbitrary"),
        name="experts",
    )(block_e, block_ord, act, n_used, xs, w_gate, w_up, w_down)


def _final_kernel(pos_ref, x1_ref, gate_ref, ys_ref, p_ref, g_ref, b_ref, wpg_ref, wpe_ref,
                  gp_ref, o_ref, buf, sem):
    tm = x1_ref.shape[0]
    i = pl.program_id(0)
    last = pl.num_programs(0) - 1
    slot = i % 2

    def issue(tile, s, unroll):
        n_tok = pl.num_programs(0) * tm
        base = tile * tm

        def body(j, carry):
            for kk in range(TOP_K_EXPERT):
                _row_copy(ys_ref, pos_ref[kk * n_tok + base + j],
                          buf.at[s, kk], j, sem.at[s]).start()
            return carry
        lax.fori_loop(0, tm, body, 0, unroll=unroll)

    def drain(s):
        for kk in range(TOP_K_EXPERT):
            pltpu.make_async_copy(ys_ref.at[pl.ds(0, tm), :], buf.at[s, kk], sem.at[s]).wait()

    @pl.when(i == 0)
    def _():
        issue(0, 0, 8)

    drain(slot)
    issue(jnp.minimum(i + 1, last), 1 - slot, True)

    gates = gate_ref[...]
    g1 = gates[:, 0:1]
    g2 = gates[:, 1:2]
    ffn = buf[slot, 0] * g1 + buf[slot, 1] * g2
    x2 = _layer_norm(ALPHA * x1_ref[...] + ffn, g_ref[...], b_ref[...])

    gate = jax.nn.sigmoid(_dot(x2.astype(BF16), wpg_ref[...]))
    e = _dot(p_ref[...].astype(BF16), wpe_ref[...])
    t = e * gate
    ple = t * lax.rsqrt(jnp.mean(t * t, axis=-1, keepdims=True) + RMS_EPS) * gp_ref[...]
    o_ref[...] = x2 + ple

    @pl.when(i == last)
    def _():
        drain(1 - slot)


def _final(pos, x1, gates, ys, p, g, b, wpg, wpe, gp):
    N, D = x1.shape
    tm = FIN_TM
    row = lambda i, pos: (i, 0)
    return pl.pallas_call(
        _final_kernel,
        out_shape=jax.ShapeDtypeStruct((N, D), F32),
        grid_spec=pltpu.PrefetchScalarGridSpec(
            num_scalar_prefetch=1,
            grid=(N // tm,),
            in_specs=[
                pl.BlockSpec((tm, D), row),
                pl.BlockSpec((tm, LANES), row),
                pl.BlockSpec(memory_space=pl.ANY),
                pl.BlockSpec((tm, PLE_DIM), row),
                _resident(g.shape), _resident(b.shape), _resident(wpg.shape),
                _resident(wpe.shape), _resident(gp.shape),
            ],
            out_specs=pl.BlockSpec((tm, D), row),
            scratch_shapes=[pltpu.VMEM((2, TOP_K_EXPERT, tm, D), F32),
                            pltpu.SemaphoreType.DMA((2,))],
        ),
        compiler_params=_params("arbitrary"),
        name="final",
    )(pos, x1, gates, ys, p, g, b, wpg, wpe, gp)


def _rope_table(positions):
    inv_freq = 1.0 / (ROPE_THETA ** (jnp.arange(0, MLA_ROPE_DIM, 2, dtype=F32) / MLA_ROPE_DIM))
    half = MLA_ROPE_DIM // 2
    phase = jnp.concatenate([jnp.zeros((2 * half,), F32), jnp.full((2 * half,), np.pi / 2, F32)])
    return jnp.cos(positions.astype(F32)[..., None] * jnp.tile(inv_freq, 4) - phase)


def _layer(x, p_i, rope_t, w_in, rpb, q_norm_g, kv_norm_g, w_uq, w_uk, w_uv, w_o, ln1_g, ln1_b,
           w_group, b_group, w_router, b_router, w_gate, w_up, w_down, ln2_g, ln2_b,
           w_ple, w_ple_gate, ple_norm_g):
    B, S, D = x.shape
    N = B * S
    s3 = 3 * NA_WIDTH
    half = MLA_ROPE_DIM // 2

    wqk = w_in[:, :2 * NA_WIDTH].astype(BF16)
    wc = w_in[:, s3:].astype(BF16)
    wvt = w_in[:, 2 * NA_WIDTH:s3].T.astype(BF16)
    uq = w_uq.reshape(Q_LORA_RANK, MLA_HEADS, MLA_QK_DIM)
    nope, x1c, x2c = (uq[..., :MLA_NOPE_DIM], uq[..., MLA_NOPE_DIM:MLA_NOPE_DIM + half],
                      uq[..., MLA_NOPE_DIM + half:])
    wuq = jnp.concatenate([nope, x1c, x2c, x2c, x1c], axis=-1).reshape(Q_LORA_RANK, -1).astype(BF16)
    wuk = w_uk.astype(BF16)
    wuvt = w_uv.T.astype(BF16)
    row = lambda v: v.reshape(1, -1).astype(F32)

    q_na, k_na, v_nat, q_m, k_m, v_mt = _proj(x, rope_t, wqk, wvt, wc, wuq, wuk, wuvt,
                                               row(q_norm_g), row(kv_norm_g))
    o_na = _na(q_na, k_na, v_nat, _na_bias_table(rpb))
    o_mla = _mla(q_m, k_m, v_mt)

    pad = LANES - N_EXPERTS - N_GROUPS
    wr = jnp.concatenate([w_router, w_group, jnp.zeros((D, pad), F32)], axis=1).astype(BF16)
    br = jnp.concatenate([b_router.reshape(-1), b_group, jnp.zeros((pad,), F32)]).reshape(1, -1)
    x1, ri, gates, cnt = _mix_out(o_na.reshape(N, -1), o_mla.reshape(N, -1), x.reshape(N, D),
                                       w_o.astype(BF16), row(ln1_g), row(ln1_b), wr, br)

    tb = MOE_TB
    counts = cnt[0, :N_EXPERTS].astype(jnp.int32)
    padded = ((counts + tb - 1) // tb) * tb
    pad_ends = jnp.cumsum(padded)
    pad_starts = pad_ends - padded
    pos = _positions(pad_starts.astype(jnp.int32), ri).reshape(-1)
    fill = jnp.concatenate([pad_starts + counts, padded - counts, pad_ends[-1:]]).astype(jnp.int32)
    n_blocks = (N * TOP_K_EXPERT + N_EXPERTS * (tb - 1) + tb - 1) // tb
    blk_row = jnp.arange(n_blocks, dtype=jnp.int32) * tb
    block_e = jnp.minimum(jnp.sum(pad_ends[None, :] <= blk_row[:, None], axis=1),
                          N_EXPERTS - 1).astype(jnp.int32)
    n_used = (pad_ends[-1:] // tb).astype(jnp.int32)
    present = counts > 0
    ord_e = jnp.cumsum(present.astype(jnp.int32)) - 1
    block_ord = ord_e[block_e].astype(jnp.int32)
    ords = jnp.arange(N_EXPERTS + 4, dtype=jnp.int32)
    hit = present[None, :] & (ord_e[None, :] == ords[:, None])
    act = jnp.where(jnp.any(hit, axis=1),
                    jnp.sum(jnp.where(hit, jnp.arange(N_EXPERTS, dtype=jnp.int32)[None, :], 0), axis=1),
                    -1).astype(jnp.int32)

    xs = _dispatch(pos, fill, x1, n_blocks * tb)
    ys = _experts(block_e, block_ord, act, n_used, xs, w_gate, w_up, w_down)
    out = _final(pos, x1, gates, ys, p_i.reshape(N, -1), row(ln2_g), row(ln2_b),
                 w_ple_gate.astype(BF16), w_ple.astype(BF16), row(ple_norm_g))
    return out.reshape(B, S, D)


def kernel(x, p, positions, w_in, rpb, q_norm_g, kv_norm_g, w_uq, w_uk, w_uv, w_o, ln1_g, ln1_b,
           w_group, b_group, w_router, b_router, w_gate, w_up, w_down, ln2_g, ln2_b,
           w_ple, w_ple_gate, ple_norm_g):
    rope_t = _rope_table(positions)
    for i in range(DEPTH):
        x = _layer(x, p[i], rope_t, w_in[i], rpb[i], q_norm_g[i], kv_norm_g[i], w_uq[i], w_uk[i],
                   w_uv[i], w_o[i], ln1_g[i], ln1_b[i], w_group[i], b_group[i], w_router[i],
                   b_router[i], w_gate[i], w_up[i], w_down[i], ln2_g[i], ln2_b[i],
                   w_ple[i], w_ple_gate[i], ple_norm_g[i])
    return x
```

```python
import numpy as np
import jax
import jax.numpy as jnp
from jax import lax
from jax.experimental import pallas as pl
from jax.experimental.pallas import tpu as pltpu

D_MODEL = 2048
DEPTH = 1
GRID_W = 64
PLE_DIM = 256
NA_HEADS = 8
NA_HEAD_DIM = 128
NA_WIN_H = 8
NA_WIN_W = 16
NA_WIDTH = NA_HEADS * NA_HEAD_DIM
MLA_HEADS = 8
MLA_NOPE_DIM = 128
MLA_ROPE_DIM = 64
MLA_V_DIM = 128
MLA_QK_DIM = MLA_NOPE_DIM + MLA_ROPE_DIM
Q_LORA_RANK = 512
KV_LORA_RANK = 512
ROPE_THETA = 10000.0
N_GROUPS = 8
EXPERTS_PER_GROUP = 8
N_EXPERTS = N_GROUPS * EXPERTS_PER_GROUP
TOP_K_EXPERT = 2
D_EXPERT = 512
ALPHA = (2 * DEPTH) ** 0.25
LN_EPS = 1e-5
RMS_EPS = 1e-6

LANES = 128
SUBLANES = 8
VMEM_LIMIT_BYTES = 56 * 1024 * 1024

PROJ_TM = 256
NA_ROWS = 4
NA_QB = NA_ROWS * GRID_W
NA_WIN_ROWS = NA_ROWS + NA_WIN_H
NA_KB = NA_WIN_ROWS * GRID_W
MLA_TQ = 256
MLA_KC = 1024
MLA_TILES = 8
MIX_TM = 512
MIX_SUB = 256
MOE_TB = 256
DISP_TM = 256
FIN_TM = 256
NEG_BIG = -1e30
LOG2E = 1.4426950408889634

BF16 = jnp.bfloat16
F32 = jnp.float32

_NT = (((1,), (1,)), ((), ()))


def _dot(a, b):
    return jnp.dot(a, b, preferred_element_type=F32)


def _dot_nt(a, b):
    return lax.dot_general(a, b, _NT, preferred_element_type=F32)


def _params(*sem):
    return pltpu.CompilerParams(dimension_semantics=sem, vmem_limit_bytes=VMEM_LIMIT_BYTES)


def _resident(shape):
    nd = len(shape)
    return pl.BlockSpec(shape, lambda *_: (0,) * nd, pipeline_mode=pl.Buffered(1))


def _proj_kernel(x_ref, t_ref, wqk_ref, wvt_ref, wc_ref, wuq_ref, wuk_ref, wuvt_ref, gq_ref, gkv_ref,
                 qna_ref, kna_ref, vnat_ref, qm_ref, km_ref, vmt_ref):
    xb = x_ref[0].astype(BF16)
    na_scale = NA_HEAD_DIM ** -0.5 * LOG2E
    mla_scale = MLA_QK_DIM ** -0.5 * LOG2E
    q = _dot(xb, wqk_ref[:, 0:NA_WIDTH]) * na_scale
    k = _dot(xb, wqk_ref[:, NA_WIDTH:2 * NA_WIDTH])
    vt = _dot_nt(wvt_ref[...], xb)
    for h in range(NA_HEADS):
        sl = slice(h * NA_HEAD_DIM, (h + 1) * NA_HEAD_DIM)
        qna_ref[0, h] = q[:, sl].astype(BF16)
        kna_ref[0, h] = k[:, sl].astype(BF16)
        vnat_ref[0, h] = vt[sl, :].astype(BF16)

    half = MLA_ROPE_DIM // 2
    cq = _dot(xb, wc_ref[:, 0:Q_LORA_RANK])
    ckv = _dot(xb, wc_ref[:, Q_LORA_RANK:Q_LORA_RANK + KV_LORA_RANK])
    kr = _dot(xb, wc_ref[:, Q_LORA_RANK + KV_LORA_RANK:])
    kr4 = jnp.concatenate([kr, kr[:, half:], kr[:, :half]], axis=1)

    def rms(c, g):
        return c * lax.rsqrt(jnp.mean(c * c, axis=-1, keepdims=True) + RMS_EPS) * g

    cqn = rms(cq, gq_ref[...]).astype(BF16)
    ckvn = rms(ckv, gkv_ref[...]).astype(BF16)

    t = t_ref[0]
    e = kr4 * t
    lane = lax.broadcasted_iota(jnp.int32, (1, LANES), 1)
    sign = jnp.where((lane // 32) % 2 == 0, -1.0, 1.0).astype(F32)
    kpe = (e + sign * pltpu.roll(e, 64, 1)).astype(BF16)

    qf = _dot(cqn, wuq_ref[...])
    kn = _dot(ckvn, wuk_ref[...])
    vmt = _dot_nt(wuvt_ref[...], ckvn)
    ts = t * mla_scale
    for h in range(MLA_HEADS):
        qm_ref[0, h, :, 0:128] = (qf[:, h * 256:h * 256 + 128] * mla_scale).astype(BF16)
        qm_ref[0, h, :, 128:256] = (qf[:, h * 256 + 128:(h + 1) * 256] * ts).astype(BF16)
        km_ref[0, h, :, 0:128] = kn[:, h * 128:(h + 1) * 128].astype(BF16)
        km_ref[0, h, :, 128:256] = kpe
        vmt_ref[0, h] = vmt[h * 128:(h + 1) * 128, :].astype(BF16)


def _proj(x, rope_t, wqk, wvt, wc, wuq, wuk, wuvt, gq, gkv):
    B, S, D = x.shape
    tm = PROJ_TM
    hm = lambda b, i: (b, 0, i, 0)
    hmt = lambda b, i: (b, 0, 0, i)
    out_shape = (
        jax.ShapeDtypeStruct((B, NA_HEADS, S, NA_HEAD_DIM), BF16),
        jax.ShapeDtypeStruct((B, NA_HEADS, S, NA_HEAD_DIM), BF16),
        jax.ShapeDtypeStruct((B, NA_HEADS, NA_HEAD_DIM, S), BF16),
        jax.ShapeDtypeStruct((B, MLA_HEADS, S, 256), BF16),
        jax.ShapeDtypeStruct((B, MLA_HEADS, S, 256), BF16),
        jax.ShapeDtypeStruct((B, MLA_HEADS, MLA_V_DIM, S), BF16),
    )
    return pl.pallas_call(
        _proj_kernel,
        out_shape=out_shape,
        grid=(B, S // tm),
        in_specs=[
            pl.BlockSpec((1, tm, D), lambda b, i: (b, i, 0)),
            pl.BlockSpec((1, tm, LANES), lambda b, i: (b, i, 0)),
            _resident(wqk.shape), _resident(wvt.shape), _resident(wc.shape), _resident(wuq.shape),
            _resident(wuk.shape), _resident(wuvt.shape), _resident(gq.shape), _resident(gkv.shape),
        ],
        out_specs=(
            pl.BlockSpec((1, NA_HEADS, tm, NA_HEAD_DIM), hm),
            pl.BlockSpec((1, NA_HEADS, tm, NA_HEAD_DIM), hm),
            pl.BlockSpec((1, NA_HEADS, NA_HEAD_DIM, tm), hmt),
            pl.BlockSpec((1, MLA_HEADS, tm, 256), hm),
            pl.BlockSpec((1, MLA_HEADS, tm, 256), hm),
            pl.BlockSpec((1, MLA_HEADS, MLA_V_DIM, tm), hmt),
        ),
        compiler_params=_params("arbitrary", "arbitrary"),
        name="proj",
    )(x, rope_t, wqk, wvt, wc, wuq, wuk, wuvt, gq, gkv)


def _na_bias_table(rpb):
    rows = GRID_W
    n_blocks = rows // NA_ROWS
    n_ro, n_co = 2 * NA_WIN_H - 1, 2 * NA_WIN_W - 1
    kr, qr = np.arange(NA_WIN_ROWS), np.arange(NA_ROWS)
    kc, qc = np.arange(GRID_W), np.arange(GRID_W)
    c0 = np.clip(qc - NA_WIN_W // 2, 0, GRID_W - NA_WIN_W)
    col_ok = (kc[:, None] >= c0[None, :]) & (kc[:, None] < c0[None, :] + NA_WIN_W)
    col_off = np.clip(kc[:, None] - qc[None, :] + (NA_WIN_W - 1), 0, n_co - 1)
    oh_c = np.zeros((n_co, GRID_W, GRID_W), np.float32)
    oh_c[col_off, kc[:, None], qc[None, :]] = 1.0
    oh_r = np.zeros((3, n_ro, NA_WIN_ROWS, NA_ROWS), np.float32)
    row_ok = np.zeros((3, NA_WIN_ROWS, NA_ROWS), bool)
    for c, rb in enumerate((0, 1, n_blocks - 1)):
        w0 = int(np.clip(NA_ROWS * rb - NA_WIN_H // 2, 0, rows - NA_WIN_ROWS))
        r = NA_ROWS * rb + qr
        r0 = np.clip(r - NA_WIN_H // 2, 0, rows - NA_WIN_H)
        krow = w0 + kr
        row_ok[c] = (krow[:, None] >= r0[None, :]) & (krow[:, None] < r0[None, :] + NA_WIN_H)
        row_off = np.clip(krow[:, None] - r[None, :] + (NA_WIN_H - 1), 0, n_ro - 1)
        oh_r[c, row_off, kr[:, None], qr[None, :]] = 1.0
    sel = np.zeros((NA_ROWS, n_co, GRID_W, NA_ROWS, GRID_W), np.float32)
    for r in range(NA_ROWS):
        sel[r, :, :, r, :] = oh_c
    sel = sel.reshape(NA_ROWS * n_co, GRID_W, NA_QB)
    ok = row_ok[:, :, None, :, None] & col_ok[None, None, :, None, :]
    mask = np.where(ok, 0.0, NEG_BIG).astype(np.float32).reshape(3, NA_WIN_ROWS, GRID_W, NA_QB)
    hi = lax.Precision.HIGHEST
    a = jnp.einsum('hrc,zrkq->hzkqc', rpb.astype(F32) * LOG2E, oh_r, precision=hi)
    a = a.reshape(rpb.shape[0], 3, NA_WIN_ROWS, NA_ROWS * n_co)
    tab = jnp.einsum('hzkj,jxn->hzkxn', a, sel, precision=hi) + mask[None]
    return tab.reshape(rpb.shape[0], 3, NA_KB, NA_QB)


def _na_kernel(q_ref, k_ref, vt_ref, tab_ref, o_ref, st_ref, m_ref):
    n_blocks = q_ref.shape[2] // NA_QB
    rows = n_blocks * NA_ROWS

    def window(rb):
        w0 = jnp.clip(NA_ROWS * rb - NA_WIN_H // 2, 0, rows - NA_WIN_ROWS)
        return pl.multiple_of(w0 * GRID_W, NA_QB)

    def scores(rb, slot):
        tok0 = window(rb)
        q0 = pl.multiple_of(rb * NA_QB, NA_QB)
        cls = jnp.where(rb == 0, 0, jnp.where(rb == n_blocks - 1, 2, 1))
        st = _dot_nt(k_ref[0, 0, pl.ds(tok0, NA_KB), :],
                     q_ref[0, 0, pl.ds(q0, NA_QB), :]) + tab_ref[0, cls]
        st_ref[slot] = st
        m_ref[slot] = jnp.max(st, axis=0, keepdims=True)

    def output(rb, slot):
        tok0 = window(rb)
        q0 = pl.multiple_of(rb * NA_QB, NA_QB)
        p = jnp.exp2(st_ref[slot] - m_ref[slot])
        l = jnp.sum(p, axis=0, keepdims=True)
        ot = _dot(vt_ref[0, 0, :, pl.ds(tok0, NA_KB)], p.astype(BF16)) / l
        o_ref[0, pl.ds(q0, NA_QB), :] = ot.T.astype(o_ref.dtype)

    scores(0, 0)

    def body(t, carry):
        rb = 2 * t + 1
        scores(rb, 1)
        output(rb - 1, 0)
        scores(rb + 1, 0)
        output(rb, 1)
        return carry

    lax.fori_loop(0, n_blocks // 2 - 1, body, 0)
    scores(n_blocks - 1, 1)
    output(n_blocks - 2, 0)
    output(n_blocks - 1, 1)


def _na(q, k, vt, tab):
    B, H, S, Dh = q.shape
    return pl.pallas_call(
        _na_kernel,
        out_shape=jax.ShapeDtypeStruct((B, S, H * Dh), BF16),
        grid=(H, B),
        in_specs=[
            pl.BlockSpec((1, 1, S, Dh), lambda h, b: (b, h, 0, 0)),
            pl.BlockSpec((1, 1, S, Dh), lambda h, b: (b, h, 0, 0)),
            pl.BlockSpec((1, 1, Dh, S), lambda h, b: (b, h, 0, 0)),
            pl.BlockSpec((1, 3, NA_KB, NA_QB), lambda h, b: (h, 0, 0, 0)),
        ],
        out_specs=pl.BlockSpec((1, S, Dh), lambda h, b: (b, 0, h)),
        scratch_shapes=[pltpu.VMEM((2, NA_KB, NA_QB), F32), pltpu.VMEM((2, 1, NA_QB), F32)],
        compiler_params=_params("arbitrary", "arbitrary"),
        name="na",
    )(q, k, vt, tab)


def _mla_kernel(q_ref, k_ref, vt_ref, o_ref, st_ref):
    n_chunks = k_ref.shape[2] // MLA_KC
    n_tiles = q_ref.shape[2] // MLA_TQ

    def pass1(j, slot):
        q = q_ref[0, 0, j * MLA_TQ:(j + 1) * MLA_TQ, :]
        m = None
        for c in range(n_chunks):
            ks = slice(c * MLA_KC, (c + 1) * MLA_KC)
            st = _dot_nt(k_ref[0, 0, ks, :], q)
            st_ref[slot, ks, :] = st
            mc = jnp.max(st, axis=0, keepdims=True)
            m = mc if c == 0 else jnp.maximum(m, mc)
        return m

    def pass2(j, slot, m):
        l = acc = None
        for c in range(n_chunks):
            ks = slice(c * MLA_KC, (c + 1) * MLA_KC)
            p = jnp.exp2(st_ref[slot, ks, :] - m)
            ps = jnp.sum(p, axis=0, keepdims=True)
            pv = _dot(vt_ref[0, 0, :, ks], p.astype(BF16))
            l = ps if c == 0 else l + ps
            acc = pv if c == 0 else acc + pv
        o_ref[0, j * MLA_TQ:(j + 1) * MLA_TQ, :] = (acc / l).T.astype(o_ref.dtype)

    m_prev = pass1(0, 0)
    for j in range(1, n_tiles):
        m_cur = pass1(j, j % 2)
        pass2(j - 1, (j - 1) % 2, m_prev)
        m_prev = m_cur
    pass2(n_tiles - 1, (n_tiles - 1) % 2, m_prev)


def _mla(q, k, vt):
    B, H, S, Dq = q.shape
    Dv = MLA_V_DIM
    tq = MLA_TQ * MLA_TILES
    return pl.pallas_call(
        _mla_kernel,
        out_shape=jax.ShapeDtypeStruct((B, S, H * Dv), BF16),
        grid=(B, H, S // tq),
        in_specs=[
            pl.BlockSpec((1, 1, tq, Dq), lambda b, h, i: (b, h, i, 0)),
            pl.BlockSpec((1, 1, S, Dq), lambda b, h, i: (b, h, 0, 0)),
            pl.BlockSpec((1, 1, Dv, S), lambda b, h, i: (b, h, 0, 0)),
        ],
        out_specs=pl.BlockSpec((1, tq, Dv), lambda b, h, i: (b, i, h)),
        scratch_shapes=[pltpu.VMEM((2, S, MLA_TQ), F32)],
        compiler_params=_params("arbitrary", "arbitrary", "arbitrary"),
        name="mla",
    )(q, k, vt)


def _layer_norm(z, g, b):
    mu = jnp.mean(z, axis=-1, keepdims=True)
    zc = z - mu
    var = jnp.mean(zc * zc, axis=-1, keepdims=True)
    return zc * lax.rsqrt(var + LN_EPS) * g + b


def _first_lane_where(cond, lane):
    return jnp.min(jnp.where(cond, lane, LANES), axis=-1, keepdims=True)


def _mix_kernel(ona_ref, omla_ref, x_ref, wo_ref, g_ref, b_ref, wr_ref, br_ref,
                x1_ref, ri_ref, gate_ref, cnt_ref, carry_ref):
    step = pl.program_id(0)

    @pl.when(step == 0)
    def _():
        carry_ref[...] = jnp.zeros_like(carry_ref)

    ts = MIX_SUB
    subs = [slice(j * ts, (j + 1) * ts) for j in range(x_ref.shape[0] // ts)]
    lane = lax.broadcasted_iota(jnp.int32, (ts, LANES), 1)
    g_mask = (lane >= N_EXPERTS) & (lane < N_EXPERTS + N_GROUPS)
    r_i = lax.broadcasted_iota(jnp.int32, (ts, ts), 0)
    c_i = lax.broadcasted_iota(jnp.int32, (ts, ts), 1)
    tri = jnp.where(c_i < r_i, 1.0, 0.0).astype(BF16)

    def project(rows):
        return (_dot(ona_ref[rows, :], wo_ref[0:NA_WIDTH, :])
                + _dot(omla_ref[rows, :], wo_ref[NA_WIDTH:, :]))

    mix = project(subs[0])
    for j, rows in enumerate(subs):
        nxt = project(subs[j + 1]) if j + 1 < len(subs) else None
        _norm_route(rows, mix, lane, g_mask, tri, x_ref, g_ref, b_ref, wr_ref, br_ref,
                    x1_ref, ri_ref, gate_ref, carry_ref)
        mix = nxt
    cnt_ref[...] = carry_ref[...]


def _norm_route(rows, mix, lane, g_mask, tri, x_ref, g_ref, b_ref, wr_ref, br_ref,
                x1_ref, ri_ref, gate_ref, carry_ref):
    x1 = _layer_norm(ALPHA * x_ref[rows, :] + mix, g_ref[...], b_ref[...])
    x1_ref[rows, :] = x1

    logit = _dot(x1.astype(BF16), wr_ref[...]) + br_ref[...]
    gl = jnp.where(g_mask, logit, NEG_BIG)
    gmax = jnp.max(gl, axis=-1, keepdims=True)
    gsum = jnp.sum(jnp.where(g_mask, jnp.exp(gl - gmax), 0.0), axis=-1, keepdims=True)
    g_val = 1.0 / gsum
    g_idx = _first_lane_where(gl == gmax, lane) - N_EXPERTS

    e_mask = (lane < N_EXPERTS) & ((lane // EXPERTS_PER_GROUP) == g_idx)
    el = jnp.where(e_mask, logit, NEG_BIG)
    emax = jnp.max(el, axis=-1, keepdims=True)
    esum = jnp.sum(jnp.where(e_mask, jnp.exp(el - emax), 0.0), axis=-1, keepdims=True)
    i1 = _first_lane_where(el == emax, lane)
    el2 = jnp.where(lane == i1, NEG_BIG, el)
    emax2 = jnp.max(el2, axis=-1, keepdims=True)
    i2 = _first_lane_where(el2 == emax2, lane)
    v1 = 1.0 / esum
    v2 = jnp.exp(emax2 - emax) / esum
    vsum = v1 + v2
    w1 = g_val * (v1 / vsum)
    w2 = g_val * (v2 / vsum)

    oh1 = lane == i1
    oh2 = lane == i2
    oh = jnp.where(oh1 | oh2, 1.0, 0.0)
    before = _dot(tri, oh.astype(BF16)) + carry_ref[...]
    rank1 = jnp.sum(jnp.where(oh1, before, 0.0), axis=-1, keepdims=True).astype(jnp.int32)
    rank2 = jnp.sum(jnp.where(oh2, before, 0.0), axis=-1, keepdims=True).astype(jnp.int32)
    carry_ref[...] = carry_ref[...] + jnp.sum(oh, axis=0, keepdims=True)

    ri = jnp.where(lane == 0, i1, jnp.where(lane == 1, i2,
                   jnp.where(lane == 2, rank1, jnp.where(lane == 3, rank2, 0))))
    ri_ref[:, rows] = ri.T[0:SUBLANES, :]
    gate_ref[rows, :] = jnp.where(lane == 0, w1, jnp.where(lane == 1, w2, 0.0))


def _mix_out(o_na, o_mla, x, wo, g, b, wr, br):
    N, D = x.shape
    tm = MIX_TM
    row = lambda i: (i, 0)
    return pl.pallas_call(
        _mix_kernel,
        out_shape=(
            jax.ShapeDtypeStruct((N, D), F32),
            jax.ShapeDtypeStruct((SUBLANES, N), jnp.int32),
            jax.ShapeDtypeStruct((N, LANES), F32),
            jax.ShapeDtypeStruct((1, LANES), F32),
        ),
        grid=(N // tm,),
        in_specs=[
            pl.BlockSpec((tm, NA_WIDTH), row),
            pl.BlockSpec((tm, NA_WIDTH), row),
            pl.BlockSpec((tm, D), row),
            _resident(wo.shape), _resident(g.shape), _resident(b.shape),
            _resident(wr.shape), _resident(br.shape),
        ],
        out_specs=(
            pl.BlockSpec((tm, D), row),
            pl.BlockSpec((SUBLANES, tm), lambda i: (0, i)),
            pl.BlockSpec((tm, LANES), row),
            pl.BlockSpec((1, LANES), lambda i: (0, 0)),
        ),
        scratch_shapes=[pltpu.VMEM((1, LANES), F32)],
        compiler_params=_params("arbitrary"),
        name="mix_out",
    )(o_na, o_mla, x, wo, g, b, wr, br)


def _pos_kernel(start_ref, ri_ref, pos_ref):
    eid = ri_ref[0:TOP_K_EXPERT, :]
    pos = ri_ref[TOP_K_EXPERT:2 * TOP_K_EXPERT, :]
    for e in range(N_EXPERTS):
        pos = pos + jnp.where(eid == e, start_ref[e], 0)
    pos_ref[...] = pos


def _positions(pad_starts, ri):
    n = ri.shape[1]
    return pl.pallas_call(
        _pos_kernel,
        out_shape=jax.ShapeDtypeStruct((TOP_K_EXPERT, n), jnp.int32),
        grid_spec=pltpu.PrefetchScalarGridSpec(
            num_scalar_prefetch=1,
            grid=(1,),
            in_specs=[pl.BlockSpec(ri.shape, lambda i, s: (0, 0))],
            out_specs=pl.BlockSpec((TOP_K_EXPERT, n), lambda i, s: (0, 0)),
        ),
        compiler_params=_params("arbitrary"),
        name="positions",
    )(pad_starts, ri)


def _row_copy(src, src_row, dst, dst_row, sem):
    return pltpu.make_async_copy(src.at[pl.ds(src_row, 1), :], dst.at[pl.ds(dst_row, 1), :], sem)


_PAD_CHUNKS = tuple(1 << b for b in reversed(range(3, MOE_TB.bit_length() - 1)))


def _dispatch_kernel(pos_ref, fill_ref, x_ref, xs_ref, zero_ref, sem, zsem):
    tm = x_ref.shape[0]
    n_tok = pl.num_programs(0) * tm
    base = pl.program_id(0) * tm

    def pad_copies(fn):
        def per_expert(e, carry):
            start = fill_ref[e]
            n = fill_ref[N_EXPERTS + e]
            head = (-start) & (SUBLANES - 1)
            for r in range(SUBLANES - 1):
                @pl.when(r < head)
                def _(r=r):
                    fn(_row_copy(zero_ref, 0, xs_ref, start + r, zsem))
            off = start + head
            rem = n - head
            for c in _PAD_CHUNKS:
                hit = (rem & c) != 0

                @pl.when(hit)
                def _(off=off, c=c):
                    fn(pltpu.make_async_copy(zero_ref.at[pl.ds(0, c), :],
                                             xs_ref.at[pl.ds(pl.multiple_of(off, SUBLANES), c), :],
                                             zsem))
                off = off + jnp.where(hit, c, 0)
            return carry
        lax.fori_loop(0, N_EXPERTS, per_expert, 0)

        zrows = zero_ref.shape[0]

        def tail(t, carry):
            row = pl.multiple_of(fill_ref[2 * N_EXPERTS] + t * zrows, zrows)
            fn(pltpu.make_async_copy(zero_ref, xs_ref.at[pl.ds(row, zrows), :], zsem))
            return carry
        lax.fori_loop(0, (xs_ref.shape[0] - fill_ref[2 * N_EXPERTS]) // zrows, tail, 0)

    @pl.when(pl.program_id(0) == 0)
    def _():
        zero_ref[...] = jnp.zeros_like(zero_ref)
        pad_copies(lambda d: d.start())

    def issue(j, carry):
        for kk in range(TOP_K_EXPERT):
            _row_copy(x_ref, j, xs_ref, pos_ref[kk * n_tok + base + j], sem).start(priority=kk)
        return carry

    lax.fori_loop(0, tm, issue, 0, unroll=8)

    @pl.when(pl.program_id(0) == 0)
    def _():
        pad_copies(lambda d: d.wait())

    for kk in range(TOP_K_EXPERT):
        pltpu.make_async_copy(x_ref, xs_ref.at[pl.ds(0, tm), :], sem).wait()


def _dispatch(pos, fill, x1, n_rows):
    N, C = x1.shape
    tm = DISP_TM
    return pl.pallas_call(
        _dispatch_kernel,
        out_shape=jax.ShapeDtypeStruct((n_rows, C), x1.dtype),
        grid_spec=pltpu.PrefetchScalarGridSpec(
            num_scalar_prefetch=2,
            grid=(N // tm,),
            in_specs=[pl.BlockSpec((tm, C), lambda i, pos, fill: (i, 0))],
            out_specs=pl.BlockSpec(memory_space=pl.ANY),
            scratch_shapes=[pltpu.VMEM((_PAD_CHUNKS[0], C), x1.dtype),
                            pltpu.SemaphoreType.DMA, pltpu.SemaphoreType.DMA],
        ),
        compiler_params=_params("arbitrary"),
        name="dispatch",
    )(pos, fill, x1)


def _experts_kernel(be_ref, nxt_ref, nb_ref, xs_ref, wg_hbm, wu_hbm, wd_hbm, ys_ref,
                    wg_f, wu_f, wd_f, wg_s, wu_s, wd_s, sem):
    i = pl.program_id(0)
    active = i < nb_ref[0]
    e = be_ref[i]
    fresh = active & ((i == 0) | (e != be_ref[jnp.maximum(i - 1, 0)]))

    def fetch(ex):
        return (pltpu.make_async_copy(wg_hbm.at[ex], wg_f, sem.at[0]),
                pltpu.make_async_copy(wu_hbm.at[ex], wu_f, sem.at[1]),
                pltpu.make_async_copy(wd_hbm.at[ex], wd_f, sem.at[2]))

    @pl.when(i == 0)
    def _():
        for d in fetch(e):
            d.start()

    @pl.when(fresh)
    def _():
        for d in fetch(e):
            d.wait()
        wg_s[...] = wg_f[...].astype(BF16)
        wu_s[...] = wu_f[...].astype(BF16)
        wd_s[...] = wd_f[...].astype(BF16)
        nx = nxt_ref[i]

        @pl.when(nx >= 0)
        def _():
            for d in fetch(nx):
                d.start()

    @pl.when(active)
    def _():
        xb = xs_ref[...].astype(BF16)
        gp = _dot(xb, wg_s[...])
        up = _dot(xb, wu_s[...])
        hdn = (gp * jax.nn.sigmoid(gp) * up).astype(BF16)
        ys_ref[...] = _dot(hdn, wd_s[...])

    @pl.when(jnp.logical_not(active))
    def _():
        ys_ref[...] = jnp.zeros_like(ys_ref)


def _experts(block_e, next_e, n_used, xs, w_gate, w_up, w_down):
    P, C = xs.shape
    E, D, F = w_gate.shape
    tb = MOE_TB
    n_blocks = P // tb
    hbm = pl.BlockSpec(memory_space=pl.ANY)
    return pl.pallas_call(
        _experts_kernel,
        out_shape=jax.ShapeDtypeStruct((P, C), F32),
        grid_spec=pltpu.PrefetchScalarGridSpec(
            num_scalar_prefetch=3,
            grid=(n_blocks,),
            in_specs=[
                pl.BlockSpec((tb, C), lambda i, be, nx, nb: (jnp.minimum(i, nb[0] - 1), 0)),
                hbm, hbm, hbm,
            ],
            out_specs=pl.BlockSpec((tb, C), lambda i, be, nx, nb: (i, 0)),
            scratch_shapes=[pltpu.VMEM((D, F), F32), pltpu.VMEM((D, F), F32), pltpu.VMEM((F, D), F32),
                            pltpu.VMEM((D, F), BF16), pltpu.VMEM((D, F), BF16),
                            pltpu.VMEM((F, D), BF16), pltpu.SemaphoreType.DMA((3,))],
        ),
        compiler_params=_params("arbitrary"),
        name="experts",
    )(block_e, next_e, n_used, xs, w_gate, w_up, w_down)


def _final_kernel(pos_ref, x1_ref, gate_ref, ys_ref, p_ref, g_ref, b_ref, wpg_ref, wpe_ref,
                  gp_ref, o_ref, buf, sem):
    tm = x1_ref.shape[0]
    i = pl.program_id(0)
    last = pl.num_programs(0) - 1
    slot = i % 2

    def issue(tile, s, unroll):
        n_tok = pl.num_programs(0) * tm
        base = tile * tm

        def body(j, carry):
            for kk in range(TOP_K_EXPERT):
                _row_copy(ys_ref, pos_ref[kk * n_tok + base + j],
                          buf.at[s, kk], j, sem.at[s]).start()
            return carry
        lax.fori_loop(0, tm, body, 0, unroll=unroll)

    def drain(s):
        for kk in range(TOP_K_EXPERT):
            pltpu.make_async_copy(ys_ref.at[pl.ds(0, tm), :], buf.at[s, kk], sem.at[s]).wait()

    @pl.when(i == 0)
    def _():
        issue(0, 0, 8)

    drain(slot)
    issue(jnp.minimum(i + 1, last), 1 - slot, True)

    gates = gate_ref[...]
    g1 = gates[:, 0:1]
    g2 = gates[:, 1:2]
    ffn = buf[slot, 0] * g1 + buf[slot, 1] * g2
    x2 = _layer_norm(ALPHA * x1_ref[...] + ffn, g_ref[...], b_ref[...])

    gate = jax.nn.sigmoid(_dot(x2.astype(BF16), wpg_ref[...]))
    e = _dot(p_ref[...].astype(BF16), wpe_ref[...])
    t = e * gate
    ple = t * lax.rsqrt(jnp.mean(t * t, axis=-1, keepdims=True) + RMS_EPS) * gp_ref[...]
    o_ref[...] = x2 + ple

    @pl.when(i == last)
    def _():
        drain(1 - slot)


def _final(pos, x1, gates, ys, p, g, b, wpg, wpe, gp):
    N, D = x1.shape
    tm = FIN_TM
    row = lambda i, pos: (i, 0)
    return pl.pallas_call(
        _final_kernel,
        out_shape=jax.ShapeDtypeStruct((N, D), F32),
        grid_spec=pltpu.PrefetchScalarGridSpec(
            num_scalar_prefetch=1,
            grid=(N // tm,),
            in_specs=[
                pl.BlockSpec((tm, D), row),
                pl.BlockSpec((tm, LANES), row),
                pl.BlockSpec(memory_space=pl.ANY),
                pl.BlockSpec((tm, PLE_DIM), row),
                _resident(g.shape), _resident(b.shape), _resident(wpg.shape),
                _resident(wpe.shape), _resident(gp.shape),
            ],
            out_specs=pl.BlockSpec((tm, D), row),
            scratch_shapes=[pltpu.VMEM((2, TOP_K_EXPERT, tm, D), F32),
                            pltpu.SemaphoreType.DMA((2,))],
        ),
        compiler_params=_params("arbitrary"),
        name="final",
    )(pos, x1, gates, ys, p, g, b, wpg, wpe, gp)


def _rope_table(positions):
    inv_freq = 1.0 / (ROPE_THETA ** (jnp.arange(0, MLA_ROPE_DIM, 2, dtype=F32) / MLA_ROPE_DIM))
    half = MLA_ROPE_DIM // 2
    phase = jnp.concatenate([jnp.zeros((2 * half,), F32), jnp.full((2 * half,), np.pi / 2, F32)])
    return jnp.cos(positions.astype(F32)[..., None] * jnp.tile(inv_freq, 4) - phase)


def _layer(x, p_i, rope_t, w_in, rpb, q_norm_g, kv_norm_g, w_uq, w_uk, w_uv, w_o, ln1_g, ln1_b,
           w_group, b_group, w_router, b_router, w_gate, w_up, w_down, ln2_g, ln2_b,
           w_ple, w_ple_gate, ple_norm_g):
    B, S, D = x.shape
    N = B * S
    s3 = 3 * NA_WIDTH
    half = MLA_ROPE_DIM // 2

    wqk = w_in[:, :2 * NA_WIDTH].astype(BF16)
    wc = w_in[:, s3:].astype(BF16)
    wvt = w_in[:, 2 * NA_WIDTH:s3].T.astype(BF16)
    uq = w_uq.reshape(Q_LORA_RANK, MLA_HEADS, MLA_QK_DIM)
    nope, x1c, x2c = (uq[..., :MLA_NOPE_DIM], uq[..., MLA_NOPE_DIM:MLA_NOPE_DIM + half],
                      uq[..., MLA_NOPE_DIM + half:])
    wuq = jnp.concatenate([nope, x1c, x2c, x2c, x1c], axis=-1).reshape(Q_LORA_RANK, -1).astype(BF16)
    wuk = w_uk.astype(BF16)
    wuvt = w_uv.T.astype(BF16)
    row = lambda v: v.reshape(1, -1).astype(F32)

    q_na, k_na, v_nat, q_m, k_m, v_mt = _proj(x, rope_t, wqk, wvt, wc, wuq, wuk, wuvt,
                                               row(q_norm_g), row(kv_norm_g))
    o_na = _na(q_na, k_na, v_nat, _na_bias_table(rpb))
    o_mla = _mla(q_m, k_m, v_mt)

    pad = LANES - N_EXPERTS - N_GROUPS
    wr = jnp.concatenate([w_router, w_group, jnp.zeros((D, pad), F32)], axis=1).astype(BF16)
    br = jnp.concatenate([b_router.reshape(-1), b_group, jnp.zeros((pad,), F32)]).reshape(1, -1)
    x1, ri, gates, cnt = _mix_out(o_na.reshape(N, -1), o_mla.reshape(N, -1), x.reshape(N, D),
                                       w_o.astype(BF16), row(ln1_g), row(ln1_b), wr, br)

    tb = MOE_TB
    counts = cnt[0, :N_EXPERTS].astype(jnp.int32)
    padded = ((counts + tb - 1) // tb) * tb
    pad_ends = jnp.cumsum(padded)
    pad_starts = pad_ends - padded
    pos = _positions(pad_starts.astype(jnp.int32), ri).reshape(-1)
    fill = jnp.concatenate([pad_starts + counts, padded - counts, pad_ends[-1:]]).astype(jnp.int32)
    n_blocks = (N * TOP_K_EXPERT + N_EXPERTS * (tb - 1) + tb - 1) // tb
    blk_row = jnp.arange(n_blocks, dtype=jnp.int32) * tb
    block_e = jnp.minimum(jnp.sum(pad_ends[None, :] <= blk_row[:, None], axis=1),
                          N_EXPERTS - 1).astype(jnp.int32)
    n_used = (pad_ends[-1:] // tb).astype(jnp.int32)
    nxt_blk = pad_ends[block_e] // tb
    next_e = jnp.where(nxt_blk < n_used[0], block_e[jnp.minimum(nxt_blk, n_blocks - 1)],
                       -1).astype(jnp.int32)

    xs = _dispatch(pos, fill, x1, n_blocks * tb)
    ys = _experts(block_e, next_e, n_used, xs, w_gate, w_up, w_down)
    out = _final(pos, x1, gates, ys, p_i.reshape(N, -1), row(ln2_g), row(ln2_b),
                 w_ple_gate.astype(BF16), w_ple.astype(BF16), row(ple_norm_g))
    return out.reshape(B, S, D)


def kernel(x, p, positions, w_in, rpb, q_norm_g, kv_norm_g, w_uq, w_uk, w_uv, w_o, ln1_g, ln1_b,
           w_group, b_group, w_router, b_router, w_gate, w_up, w_down, ln2_g, ln2_b,
           w_ple, w_ple_gate, ple_norm_g):
    rope_t = _rope_table(positions)
    for i in range(DEPTH):
        x = _layer(x, p[i], rope_t, w_in[i], rpb[i], q_norm_g[i], kv_norm_g[i], w_uq[i], w_uk[i],
                   w_uv[i], w_o[i], ln1_g[i], ln1_b[i], w_group[i], b_group[i], w_router[i],
                   b_router[i], w_gate[i], w_up[i], w_down[i], ln2_g[i], ln2_b[i],
                   w_ple[i], w_ple_gate[i], ple_norm_g[i])
    return x
```

```python
import numpy as np
import jax
import jax.numpy as jnp
from jax import lax
from jax.experimental import pallas as pl
from jax.experimental.pallas import tpu as pltpu

D_MODEL = 2048
DEPTH = 1
GRID_W = 64
PLE_DIM = 256
NA_HEADS = 8
NA_HEAD_DIM = 128
NA_WIN_H = 8
NA_WIN_W = 16
NA_WIDTH = NA_HEADS * NA_HEAD_DIM
MLA_HEADS = 8
MLA_NOPE_DIM = 128
MLA_ROPE_DIM = 64
MLA_V_DIM = 128
MLA_QK_DIM = MLA_NOPE_DIM + MLA_ROPE_DIM
Q_LORA_RANK = 512
KV_LORA_RANK = 512
ROPE_THETA = 10000.0
N_GROUPS = 8
EXPERTS_PER_GROUP = 8
N_EXPERTS = N_GROUPS * EXPERTS_PER_GROUP
TOP_K_EXPERT = 2
D_EXPERT = 512
ALPHA = (2 * DEPTH) ** 0.25
LN_EPS = 1e-5
RMS_EPS = 1e-6

LANES = 128
SUBLANES = 8
VMEM_LIMIT_BYTES = 56 * 1024 * 1024

PROJ_TM = 256
PROJ_WCHUNK = 512
NA_ROWS = 4
NA_QB = NA_ROWS * GRID_W
NA_WIN_ROWS = NA_ROWS + NA_WIN_H
NA_KB = NA_WIN_ROWS * GRID_W
MLA_TQ = 256
MLA_KC = 1024
MLA_TILES = 8
MIX_TM = 512
MIX_SUB = 256
MOE_TB = 256
DISP_TM = 256
FIN_TM = 256
NEG_BIG = -1e30
LOG2E = 1.4426950408889634

BF16 = jnp.bfloat16
F32 = jnp.float32

_NT = (((1,), (1,)), ((), ()))


def _dot(a, b):
    return jnp.dot(a, b, preferred_element_type=F32)


def _dot_nt(a, b):
    return lax.dot_general(a, b, _NT, preferred_element_type=F32)


def _params(*sem):
    return pltpu.CompilerParams(dimension_semantics=sem, vmem_limit_bytes=VMEM_LIMIT_BYTES)


def _resident(shape):
    nd = len(shape)
    return pl.BlockSpec(shape, lambda *_: (0,) * nd, pipeline_mode=pl.Buffered(1))


def _proj_kernel(x_ref, t_ref, win_hbm, wvt_ref, wkr_ref, wuq_ref, wuk_ref, wuvt_ref, gq_ref, gkv_ref,
                 qna_ref, kna_ref, vnat_ref, qm_ref, km_ref, vmt_ref, wqk_ref, wc_ref, stage, sem):
    @pl.when((pl.program_id(0) == 0) & (pl.program_id(1) == 0))
    def _():
        cw = stage.shape[2]
        chunks = ([(c * cw, wqk_ref, c * cw) for c in range(2 * NA_WIDTH // cw)]
                  + [(3 * NA_WIDTH + c * cw, wc_ref, c * cw)
                     for c in range((Q_LORA_RANK + KV_LORA_RANK) // cw)])

        def copy(k):
            return pltpu.make_async_copy(win_hbm.at[:, pl.ds(chunks[k][0], cw)],
                                         stage.at[k % 2], sem.at[k % 2])

        copy(0).start()
        for k, (_, dst, d0) in enumerate(chunks):
            if k + 1 < len(chunks):
                copy(k + 1).start()
            copy(k).wait()
            dst[:, d0:d0 + cw] = stage[k % 2].astype(BF16)

    xb = x_ref[0].astype(BF16)
    na_scale = NA_HEAD_DIM ** -0.5 * LOG2E
    mla_scale = MLA_QK_DIM ** -0.5 * LOG2E
    q = _dot(xb, wqk_ref[:, 0:NA_WIDTH]) * na_scale
    k = _dot(xb, wqk_ref[:, NA_WIDTH:2 * NA_WIDTH])
    vt = _dot_nt(wvt_ref[...], xb)
    for h in range(NA_HEADS):
        sl = slice(h * NA_HEAD_DIM, (h + 1) * NA_HEAD_DIM)
        qna_ref[0, h] = q[:, sl].astype(BF16)
        kna_ref[0, h] = k[:, sl].astype(BF16)
        vnat_ref[0, h] = vt[sl, :].astype(BF16)

    half = MLA_ROPE_DIM // 2
    cq = _dot(xb, wc_ref[:, 0:Q_LORA_RANK])
    ckv = _dot(xb, wc_ref[:, Q_LORA_RANK:Q_LORA_RANK + KV_LORA_RANK])
    kr = _dot(xb, wkr_ref[...])
    kr4 = jnp.concatenate([kr, kr[:, half:], kr[:, :half]], axis=1)

    def rms(c, g):
        return c * lax.rsqrt(jnp.mean(c * c, axis=-1, keepdims=True) + RMS_EPS) * g

    cqn = rms(cq, gq_ref[...]).astype(BF16)
    ckvn = rms(ckv, gkv_ref[...]).astype(BF16)

    t = t_ref[0]
    e = kr4 * t
    lane = lax.broadcasted_iota(jnp.int32, (1, LANES), 1)
    sign = jnp.where((lane // 32) % 2 == 0, -1.0, 1.0).astype(F32)
    kpe = (e + sign * pltpu.roll(e, 64, 1)).astype(BF16)

    qf = _dot(cqn, wuq_ref[...])
    kn = _dot(ckvn, wuk_ref[...])
    vmt = _dot_nt(wuvt_ref[...], ckvn)
    ts = t * mla_scale
    for h in range(MLA_HEADS):
        qm_ref[0, h, :, 0:128] = (qf[:, h * 256:h * 256 + 128] * mla_scale).astype(BF16)
        qm_ref[0, h, :, 128:256] = (qf[:, h * 256 + 128:(h + 1) * 256] * ts).astype(BF16)
        km_ref[0, h, :, 0:128] = kn[:, h * 128:(h + 1) * 128].astype(BF16)
        km_ref[0, h, :, 128:256] = kpe
        vmt_ref[0, h] = vmt[h * 128:(h + 1) * 128, :].astype(BF16)


def _proj(x, rope_t, w_in, wvt, wkr, wuq, wuk, wuvt, gq, gkv):
    B, S, D = x.shape
    tm = PROJ_TM
    hm = lambda b, i: (b, 0, i, 0)
    hmt = lambda b, i: (b, 0, 0, i)
    out_shape = (
        jax.ShapeDtypeStruct((B, NA_HEADS, S, NA_HEAD_DIM), BF16),
        jax.ShapeDtypeStruct((B, NA_HEADS, S, NA_HEAD_DIM), BF16),
        jax.ShapeDtypeStruct((B, NA_HEADS, NA_HEAD_DIM, S), BF16),
        jax.ShapeDtypeStruct((B, MLA_HEADS, S, 256), BF16),
        jax.ShapeDtypeStruct((B, MLA_HEADS, S, 256), BF16),
        jax.ShapeDtypeStruct((B, MLA_HEADS, MLA_V_DIM, S), BF16),
    )
    return pl.pallas_call(
        _proj_kernel,
        out_shape=out_shape,
        grid=(B, S // tm),
        in_specs=[
            pl.BlockSpec((1, tm, D), lambda b, i: (b, i, 0)),
            pl.BlockSpec((1, tm, LANES), lambda b, i: (b, i, 0)),
            pl.BlockSpec(memory_space=pl.ANY), _resident(wvt.shape), _resident(wkr.shape),
            _resident(wuq.shape),
            _resident(wuk.shape), _resident(wuvt.shape), _resident(gq.shape), _resident(gkv.shape),
        ],
        out_specs=(
            pl.BlockSpec((1, NA_HEADS, tm, NA_HEAD_DIM), hm),
            pl.BlockSpec((1, NA_HEADS, tm, NA_HEAD_DIM), hm),
            pl.BlockSpec((1, NA_HEADS, NA_HEAD_DIM, tm), hmt),
            pl.BlockSpec((1, MLA_HEADS, tm, 256), hm),
            pl.BlockSpec((1, MLA_HEADS, tm, 256), hm),
            pl.BlockSpec((1, MLA_HEADS, MLA_V_DIM, tm), hmt),
        ),
        scratch_shapes=[pltpu.VMEM((D, 2 * NA_WIDTH), BF16),
                        pltpu.VMEM((D, Q_LORA_RANK + KV_LORA_RANK), BF16),
                        pltpu.VMEM((2, D, PROJ_WCHUNK), F32), pltpu.SemaphoreType.DMA((2,))],
        compiler_params=_params("arbitrary", "arbitrary"),
        name="proj",
    )(x, rope_t, w_in, wvt, wkr, wuq, wuk, wuvt, gq, gkv)


def _na_bias_table(rpb):
    rows = GRID_W
    n_blocks = rows // NA_ROWS
    n_ro, n_co = 2 * NA_WIN_H - 1, 2 * NA_WIN_W - 1
    kr, qr = np.arange(NA_WIN_ROWS), np.arange(NA_ROWS)
    kc, qc = np.arange(GRID_W), np.arange(GRID_W)
    c0 = np.clip(qc - NA_WIN_W // 2, 0, GRID_W - NA_WIN_W)
    col_ok = (kc[:, None] >= c0[None, :]) & (kc[:, None] < c0[None, :] + NA_WIN_W)
    col_off = np.clip(kc[:, None] - qc[None, :] + (NA_WIN_W - 1), 0, n_co - 1)
    oh_c = np.zeros((n_co, GRID_W, GRID_W), np.float32)
    oh_c[col_off, kc[:, None], qc[None, :]] = 1.0
    oh_r = np.zeros((3, n_ro, NA_WIN_ROWS, NA_ROWS), np.float32)
    row_ok = np.zeros((3, NA_WIN_ROWS, NA_ROWS), bool)
    for c, rb in enumerate((0, 1, n_blocks - 1)):
        w0 = int(np.clip(NA_ROWS * rb - NA_WIN_H // 2, 0, rows - NA_WIN_ROWS))
        r = NA_ROWS * rb + qr
        r0 = np.clip(r - NA_WIN_H // 2, 0, rows - NA_WIN_H)
        krow = w0 + kr
        row_ok[c] = (krow[:, None] >= r0[None, :]) & (krow[:, None] < r0[None, :] + NA_WIN_H)
        row_off = np.clip(krow[:, None] - r[None, :] + (NA_WIN_H - 1), 0, n_ro - 1)
        oh_r[c, row_off, kr[:, None], qr[None, :]] = 1.0
    sel = np.zeros((NA_ROWS, n_co, GRID_W, NA_ROWS, GRID_W), np.float32)
    for r in range(NA_ROWS):
        sel[r, :, :, r, :] = oh_c
    sel = sel.reshape(NA_ROWS * n_co, GRID_W, NA_QB)
    ok = row_ok[:, :, None, :, None] & col_ok[None, None, :, None, :]
    mask = np.where(ok, 0.0, NEG_BIG).astype(np.float32).reshape(3, NA_WIN_ROWS, GRID_W, NA_QB)
    hi = lax.Precision.HIGHEST
    a = jnp.einsum('hrc,zrkq->hzkqc', rpb.astype(F32) * LOG2E, oh_r, precision=hi)
    a = a.reshape(rpb.shape[0], 3, NA_WIN_ROWS, NA_ROWS * n_co)
    tab = jnp.einsum('hzkj,jxn->hzkxn', a, sel, precision=hi) + mask[None]
    return tab.reshape(rpb.shape[0], 3, NA_KB, NA_QB)


def _na_kernel(q_ref, k_ref, vt_ref, tab_ref, o_ref, st_ref, m_ref):
    n_blocks = q_ref.shape[2] // NA_QB
    rows = n_blocks * NA_ROWS

    def window(rb):
        w0 = jnp.clip(NA_ROWS * rb - NA_WIN_H // 2, 0, rows - NA_WIN_ROWS)
        return pl.multiple_of(w0 * GRID_W, NA_QB)

    def scores(rb, slot):
        tok0 = window(rb)
        q0 = pl.multiple_of(rb * NA_QB, NA_QB)
        cls = jnp.where(rb == 0, 0, jnp.where(rb == n_blocks - 1, 2, 1))
        st = _dot_nt(k_ref[0, 0, pl.ds(tok0, NA_KB), :],
                     q_ref[0, 0, pl.ds(q0, NA_QB), :]) + tab_ref[0, cls]
        st_ref[slot] = st
        m_ref[slot] = jnp.max(st, axis=0, keepdims=True)

    def output(rb, slot):
        tok0 = window(rb)
        q0 = pl.multiple_of(rb * NA_QB, NA_QB)
        p = jnp.exp2(st_ref[slot] - m_ref[slot])
        l = jnp.sum(p, axis=0, keepdims=True)
        ot = _dot(vt_ref[0, 0, :, pl.ds(tok0, NA_KB)], p.astype(BF16)) / l
        o_ref[0, pl.ds(q0, NA_QB), :] = ot.T.astype(o_ref.dtype)

    scores(0, 0)

    def body(t, carry):
        rb = 2 * t + 1
        scores(rb, 1)
        output(rb - 1, 0)
        scores(rb + 1, 0)
        output(rb, 1)
        return carry

    lax.fori_loop(0, n_blocks // 2 - 1, body, 0)
    scores(n_blocks - 1, 1)
    output(n_blocks - 2, 0)
    output(n_blocks - 1, 1)


def _na(q, k, vt, tab):
    B, H, S, Dh = q.shape
    return pl.pallas_call(
        _na_kernel,
        out_shape=jax.ShapeDtypeStruct((B, S, H * Dh), BF16),
        grid=(H, B),
        in_specs=[
            pl.BlockSpec((1, 1, S, Dh), lambda h, b: (b, h, 0, 0)),
            pl.BlockSpec((1, 1, S, Dh), lambda h, b: (b, h, 0, 0)),
            pl.BlockSpec((1, 1, Dh, S), lambda h, b: (b, h, 0, 0)),
            pl.BlockSpec((1, 3, NA_KB, NA_QB), lambda h, b: (h, 0, 0, 0)),
        ],
        out_specs=pl.BlockSpec((1, S, Dh), lambda h, b: (b, 0, h)),
        scratch_shapes=[pltpu.VMEM((2, NA_KB, NA_QB), F32), pltpu.VMEM((2, 1, NA_QB), F32)],
        compiler_params=_params("arbitrary", "arbitrary"),
        name="na",
    )(q, k, vt, tab)


def _mla_kernel(q_ref, k_ref, vt_ref, o_ref, st_ref):
    n_chunks = k_ref.shape[2] // MLA_KC
    n_tiles = q_ref.shape[2] // MLA_TQ

    def pass1(j, slot):
        q = q_ref[0, 0, j * MLA_TQ:(j + 1) * MLA_TQ, :]
        m = None
        for c in range(n_chunks):
            ks = slice(c * MLA_KC, (c + 1) * MLA_KC)
            st = _dot_nt(k_ref[0, 0, ks, :], q)
            st_ref[slot, ks, :] = st
            mc = jnp.max(st, axis=0, keepdims=True)
            m = mc if c == 0 else jnp.maximum(m, mc)
        return m

    def pass2(j, slot, m):
        l = acc = None
        for c in range(n_chunks):
            ks = slice(c * MLA_KC, (c + 1) * MLA_KC)
            p = jnp.exp2(st_ref[slot, ks, :] - m)
            ps = jnp.sum(p, axis=0, keepdims=True)
            pv = _dot(vt_ref[0, 0, :, ks], p.astype(BF16))
            l = ps if c == 0 else l + ps
            acc = pv if c == 0 else acc + pv
        o_ref[0, j * MLA_TQ:(j + 1) * MLA_TQ, :] = (acc / l).T.astype(o_ref.dtype)

    m_prev = pass1(0, 0)
    for j in range(1, n_tiles):
        m_cur = pass1(j, j % 2)
        pass2(j - 1, (j - 1) % 2, m_prev)
        m_prev = m_cur
    pass2(n_tiles - 1, (n_tiles - 1) % 2, m_prev)


def _mla(q, k, vt):
    B, H, S, Dq = q.shape
    Dv = MLA_V_DIM
    tq = MLA_TQ * MLA_TILES
    return pl.pallas_call(
        _mla_kernel,
        out_shape=jax.ShapeDtypeStruct((B, S, H * Dv), BF16),
        grid=(B, H, S // tq),
        in_specs=[
            pl.BlockSpec((1, 1, tq, Dq), lambda b, h, i: (b, h, i, 0)),
            pl.BlockSpec((1, 1, S, Dq), lambda b, h, i: (b, h, 0, 0)),
            pl.BlockSpec((1, 1, Dv, S), lambda b, h, i: (b, h, 0, 0)),
        ],
        out_specs=pl.BlockSpec((1, tq, Dv), lambda b, h, i: (b, i, h)),
        scratch_shapes=[pltpu.VMEM((2, S, MLA_TQ), F32)],
        compiler_params=_params("arbitrary", "arbitrary", "arbitrary"),
        name="mla",
    )(q, k, vt)


def _layer_norm(z, g, b):
    mu = jnp.mean(z, axis=-1, keepdims=True)
    zc = z - mu
    var = jnp.mean(zc * zc, axis=-1, keepdims=True)
    return zc * lax.rsqrt(var + LN_EPS) * g + b


def _first_lane_where(cond, lane):
    return jnp.min(jnp.where(cond, lane, LANES), axis=-1, keepdims=True)


def _mix_kernel(ona_ref, omla_ref, x_ref, wo_ref, g_ref, b_ref, wr_ref, br_ref,
                x1_ref, ri_ref, gate_ref, cnt_ref, carry_ref):
    step = pl.program_id(0)

    @pl.when(step == 0)
    def _():
        carry_ref[...] = jnp.zeros_like(carry_ref)

    ts = MIX_SUB
    subs = [slice(j * ts, (j + 1) * ts) for j in range(x_ref.shape[0] // ts)]
    lane = lax.broadcasted_iota(jnp.int32, (ts, LANES), 1)
    g_mask = (lane >= N_EXPERTS) & (lane < N_EXPERTS + N_GROUPS)
    r_i = lax.broadcasted_iota(jnp.int32, (ts, ts), 0)
    c_i = lax.broadcasted_iota(jnp.int32, (ts, ts), 1)
    tri = jnp.where(c_i < r_i, 1.0, 0.0).astype(BF16)

    def project(rows):
        return (_dot(ona_ref[rows, :], wo_ref[0:NA_WIDTH, :])
                + _dot(omla_ref[rows, :], wo_ref[NA_WIDTH:, :]))

    mix = project(subs[0])
    for j, rows in enumerate(subs):
        nxt = project(subs[j + 1]) if j + 1 < len(subs) else None
        _norm_route(rows, mix, lane, g_mask, tri, x_ref, g_ref, b_ref, wr_ref, br_ref,
                    x1_ref, ri_ref, gate_ref, carry_ref)
        mix = nxt
    cnt_ref[...] = carry_ref[...]


def _norm_route(rows, mix, lane, g_mask, tri, x_ref, g_ref, b_ref, wr_ref, br_ref,
                x1_ref, ri_ref, gate_ref, carry_ref):
    x1 = _layer_norm(ALPHA * x_ref[rows, :] + mix, g_ref[...], b_ref[...])
    x1_ref[rows, :] = x1

    logit = _dot(x1.astype(BF16), wr_ref[...]) + br_ref[...]
    gl = jnp.where(g_mask, logit, NEG_BIG)
    gmax = jnp.max(gl, axis=-1, keepdims=True)
    gsum = jnp.sum(jnp.where(g_mask, jnp.exp(gl - gmax), 0.0), axis=-1, keepdims=True)
    g_val = 1.0 / gsum
    g_idx = _first_lane_where(gl == gmax, lane) - N_EXPERTS

    e_mask = (lane < N_EXPERTS) & ((lane // EXPERTS_PER_GROUP) == g_idx)
    el = jnp.where(e_mask, logit, NEG_BIG)
    emax = jnp.max(el, axis=-1, keepdims=True)
    esum = jnp.sum(jnp.where(e_mask, jnp.exp(el - emax), 0.0), axis=-1, keepdims=True)
    i1 = _first_lane_where(el == emax, lane)
    el2 = jnp.where(lane == i1, NEG_BIG, el)
    emax2 = jnp.max(el2, axis=-1, keepdims=True)
    i2 = _first_lane_where(el2 == emax2, lane)
    v1 = 1.0 / esum
    v2 = jnp.exp(emax2 - emax) / esum
    vsum = v1 + v2
    w1 = g_val * (v1 / vsum)
    w2 = g_val * (v2 / vsum)

    oh1 = lane == i1
    oh2 = lane == i2
    oh = jnp.where(oh1 | oh2, 1.0, 0.0)
    before = _dot(tri, oh.astype(BF16)) + carry_ref[...]
    rank1 = jnp.sum(jnp.where(oh1, before, 0.0), axis=-1, keepdims=True).astype(jnp.int32)
    rank2 = jnp.sum(jnp.where(oh2, before, 0.0), axis=-1, keepdims=True).astype(jnp.int32)
    carry_ref[...] = carry_ref[...] + jnp.sum(oh, axis=0, keepdims=True)

    ri = jnp.where(lane == 0, i1, jnp.where(lane == 1, i2,
                   jnp.where(lane == 2, rank1, jnp.where(lane == 3, rank2, 0))))
    ri_ref[:, rows] = ri.T[0:SUBLANES, :]
    gate_ref[rows, :] = jnp.where(lane == 0, w1, jnp.where(lane == 1, w2, 0.0))


def _mix_out(o_na, o_mla, x, wo, g, b, wr, br):
    N, D = x.shape
    tm = MIX_TM
    row = lambda i: (i, 0)
    return pl.pallas_call(
        _mix_kernel,
        out_shape=(
            jax.ShapeDtypeStruct((N, D), F32),
            jax.ShapeDtypeStruct((SUBLANES, N), jnp.int32),
            jax.ShapeDtypeStruct((N, LANES), F32),
            jax.ShapeDtypeStruct((1, LANES), F32),
        ),
        grid=(N // tm,),
        in_specs=[
            pl.BlockSpec((tm, NA_WIDTH), row),
            pl.BlockSpec((tm, NA_WIDTH), row),
            pl.BlockSpec((tm, D), row),
            _resident(wo.shape), _resident(g.shape), _resident(b.shape),
            _resident(wr.shape), _resident(br.shape),
        ],
        out_specs=(
            pl.BlockSpec((tm, D), row),
            pl.BlockSpec((SUBLANES, tm), lambda i: (0, i)),
            pl.BlockSpec((tm, LANES), row),
            pl.BlockSpec((1, LANES), lambda i: (0, 0)),
        ),
        scratch_shapes=[pltpu.VMEM((1, LANES), F32)],
        compiler_params=_params("arbitrary"),
        name="mix_out",
    )(o_na, o_mla, x, wo, g, b, wr, br)


def _pos_kernel(start_ref, ri_ref, pos_ref):
    eid = ri_ref[0:TOP_K_EXPERT, :]
    pos = ri_ref[TOP_K_EXPERT:2 * TOP_K_EXPERT, :]
    for e in range(N_EXPERTS):
        pos = pos + jnp.where(eid == e, start_ref[e], 0)
    pos_ref[...] = pos


def _positions(pad_starts, ri):
    n = ri.shape[1]
    return pl.pallas_call(
        _pos_kernel,
        out_shape=jax.ShapeDtypeStruct((TOP_K_EXPERT, n), jnp.int32),
        grid_spec=pltpu.PrefetchScalarGridSpec(
            num_scalar_prefetch=1,
            grid=(1,),
            in_specs=[pl.BlockSpec(ri.shape, lambda i, s: (0, 0))],
            out_specs=pl.BlockSpec((TOP_K_EXPERT, n), lambda i, s: (0, 0)),
        ),
        compiler_params=_params("arbitrary"),
        name="positions",
    )(pad_starts, ri)


def _row_copy(src, src_row, dst, dst_row, sem):
    return pltpu.make_async_copy(src.at[pl.ds(src_row, 1), :], dst.at[pl.ds(dst_row, 1), :], sem)


_PAD_CHUNKS = tuple(1 << b for b in reversed(range(3, MOE_TB.bit_length() - 1)))


def _dispatch_kernel(pos_ref, fill_ref, x_ref, xs_ref, zero_ref, sem, zsem):
    tm = x_ref.shape[0]
    n_tok = pl.num_programs(0) * tm
    base = pl.program_id(0) * tm

    def pad_copies(fn):
        def per_expert(e, carry):
            start = fill_ref[e]
            n = fill_ref[N_EXPERTS + e]
            head = (-start) & (SUBLANES - 1)
            for r in range(SUBLANES - 1):
                @pl.when(r < head)
                def _(r=r):
                    fn(_row_copy(zero_ref, 0, xs_ref, start + r, zsem))
            off = start + head
            rem = n - head
            for c in _PAD_CHUNKS:
                hit = (rem & c) != 0

                @pl.when(hit)
                def _(off=off, c=c):
                    fn(pltpu.make_async_copy(zero_ref.at[pl.ds(0, c), :],
                                             xs_ref.at[pl.ds(pl.multiple_of(off, SUBLANES), c), :],
                                             zsem))
                off = off + jnp.where(hit, c, 0)
            return carry
        lax.fori_loop(0, N_EXPERTS, per_expert, 0)

        zrows = zero_ref.shape[0]

        def tail(t, carry):
            row = pl.multiple_of(fill_ref[2 * N_EXPERTS] + t * zrows, zrows)
            fn(pltpu.make_async_copy(zero_ref, xs_ref.at[pl.ds(row, zrows), :], zsem))
            return carry
        lax.fori_loop(0, (xs_ref.shape[0] - fill_ref[2 * N_EXPERTS]) // zrows, tail, 0)

    @pl.when(pl.program_id(0) == 0)
    def _():
        zero_ref[...] = jnp.zeros_like(zero_ref)
        pad_copies(lambda d: d.start())

    def issue(j, carry):
        for kk in range(TOP_K_EXPERT):
            _row_copy(x_ref, j, xs_ref, pos_ref[kk * n_tok + base + j], sem).start()
        return carry

    lax.fori_loop(0, tm, issue, 0, unroll=8)

    @pl.when(pl.program_id(0) == 0)
    def _():
        pad_copies(lambda d: d.wait())

    for kk in range(TOP_K_EXPERT):
        pltpu.make_async_copy(x_ref, xs_ref.at[pl.ds(0, tm), :], sem).wait()


def _dispatch(pos, fill, x1, n_rows):
    N, C = x1.shape
    tm = DISP_TM
    return pl.pallas_call(
        _dispatch_kernel,
        out_shape=jax.ShapeDtypeStruct((n_rows, C), x1.dtype),
        grid_spec=pltpu.PrefetchScalarGridSpec(
            num_scalar_prefetch=2,
            grid=(N // tm,),
            in_specs=[pl.BlockSpec((tm, C), lambda i, pos, fill: (i, 0))],
            out_specs=pl.BlockSpec(memory_space=pl.ANY),
            scratch_shapes=[pltpu.VMEM((_PAD_CHUNKS[0], C), x1.dtype),
                            pltpu.SemaphoreType.DMA, pltpu.SemaphoreType.DMA],
        ),
        compiler_params=_params("arbitrary"),
        name="dispatch",
    )(pos, fill, x1)


def _experts_kernel(be_ref, nxt_ref, nb_ref, xs_ref, wg_hbm, wu_hbm, wd_hbm, ys_ref,
                    wg_f, wu_f, wd_f, wg_s, wu_s, wd_s, sem):
    i = pl.program_id(0)
    active = i < nb_ref[0]
    e = be_ref[i]
    fresh = active & ((i == 0) | (e != be_ref[jnp.maximum(i - 1, 0)]))

    def fetch(ex):
        return (pltpu.make_async_copy(wg_hbm.at[ex], wg_f, sem.at[0]),
                pltpu.make_async_copy(wu_hbm.at[ex], wu_f, sem.at[1]),
                pltpu.make_async_copy(wd_hbm.at[ex], wd_f, sem.at[2]))

    @pl.when(i == 0)
    def _():
        for d in fetch(e):
            d.start()

    @pl.when(fresh)
    def _():
        for d in fetch(e):
            d.wait()
        wg_s[...] = wg_f[...].astype(BF16)
        wu_s[...] = wu_f[...].astype(BF16)
        wd_s[...] = wd_f[...].astype(BF16)
        nx = nxt_ref[i]

        @pl.when(nx >= 0)
        def _():
            for d in fetch(nx):
                d.start()

    @pl.when(active)
    def _():
        xb = xs_ref[...].astype(BF16)
        gp = _dot(xb, wg_s[...])
        up = _dot(xb, wu_s[...])
        hdn = (gp * jax.nn.sigmoid(gp) * up).astype(BF16)
        ys_ref[...] = _dot(hdn, wd_s[...])

    @pl.when(jnp.logical_not(active))
    def _():
        ys_ref[...] = jnp.zeros_like(ys_ref)


def _experts(block_e, next_e, n_used, xs, w_gate, w_up, w_down):
    P, C = xs.shape
    E, D, F = w_gate.shape
    tb = MOE_TB
    n_blocks = P // tb
    hbm = pl.BlockSpec(memory_space=pl.ANY)
    return pl.pallas_call(
        _experts_kernel,
        out_shape=jax.ShapeDtypeStruct((P, C), F32),
        grid_spec=pltpu.PrefetchScalarGridSpec(
            num_scalar_prefetch=3,
            grid=(n_blocks,),
            in_specs=[
                pl.BlockSpec((tb, C), lambda i, be, nx, nb: (jnp.minimum(i, nb[0] - 1), 0)),
                hbm, hbm, hbm,
            ],
            out_specs=pl.BlockSpec((tb, C), lambda i, be, nx, nb: (i, 0)),
            scratch_shapes=[pltpu.VMEM((D, F), F32), pltpu.VMEM((D, F), F32), pltpu.VMEM((F, D), F32),
                            pltpu.VMEM((D, F), BF16), pltpu.VMEM((D, F), BF16),
                            pltpu.VMEM((F, D), BF16), pltpu.SemaphoreType.DMA((3,))],
        ),
        compiler_params=_params("arbitrary"),
        name="experts",
    )(block_e, next_e, n_used, xs, w_gate, w_up, w_down)


def _final_kernel(pos_ref, x1_ref, gate_ref, ys_ref, p_ref, g_ref, b_ref, wpg_ref, wpe_ref,
                  gp_ref, o_ref, buf, sem):
    tm = x1_ref.shape[0]
    i = pl.program_id(0)
    last = pl.num_programs(0) - 1
    slot = i % 2

    def issue(tile, s, unroll):
        n_tok = pl.num_programs(0) * tm
        base = tile * tm

        def body(j, carry):
            for kk in range(TOP_K_EXPERT):
                _row_copy(ys_ref, pos_ref[kk * n_tok + base + j],
                          buf.at[s, kk], j, sem.at[s]).start()
            return carry
        lax.fori_loop(0, tm, body, 0, unroll=unroll)

    def drain(s):
        for kk in range(TOP_K_EXPERT):
            pltpu.make_async_copy(ys_ref.at[pl.ds(0, tm), :], buf.at[s, kk], sem.at[s]).wait()

    @pl.when(i == 0)
    def _():
        issue(0, 0, 8)

    drain(slot)
    issue(jnp.minimum(i + 1, last), 1 - slot, True)

    gates = gate_ref[...]
    g1 = gates[:, 0:1]
    g2 = gates[:, 1:2]
    ffn = buf[slot, 0] * g1 + buf[slot, 1] * g2
    x2 = _layer_norm(ALPHA * x1_ref[...] + ffn, g_ref[...], b_ref[...])

    gate = jax.nn.sigmoid(_dot(x2.astype(BF16), wpg_ref[...]))
    e = _dot(p_ref[...].astype(BF16), wpe_ref[...])
    t = e * gate
    ple = t * lax.rsqrt(jnp.mean(t * t, axis=-1, keepdims=True) + RMS_EPS) * gp_ref[...]
    o_ref[...] = x2 + ple

    @pl.when(i == last)
    def _():
        drain(1 - slot)


def _final(pos, x1, gates, ys, p, g, b, wpg, wpe, gp):
    N, D = x1.shape
    tm = FIN_TM
    row = lambda i, pos: (i, 0)
    return pl.pallas_call(
        _final_kernel,
        out_shape=jax.ShapeDtypeStruct((N, D), F32),
        grid_spec=pltpu.PrefetchScalarGridSpec(
            num_scalar_prefetch=1,
            grid=(N // tm,),
            in_specs=[
                pl.BlockSpec((tm, D), row),
                pl.BlockSpec((tm, LANES), row),
                pl.BlockSpec(memory_space=pl.ANY),
                pl.BlockSpec((tm, PLE_DIM), row),
                _resident(g.shape), _resident(b.shape), _resident(wpg.shape),
                _resident(wpe.shape), _resident(gp.shape),
            ],
            out_specs=pl.BlockSpec((tm, D), row),
            scratch_shapes=[pltpu.VMEM((2, TOP_K_EXPERT, tm, D), F32),
                            pltpu.SemaphoreType.DMA((2,))],
        ),
        compiler_params=_params("arbitrary"),
        name="final",
    )(pos, x1, gates, ys, p, g, b, wpg, wpe, gp)


def _rope_table(positions):
    inv_freq = 1.0 / (ROPE_THETA ** (jnp.arange(0, MLA_ROPE_DIM, 2, dtype=F32) / MLA_ROPE_DIM))
    half = MLA_ROPE_DIM // 2
    phase = jnp.concatenate([jnp.zeros((2 * half,), F32), jnp.full((2 * half,), np.pi / 2, F32)])
    return jnp.cos(positions.astype(F32)[..., None] * jnp.tile(inv_freq, 4) - phase)


def _layer(x, p_i, rope_t, w_in, rpb, q_norm_g, kv_norm_g, w_uq, w_uk, w_uv, w_o, ln1_g, ln1_b,
           w_group, b_group, w_router, b_router, w_gate, w_up, w_down, ln2_g, ln2_b,
           w_ple, w_ple_gate, ple_norm_g):
    B, S, D = x.shape
    N = B * S
    s3 = 3 * NA_WIDTH
    half = MLA_ROPE_DIM // 2

    wkr = w_in[:, s3 + Q_LORA_RANK + KV_LORA_RANK:].astype(BF16)
    wvt = w_in[:, 2 * NA_WIDTH:s3].T.astype(BF16)
    uq = w_uq.reshape(Q_LORA_RANK, MLA_HEADS, MLA_QK_DIM)
    nope, x1c, x2c = (uq[..., :MLA_NOPE_DIM], uq[..., MLA_NOPE_DIM:MLA_NOPE_DIM + half],
                      uq[..., MLA_NOPE_DIM + half:])
    wuq = jnp.concatenate([nope, x1c, x2c, x2c, x1c], axis=-1).reshape(Q_LORA_RANK, -1).astype(BF16)
    wuk = w_uk.astype(BF16)
    wuvt = w_uv.T.astype(BF16)
    row = lambda v: v.reshape(1, -1).astype(F32)

    q_na, k_na, v_nat, q_m, k_m, v_mt = _proj(x, rope_t, w_in, wvt, wkr, wuq, wuk, wuvt,
                                               row(q_norm_g), row(kv_norm_g))
    o_na = _na(q_na, k_na, v_nat, _na_bias_table(rpb))
    o_mla = _mla(q_m, k_m, v_mt)

    pad = LANES - N_EXPERTS - N_GROUPS
    wr = jnp.concatenate([w_router, w_group, jnp.zeros((D, pad), F32)], axis=1).astype(BF16)
    br = jnp.concatenate([b_router.reshape(-1), b_group, jnp.zeros((pad,), F32)]).reshape(1, -1)
    x1, ri, gates, cnt = _mix_out(o_na.reshape(N, -1), o_mla.reshape(N, -1), x.reshape(N, D),
                                       w_o.astype(BF16), row(ln1_g), row(ln1_b), wr, br)

    tb = MOE_TB
    counts = cnt[0, :N_EXPERTS].astype(jnp.int32)
    padded = ((counts + tb - 1) // tb) * tb
    pad_ends = jnp.cumsum(padded)
    pad_starts = pad_ends - padded
    pos = _positions(pad_starts.astype(jnp.int32), ri).reshape(-1)
    fill = jnp.concatenate([pad_starts + counts, padded - counts, pad_ends[-1:]]).astype(jnp.int32)
    n_blocks = (N * TOP_K_EXPERT + N_EXPERTS * (tb - 1) + tb - 1) // tb
    blk_row = jnp.arange(n_blocks, dtype=jnp.int32) * tb
    block_e = jnp.minimum(jnp.sum(pad_ends[None, :] <= blk_row[:, None], axis=1),
                          N_EXPERTS - 1).astype(jnp.int32)
    n_used = (pad_ends[-1:] // tb).astype(jnp.int32)
    nxt_blk = pad_ends[block_e] // tb
    next_e = jnp.where(nxt_blk < n_used[0], block_e[jnp.minimum(nxt_blk, n_blocks - 1)],
                       -1).astype(jnp.int32)

    xs = _dispatch(pos, fill, x1, n_blocks * tb)
    ys = _experts(block_e, next_e, n_used, xs, w_gate, w_up, w_down)
    out = _final(pos, x1, gates, ys, p_i.reshape(N, -1), row(ln2_g), row(ln2_b),
                 w_ple_gate.astype(BF16), w_ple.astype(BF16), row(ple_norm_g))
    return out.reshape(B, S, D)


def kernel(x, p, positions, w_in, rpb, q_norm_g, kv_norm_g, w_uq, w_uk, w_uv, w_o, ln1_g, ln1_b,
           w_group, b_group, w_router, b_router, w_gate, w_up, w_down, ln2_g, ln2_b,
           w_ple, w_ple_gate, ple_norm_g):
    rope_t = _rope_table(positions)
    for i in range(DEPTH):
        x = _layer(x, p[i], rope_t, w_in[i], rpb[i], q_norm_g[i], kv_norm_g[i], w_uq[i], w_uk[i],
                   w_uv[i], w_o[i], ln1_g[i], ln1_b[i], w_group[i], b_group[i], w_router[i],
                   b_router[i], w_gate[i], w_up[i], w_down[i], ln2_g[i], ln2_b[i],
                   w_ple[i], w_ple_gate[i], ple_norm_g[i])
    return x
```

```python
import numpy as np
import jax
import jax.numpy as jnp
from jax import lax
from jax.experimental import pallas as pl
from jax.experimental.pallas import tpu as pltpu

D_MODEL = 2048
DEPTH = 1
GRID_W = 64
PLE_DIM = 256
NA_HEADS = 8
NA_HEAD_DIM = 128
NA_WIN_H = 8
NA_WIN_W = 16
NA_WIDTH = NA_HEADS * NA_HEAD_DIM
MLA_HEADS = 8
MLA_NOPE_DIM = 128
MLA_ROPE_DIM = 64
MLA_V_DIM = 128
MLA_QK_DIM = MLA_NOPE_DIM + MLA_ROPE_DIM
Q_LORA_RANK = 512
KV_LORA_RANK = 512
ROPE_THETA = 10000.0
N_GROUPS = 8
EXPERTS_PER_GROUP = 8
N_EXPERTS = N_GROUPS * EXPERTS_PER_GROUP
TOP_K_EXPERT = 2
D_EXPERT = 512
ALPHA = (2 * DEPTH) ** 0.25
LN_EPS = 1e-5
RMS_EPS = 1e-6

LANES = 128
SUBLANES = 8
VMEM_LIMIT_BYTES = 56 * 1024 * 1024

PROJ_TM = 256
NA_ROWS = 4
NA_HEADS_PER_STEP = 4
NA_QB = NA_ROWS * GRID_W
NA_WIN_ROWS = NA_ROWS + NA_WIN_H
NA_KB = NA_WIN_ROWS * GRID_W
MLA_TQ = 256
MLA_KC = 1024
MLA_TILES = 8
MIX_TM = 512
MIX_SUB = 256
MOE_TB = 256
DISP_TM = 256
FIN_TM = 256
NEG_BIG = -1e30
LOG2E = 1.4426950408889634

BF16 = jnp.bfloat16
F32 = jnp.float32

_NT = (((1,), (1,)), ((), ()))


def _dot(a, b):
    return jnp.dot(a, b, preferred_element_type=F32)


def _dot_nt(a, b):
    return lax.dot_general(a, b, _NT, preferred_element_type=F32)


def _params(*sem):
    return pltpu.CompilerParams(dimension_semantics=sem, vmem_limit_bytes=VMEM_LIMIT_BYTES)


def _resident(shape):
    nd = len(shape)
    return pl.BlockSpec(shape, lambda *_: (0,) * nd, pipeline_mode=pl.Buffered(1))


def _proj_kernel(x_ref, t_ref, wqk_ref, wvt_ref, wc_ref, wuq_ref, wuk_ref, wuvt_ref, gq_ref, gkv_ref,
                 qna_ref, kna_ref, vnat_ref, qm_ref, km_ref, vmt_ref):
    xb = x_ref[0].astype(BF16)
    na_scale = NA_HEAD_DIM ** -0.5 * LOG2E
    mla_scale = MLA_QK_DIM ** -0.5 * LOG2E
    q = _dot(xb, wqk_ref[:, 0:NA_WIDTH]) * na_scale
    k = _dot(xb, wqk_ref[:, NA_WIDTH:2 * NA_WIDTH])
    vt = _dot_nt(wvt_ref[...], xb)
    for h in range(NA_HEADS):
        sl = slice(h * NA_HEAD_DIM, (h + 1) * NA_HEAD_DIM)
        qna_ref[0, h] = q[:, sl].astype(BF16)
        kna_ref[0, h] = k[:, sl].astype(BF16)
        vnat_ref[0, h] = vt[sl, :].astype(BF16)

    half = MLA_ROPE_DIM // 2
    cq = _dot(xb, wc_ref[:, 0:Q_LORA_RANK])
    ckv = _dot(xb, wc_ref[:, Q_LORA_RANK:Q_LORA_RANK + KV_LORA_RANK])
    kr = _dot(xb, wc_ref[:, Q_LORA_RANK + KV_LORA_RANK:])
    kr4 = jnp.concatenate([kr, kr[:, half:], kr[:, :half]], axis=1)

    def rms(c, g):
        return c * lax.rsqrt(jnp.mean(c * c, axis=-1, keepdims=True) + RMS_EPS) * g

    cqn = rms(cq, gq_ref[...]).astype(BF16)
    ckvn = rms(ckv, gkv_ref[...]).astype(BF16)

    t = t_ref[0]
    e = kr4 * t
    lane = lax.broadcasted_iota(jnp.int32, (1, LANES), 1)
    sign = jnp.where((lane // 32) % 2 == 0, -1.0, 1.0).astype(F32)
    kpe = (e + sign * pltpu.roll(e, 64, 1)).astype(BF16)

    qf = _dot(cqn, wuq_ref[...])
    kn = _dot(ckvn, wuk_ref[...])
    vmt = _dot_nt(wuvt_ref[...], ckvn)
    ts = t * mla_scale
    for h in range(MLA_HEADS):
        qm_ref[0, h, :, 0:128] = (qf[:, h * 256:h * 256 + 128] * mla_scale).astype(BF16)
        qm_ref[0, h, :, 128:256] = (qf[:, h * 256 + 128:(h + 1) * 256] * ts).astype(BF16)
        km_ref[0, h, :, 0:128] = kn[:, h * 128:(h + 1) * 128].astype(BF16)
        km_ref[0, h, :, 128:256] = kpe
        vmt_ref[0, h] = vmt[h * 128:(h + 1) * 128, :].astype(BF16)


def _proj(x, rope_t, wqk, wvt, wc, wuq, wuk, wuvt, gq, gkv):
    B, S, D = x.shape
    tm = PROJ_TM
    hm = lambda b, i: (b, 0, i, 0)
    hmt = lambda b, i: (b, 0, 0, i)
    out_shape = (
        jax.ShapeDtypeStruct((B, NA_HEADS, S, NA_HEAD_DIM), BF16),
        jax.ShapeDtypeStruct((B, NA_HEADS, S, NA_HEAD_DIM), BF16),
        jax.ShapeDtypeStruct((B, NA_HEADS, NA_HEAD_DIM, S), BF16),
        jax.ShapeDtypeStruct((B, MLA_HEADS, S, 256), BF16),
        jax.ShapeDtypeStruct((B, MLA_HEADS, S, 256), BF16),
        jax.ShapeDtypeStruct((B, MLA_HEADS, MLA_V_DIM, S), BF16),
    )
    return pl.pallas_call(
        _proj_kernel,
        out_shape=out_shape,
        grid=(B, S // tm),
        in_specs=[
            pl.BlockSpec((1, tm, D), lambda b, i: (b, i, 0)),
            pl.BlockSpec((1, tm, LANES), lambda b, i: (b, i, 0)),
            _resident(wqk.shape), _resident(wvt.shape), _resident(wc.shape), _resident(wuq.shape),
            _resident(wuk.shape), _resident(wuvt.shape), _resident(gq.shape), _resident(gkv.shape),
        ],
        out_specs=(
            pl.BlockSpec((1, NA_HEADS, tm, NA_HEAD_DIM), hm),
            pl.BlockSpec((1, NA_HEADS, tm, NA_HEAD_DIM), hm),
            pl.BlockSpec((1, NA_HEADS, NA_HEAD_DIM, tm), hmt),
            pl.BlockSpec((1, MLA_HEADS, tm, 256), hm),
            pl.BlockSpec((1, MLA_HEADS, tm, 256), hm),
            pl.BlockSpec((1, MLA_HEADS, MLA_V_DIM, tm), hmt),
        ),
        compiler_params=_params("arbitrary", "arbitrary"),
        name="proj",
    )(x, rope_t, wqk, wvt, wc, wuq, wuk, wuvt, gq, gkv)


def _na_bias_table(rpb):
    rows = GRID_W
    n_blocks = rows // NA_ROWS
    n_ro, n_co = 2 * NA_WIN_H - 1, 2 * NA_WIN_W - 1
    kr, qr = np.arange(NA_WIN_ROWS), np.arange(NA_ROWS)
    kc, qc = np.arange(GRID_W), np.arange(GRID_W)
    c0 = np.clip(qc - NA_WIN_W // 2, 0, GRID_W - NA_WIN_W)
    col_ok = (kc[:, None] >= c0[None, :]) & (kc[:, None] < c0[None, :] + NA_WIN_W)
    col_off = np.clip(kc[:, None] - qc[None, :] + (NA_WIN_W - 1), 0, n_co - 1)
    oh_c = np.zeros((n_co, GRID_W, GRID_W), np.float32)
    oh_c[col_off, kc[:, None], qc[None, :]] = 1.0
    oh_r = np.zeros((3, n_ro, NA_WIN_ROWS, NA_ROWS), np.float32)
    row_ok = np.zeros((3, NA_WIN_ROWS, NA_ROWS), bool)
    for c, rb in enumerate((0, 1, n_blocks - 1)):
        w0 = int(np.clip(NA_ROWS * rb - NA_WIN_H // 2, 0, rows - NA_WIN_ROWS))
        r = NA_ROWS * rb + qr
        r0 = np.clip(r - NA_WIN_H // 2, 0, rows - NA_WIN_H)
        krow = w0 + kr
        row_ok[c] = (krow[:, None] >= r0[None, :]) & (krow[:, None] < r0[None, :] + NA_WIN_H)
        row_off = np.clip(krow[:, None] - r[None, :] + (NA_WIN_H - 1), 0, n_ro - 1)
        oh_r[c, row_off, kr[:, None], qr[None, :]] = 1.0
    sel = np.zeros((NA_ROWS, n_co, GRID_W, NA_ROWS, GRID_W), np.float32)
    for r in range(NA_ROWS):
        sel[r, :, :, r, :] = oh_c
    sel = sel.reshape(NA_ROWS * n_co, GRID_W, NA_QB)
    ok = row_ok[:, :, None, :, None] & col_ok[None, None, :, None, :]
    mask = np.where(ok, 0.0, NEG_BIG).astype(np.float32).reshape(3, NA_WIN_ROWS, GRID_W, NA_QB)
    hi = lax.Precision.HIGHEST
    a = jnp.einsum('hrc,zrkq->hzkqc', rpb.astype(F32) * LOG2E, oh_r, precision=hi)
    a = a.reshape(rpb.shape[0], 3, NA_WIN_ROWS, NA_ROWS * n_co)
    tab = jnp.einsum('hzkj,jxn->hzkxn', a, sel, precision=hi) + mask[None]
    return tab.reshape(rpb.shape[0], 3, NA_KB, NA_QB)


def _na_kernel(q_ref, k_ref, vt_ref, tab_ref, o_ref, st_ref, m_ref):
    n_heads = q_ref.shape[1]
    n_blocks = q_ref.shape[2] // NA_QB
    rows = n_blocks * NA_ROWS
    dh = q_ref.shape[3]

    def window(rb):
        w0 = jnp.clip(NA_ROWS * rb - NA_WIN_H // 2, 0, rows - NA_WIN_ROWS)
        return pl.multiple_of(w0 * GRID_W, NA_QB)

    def scores(rb, slot):
        tok0 = window(rb)
        q0 = pl.multiple_of(rb * NA_QB, NA_QB)
        cls = jnp.where(rb == 0, 0, jnp.where(rb == n_blocks - 1, 2, 1))
        for h in range(n_heads):
            st = _dot_nt(k_ref[0, h, pl.ds(tok0, NA_KB), :],
                         q_ref[0, h, pl.ds(q0, NA_QB), :]) + tab_ref[h, cls]
            st_ref[h, slot] = st
            m_ref[h, slot] = jnp.max(st, axis=0, keepdims=True)

    def output(rb, slot):
        tok0 = window(rb)
        q0 = pl.multiple_of(rb * NA_QB, NA_QB)
        for h in range(n_heads):
            p = jnp.exp2(st_ref[h, slot] - m_ref[h, slot])
            l = jnp.sum(p, axis=0, keepdims=True)
            ot = _dot(vt_ref[0, h, :, pl.ds(tok0, NA_KB)], p.astype(BF16)) / l
            o_ref[0, pl.ds(q0, NA_QB), h * dh:(h + 1) * dh] = ot.T.astype(o_ref.dtype)

    scores(0, 0)

    def body(t, carry):
        rb = 2 * t + 1
        scores(rb, 1)
        output(rb - 1, 0)
        scores(rb + 1, 0)
        output(rb, 1)
        return carry

    lax.fori_loop(0, n_blocks // 2 - 1, body, 0)
    scores(n_blocks - 1, 1)
    output(n_blocks - 2, 0)
    output(n_blocks - 1, 1)


def _na(q, k, vt, tab):
    B, H, S, Dh = q.shape
    hs = NA_HEADS_PER_STEP
    return pl.pallas_call(
        _na_kernel,
        out_shape=jax.ShapeDtypeStruct((B, S, H * Dh), BF16),
        grid=(H // hs, B),
        in_specs=[
            pl.BlockSpec((1, hs, S, Dh), lambda h, b: (b, h, 0, 0)),
            pl.BlockSpec((1, hs, S, Dh), lambda h, b: (b, h, 0, 0)),
            pl.BlockSpec((1, hs, Dh, S), lambda h, b: (b, h, 0, 0)),
            pl.BlockSpec((hs, 3, NA_KB, NA_QB), lambda h, b: (h, 0, 0, 0),
                         pipeline_mode=pl.Buffered(1)),
        ],
        out_specs=pl.BlockSpec((1, S, hs * Dh), lambda h, b: (b, 0, h)),
        scratch_shapes=[pltpu.VMEM((hs, 2, NA_KB, NA_QB), F32), pltpu.VMEM((hs, 2, 1, NA_QB), F32)],
        compiler_params=_params("arbitrary", "arbitrary"),
        name="na",
    )(q, k, vt, tab)


def _mla_kernel(q_ref, k_ref, vt_ref, o_ref, st_ref):
    n_chunks = k_ref.shape[2] // MLA_KC
    n_tiles = q_ref.shape[2] // MLA_TQ

    def pass1(j, slot):
        q = q_ref[0, 0, j * MLA_TQ:(j + 1) * MLA_TQ, :]
        m = None
        for c in range(n_chunks):
            ks = slice(c * MLA_KC, (c + 1) * MLA_KC)
            st = _dot_nt(k_ref[0, 0, ks, :], q)
            st_ref[slot, ks, :] = st
            mc = jnp.max(st, axis=0, keepdims=True)
            m = mc if c == 0 else jnp.maximum(m, mc)
        return m

    def pass2(j, slot, m):
        l = acc = None
        for c in range(n_chunks):
            ks = slice(c * MLA_KC, (c + 1) * MLA_KC)
            p = jnp.exp2(st_ref[slot, ks, :] - m)
            ps = jnp.sum(p, axis=0, keepdims=True)
            pv = _dot(vt_ref[0, 0, :, ks], p.astype(BF16))
            l = ps if c == 0 else l + ps
            acc = pv if c == 0 else acc + pv
        o_ref[0, j * MLA_TQ:(j + 1) * MLA_TQ, :] = (acc / l).T.astype(o_ref.dtype)

    m_prev = pass1(0, 0)
    for j in range(1, n_tiles):
        m_cur = pass1(j, j % 2)
        pass2(j - 1, (j - 1) % 2, m_prev)
        m_prev = m_cur
    pass2(n_tiles - 1, (n_tiles - 1) % 2, m_prev)


def _mla(q, k, vt):
    B, H, S, Dq = q.shape
    Dv = MLA_V_DIM
    tq = MLA_TQ * MLA_TILES
    return pl.pallas_call(
        _mla_kernel,
        out_shape=jax.ShapeDtypeStruct((B, S, H * Dv), BF16),
        grid=(B, H, S // tq),
        in_specs=[
            pl.BlockSpec((1, 1, tq, Dq), lambda b, h, i: (b, h, i, 0)),
            pl.BlockSpec((1, 1, S, Dq), lambda b, h, i: (b, h, 0, 0)),
            pl.BlockSpec((1, 1, Dv, S), lambda b, h, i: (b, h, 0, 0)),
        ],
        out_specs=pl.BlockSpec((1, tq, Dv), lambda b, h, i: (b, i, h)),
        scratch_shapes=[pltpu.VMEM((2, S, MLA_TQ), F32)],
        compiler_params=_params("arbitrary", "arbitrary", "arbitrary"),
        name="mla",
    )(q, k, vt)


def _layer_norm(z, g, b):
    mu = jnp.mean(z, axis=-1, keepdims=True)
    zc = z - mu
    var = jnp.mean(zc * zc, axis=-1, keepdims=True)
    return zc * lax.rsqrt(var + LN_EPS) * g + b


def _first_lane_where(cond, lane):
    return jnp.min(jnp.where(cond, lane, LANES), axis=-1, keepdims=True)


def _mix_kernel(ona_ref, omla_ref, x_ref, wo_ref, g_ref, b_ref, wr_ref, br_ref,
                x1_ref, ri_ref, gate_ref, cnt_ref, carry_ref):
    step = pl.program_id(0)

    @pl.when(step == 0)
    def _():
        carry_ref[...] = jnp.zeros_like(carry_ref)

    ts = MIX_SUB
    subs = [slice(j * ts, (j + 1) * ts) for j in range(x_ref.shape[0] // ts)]
    lane = lax.broadcasted_iota(jnp.int32, (ts, LANES), 1)
    g_mask = (lane >= N_EXPERTS) & (lane < N_EXPERTS + N_GROUPS)
    r_i = lax.broadcasted_iota(jnp.int32, (ts, ts), 0)
    c_i = lax.broadcasted_iota(jnp.int32, (ts, ts), 1)
    tri = jnp.where(c_i < r_i, 1.0, 0.0).astype(BF16)

    def project(rows):
        return (_dot(ona_ref[rows, :], wo_ref[0:NA_WIDTH, :])
                + _dot(omla_ref[rows, :], wo_ref[NA_WIDTH:, :]))

    mix = project(subs[0])
    for j, rows in enumerate(subs):
        nxt = project(subs[j + 1]) if j + 1 < len(subs) else None
        _norm_route(rows, mix, lane, g_mask, tri, x_ref, g_ref, b_ref, wr_ref, br_ref,
                    x1_ref, ri_ref, gate_ref, carry_ref)
        mix = nxt
    cnt_ref[...] = carry_ref[...]


def _norm_route(rows, mix, lane, g_mask, tri, x_ref, g_ref, b_ref, wr_ref, br_ref,
                x1_ref, ri_ref, gate_ref, carry_ref):
    x1 = _layer_norm(ALPHA * x_ref[rows, :] + mix, g_ref[...], b_ref[...])
    x1_ref[rows, :] = x1

    logit = _dot(x1.astype(BF16), wr_ref[...]) + br_ref[...]
    gl = jnp.where(g_mask, logit, NEG_BIG)
    gmax = jnp.max(gl, axis=-1, keepdims=True)
    gsum = jnp.sum(jnp.where(g_mask, jnp.exp(gl - gmax), 0.0), axis=-1, keepdims=True)
    g_val = 1.0 / gsum
    g_idx = _first_lane_where(gl == gmax, lane) - N_EXPERTS

    e_mask = (lane < N_EXPERTS) & ((lane // EXPERTS_PER_GROUP) == g_idx)
    el = jnp.where(e_mask, logit, NEG_BIG)
    emax = jnp.max(el, axis=-1, keepdims=True)
    esum = jnp.sum(jnp.where(e_mask, jnp.exp(el - emax), 0.0), axis=-1, keepdims=True)
    i1 = _first_lane_where(el == emax, lane)
    el2 = jnp.where(lane == i1, NEG_BIG, el)
    emax2 = jnp.max(el2, axis=-1, keepdims=True)
    i2 = _first_lane_where(el2 == emax2, lane)
    v1 = 1.0 / esum
    v2 = jnp.exp(emax2 - emax) / esum
    vsum = v1 + v2
    w1 = g_val * (v1 / vsum)
    w2 = g_val * (v2 / vsum)

    oh1 = lane == i1
    oh2 = lane == i2
    oh = jnp.where(oh1 | oh2, 1.0, 0.0)
    before = _dot(tri, oh.astype(BF16)) + carry_ref[...]
    rank1 = jnp.sum(jnp.where(oh1, before, 0.0), axis=-1, keepdims=True).astype(jnp.int32)
    rank2 = jnp.sum(jnp.where(oh2, before, 0.0), axis=-1, keepdims=True).astype(jnp.int32)
    carry_ref[...] = carry_ref[...] + jnp.sum(oh, axis=0, keepdims=True)

    ri = jnp.where(lane == 0, i1, jnp.where(lane == 1, i2,
                   jnp.where(lane == 2, rank1, jnp.where(lane == 3, rank2, 0))))
    ri_ref[:, rows] = ri.T[0:SUBLANES, :]
    gate_ref[rows, :] = jnp.where(lane == 0, w1, jnp.where(lane == 1, w2, 0.0))


def _mix_out(o_na, o_mla, x, wo, g, b, wr, br):
    N, D = x.shape
    tm = MIX_TM
    row = lambda i: (i, 0)
    return pl.pallas_call(
        _mix_kernel,
        out_shape=(
            jax.ShapeDtypeStruct((N, D), F32),
            jax.ShapeDtypeStruct((SUBLANES, N), jnp.int32),
            jax.ShapeDtypeStruct((N, LANES), F32),
            jax.ShapeDtypeStruct((1, LANES), F32),
        ),
        grid=(N // tm,),
        in_specs=[
            pl.BlockSpec((tm, NA_WIDTH), row),
            pl.BlockSpec((tm, NA_WIDTH), row),
            pl.BlockSpec((tm, D), row),
            _resident(wo.shape), _resident(g.shape), _resident(b.shape),
            _resident(wr.shape), _resident(br.shape),
        ],
        out_specs=(
            pl.BlockSpec((tm, D), row),
            pl.BlockSpec((SUBLANES, tm), lambda i: (0, i)),
            pl.BlockSpec((tm, LANES), row),
            pl.BlockSpec((1, LANES), lambda i: (0, 0)),
        ),
        scratch_shapes=[pltpu.VMEM((1, LANES), F32)],
        compiler_params=_params("arbitrary"),
        name="mix_out",
    )(o_na, o_mla, x, wo, g, b, wr, br)


def _pos_kernel(start_ref, ri_ref, pos_ref):
    eid = ri_ref[0:TOP_K_EXPERT, :]
    pos = ri_ref[TOP_K_EXPERT:2 * TOP_K_EXPERT, :]
    for e in range(N_EXPERTS):
        pos = pos + jnp.where(eid == e, start_ref[e], 0)
    pos_ref[...] = pos


def _positions(pad_starts, ri):
    n = ri.shape[1]
    return pl.pallas_call(
        _pos_kernel,
        out_shape=jax.ShapeDtypeStruct((TOP_K_EXPERT, n), jnp.int32),
        grid_spec=pltpu.PrefetchScalarGridSpec(
            num_scalar_prefetch=1,
            grid=(1,),
            in_specs=[pl.BlockSpec(ri.shape, lambda i, s: (0, 0))],
            out_specs=pl.BlockSpec((TOP_K_EXPERT, n), lambda i, s: (0, 0)),
        ),
        compiler_params=_params("arbitrary"),
        name="positions",
    )(pad_starts, ri)


def _row_copy(src, src_row, dst, dst_row, sem):
    return pltpu.make_async_copy(src.at[pl.ds(src_row, 1), :], dst.at[pl.ds(dst_row, 1), :], sem)


_PAD_CHUNKS = tuple(1 << b for b in reversed(range(3, MOE_TB.bit_length() - 1)))


def _dispatch_kernel(pos_ref, fill_ref, x_ref, xs_ref, zero_ref, sem, zsem):
    tm = x_ref.shape[0]
    n_tok = pl.num_programs(0) * tm
    base = pl.program_id(0) * tm

    def pad_copies(fn):
        def per_expert(e, carry):
            start = fill_ref[e]
            n = fill_ref[N_EXPERTS + e]
            head = (-start) & (SUBLANES - 1)
            for r in range(SUBLANES - 1):
                @pl.when(r < head)
                def _(r=r):
                    fn(_row_copy(zero_ref, 0, xs_ref, start + r, zsem))
            off = start + head
            rem = n - head
            for c in _PAD_CHUNKS:
                hit = (rem & c) != 0

                @pl.when(hit)
                def _(off=off, c=c):
                    fn(pltpu.make_async_copy(zero_ref.at[pl.ds(0, c), :],
                                             xs_ref.at[pl.ds(pl.multiple_of(off, SUBLANES), c), :],
                                             zsem))
                off = off + jnp.where(hit, c, 0)
            return carry
        lax.fori_loop(0, N_EXPERTS, per_expert, 0)

        zrows = zero_ref.shape[0]

        def tail(t, carry):
            row = pl.multiple_of(fill_ref[2 * N_EXPERTS] + t * zrows, zrows)
            fn(pltpu.make_async_copy(zero_ref, xs_ref.at[pl.ds(row, zrows), :], zsem))
            return carry
        lax.fori_loop(0, (xs_ref.shape[0] - fill_ref[2 * N_EXPERTS]) // zrows, tail, 0)

    @pl.when(pl.program_id(0) == 0)
    def _():
        zero_ref[...] = jnp.zeros_like(zero_ref)
        pad_copies(lambda d: d.start())

    def issue(j, carry):
        for kk in range(TOP_K_EXPERT):
            _row_copy(x_ref, j, xs_ref, pos_ref[kk * n_tok + base + j], sem).start()
        return carry

    lax.fori_loop(0, tm, issue, 0, unroll=8)

    @pl.when(pl.program_id(0) == 0)
    def _():
        pad_copies(lambda d: d.wait())

    for kk in range(TOP_K_EXPERT):
        pltpu.make_async_copy(x_ref, xs_ref.at[pl.ds(0, tm), :], sem).wait()


def _dispatch(pos, fill, x1, n_rows):
    N, C = x1.shape
    tm = DISP_TM
    return pl.pallas_call(
        _dispatch_kernel,
        out_shape=jax.ShapeDtypeStruct((n_rows, C), x1.dtype),
        grid_spec=pltpu.PrefetchScalarGridSpec(
            num_scalar_prefetch=2,
            grid=(N // tm,),
            in_specs=[pl.BlockSpec((tm, C), lambda i, pos, fill: (i, 0))],
            out_specs=pl.BlockSpec(memory_space=pl.ANY),
            scratch_shapes=[pltpu.VMEM((_PAD_CHUNKS[0], C), x1.dtype),
                            pltpu.SemaphoreType.DMA, pltpu.SemaphoreType.DMA],
        ),
        compiler_params=_params("arbitrary"),
        name="dispatch",
    )(pos, fill, x1)


def _experts_kernel(be_ref, nxt_ref, nb_ref, xs_ref, wg_hbm, wu_hbm, wd_hbm, ys_ref,
                    wg_f, wu_f, wd_f, wg_s, wu_s, wd_s, sem):
    i = pl.program_id(0)
    active = i < nb_ref[0]
    e = be_ref[i]
    fresh = active & ((i == 0) | (e != be_ref[jnp.maximum(i - 1, 0)]))

    def fetch(ex):
        return (pltpu.make_async_copy(wg_hbm.at[ex], wg_f, sem.at[0]),
                pltpu.make_async_copy(wu_hbm.at[ex], wu_f, sem.at[1]),
                pltpu.make_async_copy(wd_hbm.at[ex], wd_f, sem.at[2]))

    @pl.when(i == 0)
    def _():
        for d in fetch(e):
            d.start()

    @pl.when(fresh)
    def _():
        for d in fetch(e):
            d.wait()
        wg_s[...] = wg_f[...].astype(BF16)
        wu_s[...] = wu_f[...].astype(BF16)
        wd_s[...] = wd_f[...].astype(BF16)
        nx = nxt_ref[i]

        @pl.when(nx >= 0)
        def _():
            for d in fetch(nx):
                d.start()

    @pl.when(active)
    def _():
        xb = xs_ref[...].astype(BF16)
        gp = _dot(xb, wg_s[...])
        up = _dot(xb, wu_s[...])
        hdn = (gp * jax.nn.sigmoid(gp) * up).astype(BF16)
        ys_ref[...] = _dot(hdn, wd_s[...])

    @pl.when(jnp.logical_not(active))
    def _():
        ys_ref[...] = jnp.zeros_like(ys_ref)


def _experts(block_e, next_e, n_used, xs, w_gate, w_up, w_down):
    P, C = xs.shape
    E, D, F = w_gate.shape
    tb = MOE_TB
    n_blocks = P // tb
    hbm = pl.BlockSpec(memory_space=pl.ANY)
    return pl.pallas_call(
        _experts_kernel,
        out_shape=jax.ShapeDtypeStruct((P, C), F32),
        grid_spec=pltpu.PrefetchScalarGridSpec(
            num_scalar_prefetch=3,
            grid=(n_blocks,),
            in_specs=[
                pl.BlockSpec((tb, C), lambda i, be, nx, nb: (jnp.minimum(i, nb[0] - 1), 0)),
                hbm, hbm, hbm,
            ],
            out_specs=pl.BlockSpec((tb, C), lambda i, be, nx, nb: (i, 0)),
            scratch_shapes=[pltpu.VMEM((D, F), F32), pltpu.VMEM((D, F), F32), pltpu.VMEM((F, D), F32),
                            pltpu.VMEM((D, F), BF16), pltpu.VMEM((D, F), BF16),
                            pltpu.VMEM((F, D), BF16), pltpu.SemaphoreType.DMA((3,))],
        ),
        compiler_params=_params("arbitrary"),
        name="experts",
    )(block_e, next_e, n_used, xs, w_gate, w_up, w_down)


def _final_kernel(pos_ref, x1_ref, gate_ref, ys_ref, p_ref, g_ref, b_ref, wpg_ref, wpe_ref,
                  gp_ref, o_ref, buf, sem):
    tm = x1_ref.shape[0]
    i = pl.program_id(0)
    last = pl.num_programs(0) - 1
    slot = i % 2

    def issue(tile, s, unroll):
        n_tok = pl.num_programs(0) * tm
        base = tile * tm

        def body(j, carry):
            for kk in range(TOP_K_EXPERT):
                _row_copy(ys_ref, pos_ref[kk * n_tok + base + j],
                          buf.at[s, kk], j, sem.at[s]).start()
            return carry
        lax.fori_loop(0, tm, body, 0, unroll=unroll)

    def drain(s):
        for kk in range(TOP_K_EXPERT):
            pltpu.make_async_copy(ys_ref.at[pl.ds(0, tm), :], buf.at[s, kk], sem.at[s]).wait()

    @pl.when(i == 0)
    def _():
        issue(0, 0, 8)

    drain(slot)
    issue(jnp.minimum(i + 1, last), 1 - slot, True)

    gates = gate_ref[...]
    g1 = gates[:, 0:1]
    g2 = gates[:, 1:2]
    ffn = buf[slot, 0] * g1 + buf[slot, 1] * g2
    x2 = _layer_norm(ALPHA * x1_ref[...] + ffn, g_ref[...], b_ref[...])

    gate = jax.nn.sigmoid(_dot(x2.astype(BF16), wpg_ref[...]))
    e = _dot(p_ref[...].astype(BF16), wpe_ref[...])
    t = e * gate
    ple = t * lax.rsqrt(jnp.mean(t * t, axis=-1, keepdims=True) + RMS_EPS) * gp_ref[...]
    o_ref[...] = x2 + ple

    @pl.when(i == last)
    def _():
        drain(1 - slot)


def _final(pos, x1, gates, ys, p, g, b, wpg, wpe, gp):
    N, D = x1.shape
    tm = FIN_TM
    row = lambda i, pos: (i, 0)
    return pl.pallas_call(
        _final_kernel,
        out_shape=jax.ShapeDtypeStruct((N, D), F32),
        grid_spec=pltpu.PrefetchScalarGridSpec(
            num_scalar_prefetch=1,
            grid=(N // tm,),
            in_specs=[
                pl.BlockSpec((tm, D), row),
                pl.BlockSpec((tm, LANES), row),
                pl.BlockSpec(memory_space=pl.ANY),
                pl.BlockSpec((tm, PLE_DIM), row),
                _resident(g.shape), _resident(b.shape), _resident(wpg.shape),
                _resident(wpe.shape), _resident(gp.shape),
            ],
            out_specs=pl.BlockSpec((tm, D), row),
            scratch_shapes=[pltpu.VMEM((2, TOP_K_EXPERT, tm, D), F32),
                            pltpu.SemaphoreType.DMA((2,))],
        ),
        compiler_params=_params("arbitrary"),
        name="final",
    )(pos, x1, gates, ys, p, g, b, wpg, wpe, gp)


def _rope_table(positions):
    inv_freq = 1.0 / (ROPE_THETA ** (jnp.arange(0, MLA_ROPE_DIM, 2, dtype=F32) / MLA_ROPE_DIM))
    half = MLA_ROPE_DIM // 2
    phase = jnp.concatenate([jnp.zeros((2 * half,), F32), jnp.full((2 * half,), np.pi / 2, F32)])
    return jnp.cos(positions.astype(F32)[..., None] * jnp.tile(inv_freq, 4) - phase)


def _layer(x, p_i, rope_t, w_in, rpb, q_norm_g, kv_norm_g, w_uq, w_uk, w_uv, w_o, ln1_g, ln1_b,
           w_group, b_group, w_router, b_router, w_gate, w_up, w_down, ln2_g, ln2_b,
           w_ple, w_ple_gate, ple_norm_g):
    B, S, D = x.shape
    N = B * S
    s3 = 3 * NA_WIDTH
    half = MLA_ROPE_DIM // 2

    wqk = w_in[:, :2 * NA_WIDTH].astype(BF16)
    wc = w_in[:, s3:].astype(BF16)
    wvt = w_in[:, 2 * NA_WIDTH:s3].T.astype(BF16)
    uq = w_uq.reshape(Q_LORA_RANK, MLA_HEADS, MLA_QK_DIM)
    nope, x1c, x2c = (uq[..., :MLA_NOPE_DIM], uq[..., MLA_NOPE_DIM:MLA_NOPE_DIM + half],
                      uq[..., MLA_NOPE_DIM + half:])
    wuq = jnp.concatenate([nope, x1c, x2c, x2c, x1c], axis=-1).reshape(Q_LORA_RANK, -1).astype(BF16)
    wuk = w_uk.astype(BF16)
    wuvt = w_uv.T.astype(BF16)
    row = lambda v: v.reshape(1, -1).astype(F32)

    q_na, k_na, v_nat, q_m, k_m, v_mt = _proj(x, rope_t, wqk, wvt, wc, wuq, wuk, wuvt,
                                               row(q_norm_g), row(kv_norm_g))
    o_na = _na(q_na, k_na, v_nat, _na_bias_table(rpb))
    o_mla = _mla(q_m, k_m, v_mt)

    pad = LANES - N_EXPERTS - N_GROUPS
    wr = jnp.concatenate([w_router, w_group, jnp.zeros((D, pad), F32)], axis=1).astype(BF16)
    br = jnp.concatenate([b_router.reshape(-1), b_group, jnp.zeros((pad,), F32)]).reshape(1, -1)
    x1, ri, gates, cnt = _mix_out(o_na.reshape(N, -1), o_mla.reshape(N, -1), x.reshape(N, D),
                                       w_o.astype(BF16), row(ln1_g), row(ln1_b), wr, br)

    tb = MOE_TB
    counts = cnt[0, :N_EXPERTS].astype(jnp.int32)
    padded = ((counts + tb - 1) // tb) * tb
    pad_ends = jnp.cumsum(padded)
    pad_starts = pad_ends - padded
    pos = _positions(pad_starts.astype(jnp.int32), ri).reshape(-1)
    fill = jnp.concatenate([pad_starts + counts, padded - counts, pad_ends[-1:]]).astype(jnp.int32)
    n_blocks = (N * TOP_K_EXPERT + N_EXPERTS * (tb - 1) + tb - 1) // tb
    blk_row = jnp.arange(n_blocks, dtype=jnp.int32) * tb
    block_e = jnp.minimum(jnp.sum(pad_ends[None, :] <= blk_row[:, None], axis=1),
                          N_EXPERTS - 1).astype(jnp.int32)
    n_used = (pad_ends[-1:] // tb).astype(jnp.int32)
    nxt_blk = pad_ends[block_e] // tb
    next_e = jnp.where(nxt_blk < n_used[0], block_e[jnp.minimum(nxt_blk, n_blocks - 1)],
                       -1).astype(jnp.int32)

    xs = _dispatch(pos, fill, x1, n_blocks * tb)
    ys = _experts(block_e, next_e, n_used, xs, w_gate, w_up, w_down)
    out = _final(pos, x1, gates, ys, p_i.reshape(N, -1), row(ln2_g), row(ln2_b),
                 w_ple_gate.astype(BF16), w_ple.astype(BF16), row(ple_norm_g))
    return out.reshape(B, S, D)


def kernel(x, p, positions, w_in, rpb, q_norm_g, kv_norm_g, w_uq, w_uk, w_uv, w_o, ln1_g, ln1_b,
           w_group, b_group, w_router, b_router, w_gate, w_up, w_down, ln2_g, ln2_b,
           w_ple, w_ple_gate, ple_norm_g):
    rope_t = _rope_table(positions)
    for i in range(DEPTH):
        x = _layer(x, p[i], rope_t, w_in[i], rpb[i], q_norm_g[i], kv_norm_g[i], w_uq[i], w_uk[i],
                   w_uv[i], w_o[i], ln1_g[i], ln1_b[i], w_group[i], b_group[i], w_router[i],
                   b_router[i], w_gate[i], w_up[i], w_down[i], ln2_g[i], ln2_b[i],
                   w_ple[i], w_ple_gate[i], ple_norm_g[i])
    return x
```

```python
import numpy as np
import jax
import jax.numpy as jnp
from jax import lax
from jax.experimental import pallas as pl
from jax.experimental.pallas import tpu as pltpu

D_MODEL = 2048
DEPTH = 1
GRID_W = 64
PLE_DIM = 256
NA_HEADS = 8
NA_HEAD_DIM = 128
NA_WIN_H = 8
NA_WIN_W = 16
NA_WIDTH = NA_HEADS * NA_HEAD_DIM
MLA_HEADS = 8
MLA_NOPE_DIM = 128
MLA_ROPE_DIM = 64
MLA_V_DIM = 128
MLA_QK_DIM = MLA_NOPE_DIM + MLA_ROPE_DIM
Q_LORA_RANK = 512
KV_LORA_RANK = 512
ROPE_THETA = 10000.0
N_GROUPS = 8
EXPERTS_PER_GROUP = 8
N_EXPERTS = N_GROUPS * EXPERTS_PER_GROUP
TOP_K_EXPERT = 2
D_EXPERT = 512
ALPHA = (2 * DEPTH) ** 0.25
LN_EPS = 1e-5
RMS_EPS = 1e-6

LANES = 128
SUBLANES = 8
VMEM_LIMIT_BYTES = 56 * 1024 * 1024

PROJ_TM = 256
NA_ROWS = 4
NA_HEADS_PER_STEP = 4
NA_QB = NA_ROWS * GRID_W
NA_WIN_ROWS = NA_ROWS + NA_WIN_H
NA_KB = NA_WIN_ROWS * GRID_W
MLA_QK_PAD = MLA_NOPE_DIM + 2 * MLA_ROPE_DIM
MLA_TQ = 256
MLA_KC = 1024
MLA_TILES = 8
MIX_TM = 512
MIX_SUB = 256
MOE_TB = 256
DISP_TM = 256
FIN_TM = 256
NEG_BIG = -1e30
LOG2E = 1.4426950408889634

BF16 = jnp.bfloat16
F32 = jnp.float32

_NT = (((1,), (1,)), ((), ()))


def _dot(a, b):
    return jnp.dot(a, b, preferred_element_type=F32)


def _dot_nt(a, b):
    return lax.dot_general(a, b, _NT, preferred_element_type=F32)


def _params(*sem):
    return pltpu.CompilerParams(dimension_semantics=sem, vmem_limit_bytes=VMEM_LIMIT_BYTES)


def _resident(shape):
    nd = len(shape)
    return pl.BlockSpec(shape, lambda *_: (0,) * nd, pipeline_mode=pl.Buffered(1))


def _proj_kernel(x_ref, t_ref, wqk_ref, wvt_ref, wc_ref, wuq_ref, wuk_ref, wuvt_ref, gq_ref, gkv_ref,
                 qna_ref, kna_ref, vnat_ref, qm_ref, km_ref, vmt_ref):
    xb = x_ref[0].astype(BF16)
    na_scale = NA_HEAD_DIM ** -0.5 * LOG2E
    mla_scale = MLA_QK_DIM ** -0.5 * LOG2E
    q = _dot(xb, wqk_ref[:, 0:NA_WIDTH]) * na_scale
    k = _dot(xb, wqk_ref[:, NA_WIDTH:2 * NA_WIDTH])
    vt = _dot_nt(wvt_ref[...], xb)
    for h in range(NA_HEADS):
        sl = slice(h * NA_HEAD_DIM, (h + 1) * NA_HEAD_DIM)
        qna_ref[0, h] = q[:, sl].astype(BF16)
        kna_ref[0, h] = k[:, sl].astype(BF16)
        vnat_ref[0, h] = vt[sl, :].astype(BF16)

    half = MLA_ROPE_DIM // 2
    cq = _dot(xb, wc_ref[:, 0:Q_LORA_RANK])
    ckv = _dot(xb, wc_ref[:, Q_LORA_RANK:Q_LORA_RANK + KV_LORA_RANK])
    kr = _dot(xb, wc_ref[:, Q_LORA_RANK + KV_LORA_RANK:])
    kr4 = jnp.concatenate([kr, kr[:, half:], kr[:, :half]], axis=1)

    def rms(c, g):
        return c * lax.rsqrt(jnp.mean(c * c, axis=-1, keepdims=True) + RMS_EPS) * g

    cqn = rms(cq, gq_ref[...]).astype(BF16)
    ckvn = rms(ckv, gkv_ref[...]).astype(BF16)

    t = t_ref[0]
    e = kr4 * t
    lane = lax.broadcasted_iota(jnp.int32, (1, 2 * MLA_ROPE_DIM), 1)
    sign = jnp.where((lane // half) % 2 == 0, -1.0, 1.0).astype(F32)
    kpe = (e + sign * pltpu.roll(e, MLA_ROPE_DIM, 1)).astype(BF16)

    qf = _dot(cqn, wuq_ref[...])
    kn = _dot(ckvn, wuk_ref[...])
    vmt = _dot_nt(wuvt_ref[...], ckvn)
    ts = t * mla_scale
    nope, pad = MLA_NOPE_DIM, MLA_QK_PAD
    for h in range(MLA_HEADS):
        qm_ref[0, h, :, 0:nope] = (qf[:, h * pad:h * pad + nope] * mla_scale).astype(BF16)
        qm_ref[0, h, :, nope:pad] = (qf[:, h * pad + nope:(h + 1) * pad] * ts).astype(BF16)
        km_ref[0, h, :, 0:nope] = kn[:, h * nope:(h + 1) * nope].astype(BF16)
        km_ref[0, h, :, nope:pad] = kpe
        vmt_ref[0, h] = vmt[h * MLA_V_DIM:(h + 1) * MLA_V_DIM, :].astype(BF16)


def _proj(x, rope_t, wqk, wvt, wc, wuq, wuk, wuvt, gq, gkv):
    B, S, D = x.shape
    tm = PROJ_TM
    hm = lambda b, i: (b, 0, i, 0)
    hmt = lambda b, i: (b, 0, 0, i)
    out_shape = (
        jax.ShapeDtypeStruct((B, NA_HEADS, S, NA_HEAD_DIM), BF16),
        jax.ShapeDtypeStruct((B, NA_HEADS, S, NA_HEAD_DIM), BF16),
        jax.ShapeDtypeStruct((B, NA_HEADS, NA_HEAD_DIM, S), BF16),
        jax.ShapeDtypeStruct((B, MLA_HEADS, S, MLA_QK_PAD), BF16),
        jax.ShapeDtypeStruct((B, MLA_HEADS, S, MLA_QK_PAD), BF16),
        jax.ShapeDtypeStruct((B, MLA_HEADS, MLA_V_DIM, S), BF16),
    )
    return pl.pallas_call(
        _proj_kernel,
        out_shape=out_shape,
        grid=(B, S // tm),
        in_specs=[
            pl.BlockSpec((1, tm, D), lambda b, i: (b, i, 0)),
            pl.BlockSpec((1, tm, LANES), lambda b, i: (b, i, 0)),
            _resident(wqk.shape), _resident(wvt.shape), _resident(wc.shape), _resident(wuq.shape),
            _resident(wuk.shape), _resident(wuvt.shape), _resident(gq.shape), _resident(gkv.shape),
        ],
        out_specs=(
            pl.BlockSpec((1, NA_HEADS, tm, NA_HEAD_DIM), hm),
            pl.BlockSpec((1, NA_HEADS, tm, NA_HEAD_DIM), hm),
            pl.BlockSpec((1, NA_HEADS, NA_HEAD_DIM, tm), hmt),
            pl.BlockSpec((1, MLA_HEADS, tm, MLA_QK_PAD), hm),
            pl.BlockSpec((1, MLA_HEADS, tm, MLA_QK_PAD), hm),
            pl.BlockSpec((1, MLA_HEADS, MLA_V_DIM, tm), hmt),
        ),
        compiler_params=_params("arbitrary", "arbitrary"),
        name="proj",
    )(x, rope_t, wqk, wvt, wc, wuq, wuk, wuvt, gq, gkv)


def _na_bias_table(rpb):
    rows = GRID_W
    n_blocks = rows // NA_ROWS
    n_ro, n_co = 2 * NA_WIN_H - 1, 2 * NA_WIN_W - 1
    kr, qr = np.arange(NA_WIN_ROWS), np.arange(NA_ROWS)
    kc, qc = np.arange(GRID_W), np.arange(GRID_W)
    c0 = np.clip(qc - NA_WIN_W // 2, 0, GRID_W - NA_WIN_W)
    col_ok = (kc[:, None] >= c0[None, :]) & (kc[:, None] < c0[None, :] + NA_WIN_W)
    col_off = np.clip(kc[:, None] - qc[None, :] + (NA_WIN_W - 1), 0, n_co - 1)
    oh_c = np.zeros((n_co, GRID_W, GRID_W), np.float32)
    oh_c[col_off, kc[:, None], qc[None, :]] = 1.0
    oh_r = np.zeros((3, n_ro, NA_WIN_ROWS, NA_ROWS), np.float32)
    row_ok = np.zeros((3, NA_WIN_ROWS, NA_ROWS), bool)
    for c, rb in enumerate((0, 1, n_blocks - 1)):
        w0 = int(np.clip(NA_ROWS * rb - NA_WIN_H // 2, 0, rows - NA_WIN_ROWS))
        r = NA_ROWS * rb + qr
        r0 = np.clip(r - NA_WIN_H // 2, 0, rows - NA_WIN_H)
        krow = w0 + kr
        row_ok[c] = (krow[:, None] >= r0[None, :]) & (krow[:, None] < r0[None, :] + NA_WIN_H)
        row_off = np.clip(krow[:, None] - r[None, :] + (NA_WIN_H - 1), 0, n_ro - 1)
        oh_r[c, row_off, kr[:, None], qr[None, :]] = 1.0
    sel = np.zeros((NA_ROWS, n_co, GRID_W, NA_ROWS, GRID_W), np.float32)
    for r in range(NA_ROWS):
        sel[r, :, :, r, :] = oh_c
    sel = sel.reshape(NA_ROWS * n_co, GRID_W, NA_QB)
    ok = row_ok[:, :, None, :, None] & col_ok[None, None, :, None, :]
    mask = np.where(ok, 0.0, NEG_BIG).astype(np.float32).reshape(3, NA_WIN_ROWS, GRID_W, NA_QB)
    hi = lax.Precision.HIGHEST
    a = jnp.einsum('hrc,zrkq->hzkqc', rpb.astype(F32) * LOG2E, oh_r, precision=hi)
    a = a.reshape(rpb.shape[0], 3, NA_WIN_ROWS, NA_ROWS * n_co)
    tab = jnp.einsum('hzkj,jxn->hzkxn', a, sel, precision=hi) + mask[None]
    return tab.reshape(rpb.shape[0], 3, NA_KB, NA_QB)


def _na_kernel(q_ref, k_ref, vt_ref, tab_ref, o_ref, st_ref, m_ref):
    n_heads = q_ref.shape[1]
    n_blocks = q_ref.shape[2] // NA_QB
    rows = n_blocks * NA_ROWS
    dh = q_ref.shape[3]

    def window(rb):
        w0 = jnp.clip(NA_ROWS * rb - NA_WIN_H // 2, 0, rows - NA_WIN_ROWS)
        return pl.multiple_of(w0 * GRID_W, NA_QB)

    def scores(rb, slot):
        tok0 = window(rb)
        q0 = pl.multiple_of(rb * NA_QB, NA_QB)
        cls = jnp.where(rb == 0, 0, jnp.where(rb == n_blocks - 1, 2, 1))
        for h in range(n_heads):
            st = _dot_nt(k_ref[0, h, pl.ds(tok0, NA_KB), :],
                         q_ref[0, h, pl.ds(q0, NA_QB), :]) + tab_ref[h, cls]
            st_ref[h, slot] = st
            m_ref[h, slot] = jnp.max(st, axis=0, keepdims=True)

    def output(rb, slot):
        tok0 = window(rb)
        q0 = pl.multiple_of(rb * NA_QB, NA_QB)
        for h in range(n_heads):
            p = jnp.exp2(st_ref[h, slot] - m_ref[h, slot])
            l = jnp.sum(p, axis=0, keepdims=True)
            ot = _dot(vt_ref[0, h, :, pl.ds(tok0, NA_KB)], p.astype(BF16)) / l
            o_ref[0, pl.ds(q0, NA_QB), h * dh:(h + 1) * dh] = ot.T.astype(o_ref.dtype)

    scores(0, 0)

    def body(t, carry):
        rb = 2 * t + 1
        scores(rb, 1)
        output(rb - 1, 0)
        scores(rb + 1, 0)
        output(rb, 1)
        return carry

    lax.fori_loop(0, n_blocks // 2 - 1, body, 0)
    scores(n_blocks - 1, 1)
    output(n_blocks - 2, 0)
    output(n_blocks - 1, 1)


def _na(q, k, vt, tab):
    B, H, S, Dh = q.shape
    hs = NA_HEADS_PER_STEP
    return pl.pallas_call(
        _na_kernel,
        out_shape=jax.ShapeDtypeStruct((B, S, H * Dh), BF16),
        grid=(H // hs, B),
        in_specs=[
            pl.BlockSpec((1, hs, S, Dh), lambda h, b: (b, h, 0, 0)),
            pl.BlockSpec((1, hs, S, Dh), lambda h, b: (b, h, 0, 0)),
            pl.BlockSpec((1, hs, Dh, S), lambda h, b: (b, h, 0, 0)),
            pl.BlockSpec((hs, 3, NA_KB, NA_QB), lambda h, b: (h, 0, 0, 0),
                         pipeline_mode=pl.Buffered(1)),
        ],
        out_specs=pl.BlockSpec((1, S, hs * Dh), lambda h, b: (b, 0, h)),
        scratch_shapes=[pltpu.VMEM((hs, 2, NA_KB, NA_QB), F32), pltpu.VMEM((hs, 2, 1, NA_QB), F32)],
        compiler_params=_params("arbitrary", "arbitrary"),
        name="na",
    )(q, k, vt, tab)


def _mla_kernel(q_ref, k_ref, vt_ref, o_ref, st_ref):
    n_chunks = k_ref.shape[2] // MLA_KC
    n_tiles = q_ref.shape[2] // MLA_TQ

    def pass1(j, slot):
        q = q_ref[0, 0, j * MLA_TQ:(j + 1) * MLA_TQ, :]
        m = None
        for c in range(n_chunks):
            ks = slice(c * MLA_KC, (c + 1) * MLA_KC)
            st = _dot_nt(k_ref[0, 0, ks, :], q)
            st_ref[slot, ks, :] = st
            mc = jnp.max(st, axis=0, keepdims=True)
            m = mc if c == 0 else jnp.maximum(m, mc)
        return m

    def pass2(j, slot, m):
        l = acc = None
        for c in range(n_chunks):
            ks = slice(c * MLA_KC, (c + 1) * MLA_KC)
            p = jnp.exp2(st_ref[slot, ks, :] - m)
            ps = jnp.sum(p, axis=0, keepdims=True)
            pv = _dot(vt_ref[0, 0, :, ks], p.astype(BF16))
            l = ps if c == 0 else l + ps
            acc = pv if c == 0 else acc + pv
        o_ref[0, j * MLA_TQ:(j + 1) * MLA_TQ, :] = (acc / l).T.astype(o_ref.dtype)

    m_prev = pass1(0, 0)
    for j in range(1, n_tiles):
        m_cur = pass1(j, j % 2)
        pass2(j - 1, (j - 1) % 2, m_prev)
        m_prev = m_cur
    pass2(n_tiles - 1, (n_tiles - 1) % 2, m_prev)


def _mla(q, k, vt):
    B, H, S, Dq = q.shape
    Dv = MLA_V_DIM
    tq = MLA_TQ * MLA_TILES
    return pl.pallas_call(
        _mla_kernel,
        out_shape=jax.ShapeDtypeStruct((B, S, H * Dv), BF16),
        grid=(B, H, S // tq),
        in_specs=[
            pl.BlockSpec((1, 1, tq, Dq), lambda b, h, i: (b, h, i, 0)),
            pl.BlockSpec((1, 1, S, Dq), lambda b, h, i: (b, h, 0, 0)),
            pl.BlockSpec((1, 1, Dv, S), lambda b, h, i: (b, h, 0, 0)),
        ],
        out_specs=pl.BlockSpec((1, tq, Dv), lambda b, h, i: (b, i, h)),
        scratch_shapes=[pltpu.VMEM((2, S, MLA_TQ), F32)],
        compiler_params=_params("arbitrary", "arbitrary", "arbitrary"),
        name="mla",
    )(q, k, vt)


def _layer_norm(z, g, b):
    mu = jnp.mean(z, axis=-1, keepdims=True)
    zc = z - mu
    var = jnp.mean(zc * zc, axis=-1, keepdims=True)
    return zc * lax.rsqrt(var + LN_EPS) * g + b


def _first_lane_where(cond, lane):
    return jnp.min(jnp.where(cond, lane, LANES), axis=-1, keepdims=True)


def _mix_kernel(ona_ref, omla_ref, x_ref, wo_ref, g_ref, b_ref, wr_ref, br_ref,
                x1_ref, ri_ref, gate_ref, cnt_ref, carry_ref):
    step = pl.program_id(0)

    @pl.when(step == 0)
    def _():
        carry_ref[...] = jnp.zeros_like(carry_ref)

    ts = MIX_SUB
    subs = [slice(j * ts, (j + 1) * ts) for j in range(x_ref.shape[0] // ts)]
    lane = lax.broadcasted_iota(jnp.int32, (ts, LANES), 1)
    g_mask = (lane >= N_EXPERTS) & (lane < N_EXPERTS + N_GROUPS)
    r_i = lax.broadcasted_iota(jnp.int32, (ts, ts), 0)
    c_i = lax.broadcasted_iota(jnp.int32, (ts, ts), 1)
    tri = jnp.where(c_i < r_i, 1.0, 0.0).astype(BF16)

    def project(rows):
        return (_dot(ona_ref[rows, :], wo_ref[0:NA_WIDTH, :])
                + _dot(omla_ref[rows, :], wo_ref[NA_WIDTH:, :]))

    mix = project(subs[0])
    for j, rows in enumerate(subs):
        nxt = project(subs[j + 1]) if j + 1 < len(subs) else None
        _norm_route(rows, mix, lane, g_mask, tri, x_ref, g_ref, b_ref, wr_ref, br_ref,
                    x1_ref, ri_ref, gate_ref, carry_ref)
        mix = nxt
    cnt_ref[...] = carry_ref[...]


def _norm_route(rows, mix, lane, g_mask, tri, x_ref, g_ref, b_ref, wr_ref, br_ref,
                x1_ref, ri_ref, gate_ref, carry_ref):
    x1 = _layer_norm(ALPHA * x_ref[rows, :] + mix, g_ref[...], b_ref[...])
    x1_ref[rows, :] = x1

    logit = _dot(x1.astype(BF16), wr_ref[...]) + br_ref[...]
    gl = jnp.where(g_mask, logit, NEG_BIG)
    gmax = jnp.max(gl, axis=-1, keepdims=True)
    gsum = jnp.sum(jnp.where(g_mask, jnp.exp(gl - gmax), 0.0), axis=-1, keepdims=True)
    g_val = 1.0 / gsum
    g_idx = _first_lane_where(gl == gmax, lane) - N_EXPERTS

    e_mask = (lane < N_EXPERTS) & ((lane // EXPERTS_PER_GROUP) == g_idx)
    el = jnp.where(e_mask, logit, NEG_BIG)
    emax = jnp.max(el, axis=-1, keepdims=True)
    esum = jnp.sum(jnp.where(e_mask, jnp.exp(el - emax), 0.0), axis=-1, keepdims=True)
    i1 = _first_lane_where(el == emax, lane)
    el2 = jnp.where(lane == i1, NEG_BIG, el)
    emax2 = jnp.max(el2, axis=-1, keepdims=True)
    i2 = _first_lane_where(el2 == emax2, lane)
    v1 = 1.0 / esum
    v2 = jnp.exp(emax2 - emax) / esum
    vsum = v1 + v2
    w1 = g_val * (v1 / vsum)
    w2 = g_val * (v2 / vsum)

    oh1 = lane == i1
    oh2 = lane == i2
    oh = jnp.where(oh1 | oh2, 1.0, 0.0)
    before = _dot(tri, oh.astype(BF16)) + carry_ref[...]
    rank1 = jnp.sum(jnp.where(oh1, before, 0.0), axis=-1, keepdims=True).astype(jnp.int32)
    rank2 = jnp.sum(jnp.where(oh2, before, 0.0), axis=-1, keepdims=True).astype(jnp.int32)
    carry_ref[...] = carry_ref[...] + jnp.sum(oh, axis=0, keepdims=True)

    ri = jnp.where(lane == 0, i1, jnp.where(lane == 1, i2,
                   jnp.where(lane == 2, rank1, jnp.where(lane == 3, rank2, 0))))
    ri_ref[:, rows] = ri.T[0:SUBLANES, :]
    gate_ref[rows, :] = jnp.where(lane == 0, w1, jnp.where(lane == 1, w2, 0.0))


def _mix_out(o_na, o_mla, x, wo, g, b, wr, br):
    N, D = x.shape
    tm = MIX_TM
    row = lambda i: (i, 0)
    return pl.pallas_call(
        _mix_kernel,
        out_shape=(
            jax.ShapeDtypeStruct((N, D), F32),
            jax.ShapeDtypeStruct((SUBLANES, N), jnp.int32),
            jax.ShapeDtypeStruct((N, LANES), F32),
            jax.ShapeDtypeStruct((1, LANES), F32),
        ),
        grid=(N // tm,),
        in_specs=[
            pl.BlockSpec((tm, NA_WIDTH), row),
            pl.BlockSpec((tm, NA_WIDTH), row),
            pl.BlockSpec((tm, D), row),
            _resident(wo.shape), _resident(g.shape), _resident(b.shape),
            _resident(wr.shape), _resident(br.shape),
        ],
        out_specs=(
            pl.BlockSpec((tm, D), row),
            pl.BlockSpec((SUBLANES, tm), lambda i: (0, i)),
            pl.BlockSpec((tm, LANES), row),
            pl.BlockSpec((1, LANES), lambda i: (0, 0)),
        ),
        scratch_shapes=[pltpu.VMEM((1, LANES), F32)],
        compiler_params=_params("arbitrary"),
        name="mix_out",
    )(o_na, o_mla, x, wo, g, b, wr, br)


def _pos_kernel(start_ref, ri_ref, pos_ref):
    eid = ri_ref[0:TOP_K_EXPERT, :]
    pos = ri_ref[TOP_K_EXPERT:2 * TOP_K_EXPERT, :]
    for e in range(N_EXPERTS):
        pos = pos + jnp.where(eid == e, start_ref[e], 0)
    pos_ref[...] = pos


def _positions(pad_starts, ri):
    n = ri.shape[1]
    return pl.pallas_call(
        _pos_kernel,
        out_shape=jax.ShapeDtypeStruct((TOP_K_EXPERT, n), jnp.int32),
        grid_spec=pltpu.PrefetchScalarGridSpec(
            num_scalar_prefetch=1,
            grid=(1,),
            in_specs=[pl.BlockSpec(ri.shape, lambda i, s: (0, 0))],
            out_specs=pl.BlockSpec((TOP_K_EXPERT, n), lambda i, s: (0, 0)),
        ),
        compiler_params=_params("arbitrary"),
        name="positions",
    )(pad_starts, ri)


def _row_copy(src, src_row, dst, dst_row, sem):
    return pltpu.make_async_copy(src.at[pl.ds(src_row, 1), :], dst.at[pl.ds(dst_row, 1), :], sem)


_PAD_CHUNKS = tuple(1 << b for b in reversed(range(MOE_TB.bit_length() - 1)))
ROW_CHUNKS = D_MODEL // LANES


def _dispatch_kernel(pos_ref, fill_ref, x_ref, xs_ref, rows_ref, zero_ref, sem, zsem):
    tm = x_ref.shape[0]
    rc = ROW_CHUNKS
    n_tok = pl.num_programs(0) * tm
    base = pl.program_id(0) * tm

    def token_rows(ref, tok, n):
        return ref.at[pl.ds(pl.multiple_of(tok * rc, rc), n * rc), :]

    def pad_copies(fn):
        def per_expert(e, carry):
            off = fill_ref[e]
            n = fill_ref[N_EXPERTS + e]
            for c in _PAD_CHUNKS:
                hit = (n & c) != 0

                @pl.when(hit)
                def _(off=off, c=c):
                    fn(pltpu.make_async_copy(token_rows(zero_ref, 0, c), token_rows(xs_ref, off, c),
                                             zsem))
                off = off + jnp.where(hit, c, 0)
            return carry
        lax.fori_loop(0, N_EXPERTS, per_expert, 0)

        ztok = zero_ref.shape[0] // rc

        def tail(t, carry):
            fn(pltpu.make_async_copy(zero_ref, token_rows(xs_ref, fill_ref[2 * N_EXPERTS] + t * ztok,
                                                           ztok), zsem))
            return carry
        lax.fori_loop(0, (xs_ref.shape[0] // rc - fill_ref[2 * N_EXPERTS]) // ztok, tail, 0)

    @pl.when(pl.program_id(0) == 0)
    def _():
        zero_ref[...] = jnp.zeros_like(zero_ref)
        pad_copies(lambda d: d.start())

    for c in range(rc):
        rows_ref[pl.ds(c, tm, stride=rc), :] = x_ref[:, c * LANES:(c + 1) * LANES]

    def issue(j, carry):
        for kk in range(TOP_K_EXPERT):
            pltpu.make_async_copy(token_rows(rows_ref, j, 1),
                                  token_rows(xs_ref, pos_ref[kk * n_tok + base + j], 1), sem).start()
        return carry

    lax.fori_loop(0, tm, issue, 0, unroll=8)

    @pl.when(pl.program_id(0) == 0)
    def _():
        pad_copies(lambda d: d.wait())

    for kk in range(TOP_K_EXPERT):
        pltpu.make_async_copy(rows_ref, token_rows(xs_ref, 0, tm), sem).wait()


def _dispatch(pos, fill, x1, n_pos):
    N, C = x1.shape
    tm = DISP_TM
    rc = ROW_CHUNKS
    return pl.pallas_call(
        _dispatch_kernel,
        out_shape=jax.ShapeDtypeStruct((n_pos * rc, LANES), x1.dtype),
        grid_spec=pltpu.PrefetchScalarGridSpec(
            num_scalar_prefetch=2,
            grid=(N // tm,),
            in_specs=[pl.BlockSpec((tm, C), lambda i, pos, fill: (i, 0))],
            out_specs=pl.BlockSpec(memory_space=pl.ANY),
            scratch_shapes=[pltpu.VMEM((tm * rc, LANES), x1.dtype),
                            pltpu.VMEM((_PAD_CHUNKS[0] * rc, LANES), x1.dtype),
                            pltpu.SemaphoreType.DMA, pltpu.SemaphoreType.DMA],
        ),
        compiler_params=_params("arbitrary"),
        name="dispatch",
    )(pos, fill, x1)


def _experts_kernel(be_ref, nxt_ref, nb_ref, xs_ref, wg_hbm, wu_hbm, wd_hbm, ys_ref,
                    wg_f, wu_f, wd_f, wg_s, wu_s, wd_s, sem):
    i = pl.program_id(0)
    active = i < nb_ref[0]
    e = be_ref[i]
    fresh = active & ((i == 0) | (e != be_ref[jnp.maximum(i - 1, 0)]))

    def fetch(ex):
        return (pltpu.make_async_copy(wg_hbm.at[ex], wg_f, sem.at[0]),
                pltpu.make_async_copy(wu_hbm.at[ex], wu_f, sem.at[1]),
                pltpu.make_async_copy(wd_hbm.at[ex], wd_f, sem.at[2]))

    @pl.when(i == 0)
    def _():
        for d in fetch(e):
            d.start()

    @pl.when(fresh)
    def _():
        for d in fetch(e):
            d.wait()
        wg_s[...] = wg_f[...].astype(BF16)
        wu_s[...] = wu_f[...].astype(BF16)
        wd_s[...] = wd_f[...].astype(BF16)
        nx = nxt_ref[i]

        @pl.when(nx >= 0)
        def _():
            for d in fetch(nx):
                d.start()

    @pl.when(active)
    def _():
        tb = xs_ref.shape[0] // ROW_CHUNKS
        xb = jnp.concatenate([xs_ref[pl.ds(c, tb, stride=ROW_CHUNKS), :]
                              for c in range(ROW_CHUNKS)], axis=1).astype(BF16)
        gp = _dot(xb, wg_s[...])
        up = _dot(xb, wu_s[...])
        hdn = (gp * jax.nn.sigmoid(gp) * up).astype(BF16)
        ys_ref[...] = _dot(hdn, wd_s[...])

    @pl.when(jnp.logical_not(active))
    def _():
        ys_ref[...] = jnp.zeros_like(ys_ref)


def _experts(block_e, next_e, n_used, xs, w_gate, w_up, w_down):
    E, D, F = w_gate.shape
    tb = MOE_TB
    rc = ROW_CHUNKS
    P, C = xs.shape[0] // rc, D
    n_blocks = P // tb
    hbm = pl.BlockSpec(memory_space=pl.ANY)
    return pl.pallas_call(
        _experts_kernel,
        out_shape=jax.ShapeDtypeStruct((P, C), F32),
        grid_spec=pltpu.PrefetchScalarGridSpec(
            num_scalar_prefetch=3,
            grid=(n_blocks,),
            in_specs=[
                pl.BlockSpec((tb * rc, LANES), lambda i, be, nx, nb: (jnp.minimum(i, nb[0] - 1), 0)),
                hbm, hbm, hbm,
            ],
            out_specs=pl.BlockSpec((tb, C), lambda i, be, nx, nb: (i, 0)),
            scratch_shapes=[pltpu.VMEM((D, F), F32), pltpu.VMEM((D, F), F32), pltpu.VMEM((F, D), F32),
                            pltpu.VMEM((D, F), BF16), pltpu.VMEM((D, F), BF16),
                            pltpu.VMEM((F, D), BF16), pltpu.SemaphoreType.DMA((3,))],
        ),
        compiler_params=_params("arbitrary"),
        name="experts",
    )(block_e, next_e, n_used, xs, w_gate, w_up, w_down)


def _final_kernel(pos_ref, x1_ref, gate_ref, ys_ref, p_ref, g_ref, b_ref, wpg_ref, wpe_ref,
                  gp_ref, o_ref, buf, sem):
    tm = x1_ref.shape[0]
    i = pl.program_id(0)
    last = pl.num_programs(0) - 1
    slot = i % 2

    def issue(tile, s, unroll):
        n_tok = pl.num_programs(0) * tm
        base = tile * tm

        def body(j, carry):
            for kk in range(TOP_K_EXPERT):
                _row_copy(ys_ref, pos_ref[kk * n_tok + base + j],
                          buf.at[s, kk], j, sem.at[s]).start()
            return carry
        lax.fori_loop(0, tm, body, 0, unroll=unroll)

    def drain(s):
        for kk in range(TOP_K_EXPERT):
            pltpu.make_async_copy(ys_ref.at[pl.ds(0, tm), :], buf.at[s, kk], sem.at[s]).wait()

    @pl.when(i == 0)
    def _():
        issue(0, 0, 8)

    drain(slot)
    issue(jnp.minimum(i + 1, last), 1 - slot, True)

    gates = gate_ref[...]
    g1 = gates[:, 0:1]
    g2 = gates[:, 1:2]
    ffn = buf[slot, 0] * g1 + buf[slot, 1] * g2
    x2 = _layer_norm(ALPHA * x1_ref[...] + ffn, g_ref[...], b_ref[...])

    gate = jax.nn.sigmoid(_dot(x2.astype(BF16), wpg_ref[...]))
    e = _dot(p_ref[...].astype(BF16), wpe_ref[...])
    t = e * gate
    ple = t * lax.rsqrt(jnp.mean(t * t, axis=-1, keepdims=True) + RMS_EPS) * gp_ref[...]
    o_ref[...] = x2 + ple

    @pl.when(i == last)
    def _():
        drain(1 - slot)


def _final(pos, x1, gates, ys, p, g, b, wpg, wpe, gp):
    N, D = x1.shape
    tm = FIN_TM
    row = lambda i, pos: (i, 0)
    return pl.pallas_call(
        _final_kernel,
        out_shape=jax.ShapeDtypeStruct((N, D), F32),
        grid_spec=pltpu.PrefetchScalarGridSpec(
            num_scalar_prefetch=1,
            grid=(N // tm,),
            in_specs=[
                pl.BlockSpec((tm, D), row),
                pl.BlockSpec((tm, LANES), row),
                pl.BlockSpec(memory_space=pl.ANY),
                pl.BlockSpec((tm, PLE_DIM), row),
                _resident(g.shape), _resident(b.shape), _resident(wpg.shape),
                _resident(wpe.shape), _resident(gp.shape),
            ],
            out_specs=pl.BlockSpec((tm, D), row),
            scratch_shapes=[pltpu.VMEM((2, TOP_K_EXPERT, tm, D), F32),
                            pltpu.SemaphoreType.DMA((2,))],
        ),
        compiler_params=_params("arbitrary"),
        name="final",
    )(pos, x1, gates, ys, p, g, b, wpg, wpe, gp)


def _rope_table(positions):
    inv_freq = 1.0 / (ROPE_THETA ** (jnp.arange(0, MLA_ROPE_DIM, 2, dtype=F32) / MLA_ROPE_DIM))
    half = MLA_ROPE_DIM // 2
    phase = jnp.concatenate([jnp.zeros((2 * half,), F32), jnp.full((2 * half,), np.pi / 2, F32)])
    return jnp.cos(positions.astype(F32)[..., None] * jnp.tile(inv_freq, 4) - phase)


def _layer(x, p_i, rope_t, w_in, rpb, q_norm_g, kv_norm_g, w_uq, w_uk, w_uv, w_o, ln1_g, ln1_b,
           w_group, b_group, w_router, b_router, w_gate, w_up, w_down, ln2_g, ln2_b,
           w_ple, w_ple_gate, ple_norm_g):
    B, S, D = x.shape
    N = B * S
    s3 = 3 * NA_WIDTH
    half = MLA_ROPE_DIM // 2

    wqk = w_in[:, :2 * NA_WIDTH].astype(BF16)
    wc = w_in[:, s3:].astype(BF16)
    wvt = w_in[:, 2 * NA_WIDTH:s3].T.astype(BF16)
    uq = w_uq.reshape(Q_LORA_RANK, MLA_HEADS, MLA_QK_DIM)
    nope, x1c, x2c = (uq[..., :MLA_NOPE_DIM], uq[..., MLA_NOPE_DIM:MLA_NOPE_DIM + half],
                      uq[..., MLA_NOPE_DIM + half:])
    wuq = jnp.concatenate([nope, x1c, x2c, x2c, x1c], axis=-1).reshape(Q_LORA_RANK, -1).astype(BF16)
    wuk = w_uk.astype(BF16)
    wuvt = w_uv.T.astype(BF16)
    row = lambda v: v.reshape(1, -1).astype(F32)

    q_na, k_na, v_nat, q_m, k_m, v_mt = _proj(x, rope_t, wqk, wvt, wc, wuq, wuk, wuvt,
                                               row(q_norm_g), row(kv_norm_g))
    o_na = _na(q_na, k_na, v_nat, _na_bias_table(rpb))
    o_mla = _mla(q_m, k_m, v_mt)

    pad = LANES - N_EXPERTS - N_GROUPS
    wr = jnp.concatenate([w_router, w_group, jnp.zeros((D, pad), F32)], axis=1).astype(BF16)
    br = jnp.concatenate([b_router.reshape(-1), b_group, jnp.zeros((pad,), F32)]).reshape(1, -1)
    x1, ri, gates, cnt = _mix_out(o_na.reshape(N, -1), o_mla.reshape(N, -1), x.reshape(N, D),
                                       w_o.astype(BF16), row(ln1_g), row(ln1_b), wr, br)

    tb = MOE_TB
    counts = cnt[0, :N_EXPERTS].astype(jnp.int32)
    padded = ((counts + tb - 1) // tb) * tb
    pad_ends = jnp.cumsum(padded)
    pad_starts = pad_ends - padded
    pos = _positions(pad_starts.astype(jnp.int32), ri).reshape(-1)
    fill = jnp.concatenate([pad_starts + counts, padded - counts, pad_ends[-1:]]).astype(jnp.int32)
    n_blocks = (N * TOP_K_EXPERT + N_EXPERTS * (tb - 1) + tb - 1) // tb
    blk_row = jnp.arange(n_blocks, dtype=jnp.int32) * tb
    block_e = jnp.minimum(jnp.sum(pad_ends[None, :] <= blk_row[:, None], axis=1),
                          N_EXPERTS - 1).astype(jnp.int32)
    n_used = (pad_ends[-1:] // tb).astype(jnp.int32)
    nxt_blk = pad_ends[block_e] // tb
    next_e = jnp.where(nxt_blk < n_used[0], block_e[jnp.minimum(nxt_blk, n_blocks - 1)],
                       -1).astype(jnp.int32)

    xs = _dispatch(pos, fill, x1, n_blocks * tb)
    ys = _experts(block_e, next_e, n_used, xs, w_gate, w_up, w_down)
    out = _final(pos, x1, gates, ys, p_i.reshape(N, -1), row(ln2_g), row(ln2_b),
                 w_ple_gate.astype(BF16), w_ple.astype(BF16), row(ple_norm_g))
    return out.reshape(B, S, D)


def kernel(x, p, positions, w_in, rpb, q_norm_g, kv_norm_g, w_uq, w_uk, w_uv, w_o, ln1_g, ln1_b,
           w_group, b_group, w_router, b_router, w_gate, w_up, w_down, ln2_g, ln2_b,
           w_ple, w_ple_gate, ple_norm_g):
    rope_t = _rope_table(positions)
    for i in range(DEPTH):
        x = _layer(x, p[i], rope_t, w_in[i], rpb[i], q_norm_g[i], kv_norm_g[i], w_uq[i], w_uk[i],
                   w_uv[i], w_o[i], ln1_g[i], ln1_b[i], w_group[i], b_group[i], w_router[i],
                   b_router[i], w_gate[i], w_up[i], w_down[i], ln2_g[i], ln2_b[i],
                   w_ple[i], w_ple_gate[i], ple_norm_g[i])
    return x
```

```python
import numpy as np
import jax
import jax.numpy as jnp
from jax import lax
from jax.experimental import pallas as pl
from jax.experimental.pallas import tpu as pltpu

D_MODEL = 2048
DEPTH = 1
GRID_W = 64
PLE_DIM = 256
NA_HEADS = 8
NA_HEAD_DIM = 128
NA_WIN_H = 8
NA_WIN_W = 16
NA_WIDTH = NA_HEADS * NA_HEAD_DIM
MLA_HEADS = 8
MLA_NOPE_DIM = 128
MLA_ROPE_DIM = 64
MLA_V_DIM = 128
MLA_QK_DIM = MLA_NOPE_DIM + MLA_ROPE_DIM
Q_LORA_RANK = 512
KV_LORA_RANK = 512
ROPE_THETA = 10000.0
N_GROUPS = 8
EXPERTS_PER_GROUP = 8
N_EXPERTS = N_GROUPS * EXPERTS_PER_GROUP
TOP_K_EXPERT = 2
D_EXPERT = 512
ALPHA = (2 * DEPTH) ** 0.25
LN_EPS = 1e-5
RMS_EPS = 1e-6

LANES = 128
SUBLANES = 8
VMEM_LIMIT_BYTES = 56 * 1024 * 1024

PROJ_TM = 512
NA_ROWS = 4
NA_HEADS_PER_STEP = 4
NA_QB = NA_ROWS * GRID_W
NA_WIN_ROWS = NA_ROWS + NA_WIN_H
NA_KB = NA_WIN_ROWS * GRID_W
MLA_QK_PAD = MLA_NOPE_DIM + 2 * MLA_ROPE_DIM
MLA_TQ = 256
MLA_KC = 1024
MLA_TILES = 8
MIX_TM = 512
MIX_SUB = 256
MOE_TB = 256
DISP_TM = 256
FIN_TM = 512
NEG_BIG = -1e30
LOG2E = 1.4426950408889634

BF16 = jnp.bfloat16
F32 = jnp.float32

_NT = (((1,), (1,)), ((), ()))


def _dot(a, b):
    return jnp.dot(a, b, preferred_element_type=F32)


def _dot_nt(a, b):
    return lax.dot_general(a, b, _NT, preferred_element_type=F32)


def _params(*sem):
    return pltpu.CompilerParams(dimension_semantics=sem, vmem_limit_bytes=VMEM_LIMIT_BYTES)


def _resident(shape):
    nd = len(shape)
    return pl.BlockSpec(shape, lambda *_: (0,) * nd, pipeline_mode=pl.Buffered(1))


def _proj_kernel(x_ref, t_ref, wqk_ref, wvt_ref, wc_ref, wuq_ref, wuk_ref, wuvt_ref, gq_ref, gkv_ref,
                 qna_ref, kna_ref, vnat_ref, qm_ref, km_ref, vmt_ref):
    xb = x_ref[0].astype(BF16)
    na_scale = NA_HEAD_DIM ** -0.5 * LOG2E
    mla_scale = MLA_QK_DIM ** -0.5 * LOG2E
    q = _dot(xb, wqk_ref[:, 0:NA_WIDTH]) * na_scale
    k = _dot(xb, wqk_ref[:, NA_WIDTH:2 * NA_WIDTH])
    vt = _dot_nt(wvt_ref[...], xb)
    for h in range(NA_HEADS):
        sl = slice(h * NA_HEAD_DIM, (h + 1) * NA_HEAD_DIM)
        qna_ref[0, h] = q[:, sl].astype(BF16)
        kna_ref[0, h] = k[:, sl].astype(BF16)
        vnat_ref[0, h] = vt[sl, :].astype(BF16)

    half = MLA_ROPE_DIM // 2
    cq = _dot(xb, wc_ref[:, 0:Q_LORA_RANK])
    ckv = _dot(xb, wc_ref[:, Q_LORA_RANK:Q_LORA_RANK + KV_LORA_RANK])
    kr = _dot(xb, wc_ref[:, Q_LORA_RANK + KV_LORA_RANK:])
    kr4 = jnp.concatenate([kr, kr[:, half:], kr[:, :half]], axis=1)

    def rms(c, g):
        return c * lax.rsqrt(jnp.mean(c * c, axis=-1, keepdims=True) + RMS_EPS) * g

    cqn = rms(cq, gq_ref[...]).astype(BF16)
    ckvn = rms(ckv, gkv_ref[...]).astype(BF16)

    t = t_ref[0]
    e = kr4 * t
    lane = lax.broadcasted_iota(jnp.int32, (1, 2 * MLA_ROPE_DIM), 1)
    sign = jnp.where((lane // half) % 2 == 0, -1.0, 1.0).astype(F32)
    kpe = (e + sign * pltpu.roll(e, MLA_ROPE_DIM, 1)).astype(BF16)

    qf = _dot(cqn, wuq_ref[...])
    kn = _dot(ckvn, wuk_ref[...])
    vmt = _dot_nt(wuvt_ref[...], ckvn)
    ts = t * mla_scale
    nope, pad = MLA_NOPE_DIM, MLA_QK_PAD
    for h in range(MLA_HEADS):
        qm_ref[0, h, :, 0:nope] = (qf[:, h * pad:h * pad + nope] * mla_scale).astype(BF16)
        qm_ref[0, h, :, nope:pad] = (qf[:, h * pad + nope:(h + 1) * pad] * ts).astype(BF16)
        km_ref[0, h, :, 0:nope] = kn[:, h * nope:(h + 1) * nope].astype(BF16)
        km_ref[0, h, :, nope:pad] = kpe
        vmt_ref[0, h] = vmt[h * MLA_V_DIM:(h + 1) * MLA_V_DIM, :].astype(BF16)


def _proj(x, rope_t, wqk, wvt, wc, wuq, wuk, wuvt, gq, gkv):
    B, S, D = x.shape
    tm = PROJ_TM
    hm = lambda b, i: (b, 0, i, 0)
    hmt = lambda b, i: (b, 0, 0, i)
    out_shape = (
        jax.ShapeDtypeStruct((B, NA_HEADS, S, NA_HEAD_DIM), BF16),
        jax.ShapeDtypeStruct((B, NA_HEADS, S, NA_HEAD_DIM), BF16),
        jax.ShapeDtypeStruct((B, NA_HEADS, NA_HEAD_DIM, S), BF16),
        jax.ShapeDtypeStruct((B, MLA_HEADS, S, MLA_QK_PAD), BF16),
        jax.ShapeDtypeStruct((B, MLA_HEADS, S, MLA_QK_PAD), BF16),
        jax.ShapeDtypeStruct((B, MLA_HEADS, MLA_V_DIM, S), BF16),
    )
    return pl.pallas_call(
        _proj_kernel,
        out_shape=out_shape,
        grid=(B, S // tm),
        in_specs=[
            pl.BlockSpec((1, tm, D), lambda b, i: (b, i, 0)),
            pl.BlockSpec((1, tm, LANES), lambda b, i: (b, i, 0)),
            _resident(wqk.shape), _resident(wvt.shape), _resident(wc.shape), _resident(wuq.shape),
            _resident(wuk.shape), _resident(wuvt.shape), _resident(gq.shape), _resident(gkv.shape),
        ],
        out_specs=(
            pl.BlockSpec((1, NA_HEADS, tm, NA_HEAD_DIM), hm),
            pl.BlockSpec((1, NA_HEADS, tm, NA_HEAD_DIM), hm),
            pl.BlockSpec((1, NA_HEADS, NA_HEAD_DIM, tm), hmt),
            pl.BlockSpec((1, MLA_HEADS, tm, MLA_QK_PAD), hm),
            pl.BlockSpec((1, MLA_HEADS, tm, MLA_QK_PAD), hm),
            pl.BlockSpec((1, MLA_HEADS, MLA_V_DIM, tm), hmt),
        ),
        compiler_params=_params("arbitrary", "arbitrary"),
        name="proj",
    )(x, rope_t, wqk, wvt, wc, wuq, wuk, wuvt, gq, gkv)


def _na_bias_table(rpb):
    rows = GRID_W
    n_blocks = rows // NA_ROWS
    n_ro, n_co = 2 * NA_WIN_H - 1, 2 * NA_WIN_W - 1
    kr, qr = np.arange(NA_WIN_ROWS), np.arange(NA_ROWS)
    kc, qc = np.arange(GRID_W), np.arange(GRID_W)
    c0 = np.clip(qc - NA_WIN_W // 2, 0, GRID_W - NA_WIN_W)
    col_ok = (kc[:, None] >= c0[None, :]) & (kc[:, None] < c0[None, :] + NA_WIN_W)
    col_off = np.clip(kc[:, None] - qc[None, :] + (NA_WIN_W - 1), 0, n_co - 1)
    oh_c = np.zeros((n_co, GRID_W, GRID_W), np.float32)
    oh_c[col_off, kc[:, None], qc[None, :]] = 1.0
    oh_r = np.zeros((3, n_ro, NA_WIN_ROWS, NA_ROWS), np.float32)
    row_ok = np.zeros((3, NA_WIN_ROWS, NA_ROWS), bool)
    for c, rb in enumerate((0, 1, n_blocks - 1)):
        w0 = int(np.clip(NA_ROWS * rb - NA_WIN_H // 2, 0, rows - NA_WIN_ROWS))
        r = NA_ROWS * rb + qr
        r0 = np.clip(r - NA_WIN_H // 2, 0, rows - NA_WIN_H)
        krow = w0 + kr
        row_ok[c] = (krow[:, None] >= r0[None, :]) & (krow[:, None] < r0[None, :] + NA_WIN_H)
        row_off = np.clip(krow[:, None] - r[None, :] + (NA_WIN_H - 1), 0, n_ro - 1)
        oh_r[c, row_off, kr[:, None], qr[None, :]] = 1.0
    sel = np.zeros((NA_ROWS, n_co, GRID_W, NA_ROWS, GRID_W), np.float32)
    for r in range(NA_ROWS):
        sel[r, :, :, r, :] = oh_c
    sel = sel.reshape(NA_ROWS * n_co, GRID_W, NA_QB)
    ok = row_ok[:, :, None, :, None] & col_ok[None, None, :, None, :]
    mask = np.where(ok, 0.0, NEG_BIG).astype(np.float32).reshape(3, NA_WIN_ROWS, GRID_W, NA_QB)
    hi = lax.Precision.HIGHEST
    a = jnp.einsum('hrc,zrkq->hzkqc', rpb.astype(F32) * LOG2E, oh_r, precision=hi)
    a = a.reshape(rpb.shape[0], 3, NA_WIN_ROWS, NA_ROWS * n_co)
    tab = jnp.einsum('hzkj,jxn->hzkxn', a, sel, precision=hi) + mask[None]
    return tab.reshape(rpb.shape[0], 3, NA_KB, NA_QB)


def _na_kernel(q_ref, k_ref, vt_ref, tab_ref, o_ref, st_ref, m_ref):
    n_heads = q_ref.shape[1]
    n_blocks = q_ref.shape[2] // NA_QB
    rows = n_blocks * NA_ROWS
    dh = q_ref.shape[3]

    def window(rb):
        w0 = jnp.clip(NA_ROWS * rb - NA_WIN_H // 2, 0, rows - NA_WIN_ROWS)
        return pl.multiple_of(w0 * GRID_W, NA_QB)

    def scores(rb, slot):
        tok0 = window(rb)
        q0 = pl.multiple_of(rb * NA_QB, NA_QB)
        cls = jnp.where(rb == 0, 0, jnp.where(rb == n_blocks - 1, 2, 1))
        for h in range(n_heads):
            st = _dot_nt(k_ref[0, h, pl.ds(tok0, NA_KB), :],
                         q_ref[0, h, pl.ds(q0, NA_QB), :]) + tab_ref[h, cls]
            st_ref[h, slot] = st
            m_ref[h, slot] = jnp.max(st, axis=0, keepdims=True)

    def output(rb, slot):
        tok0 = window(rb)
        q0 = pl.multiple_of(rb * NA_QB, NA_QB)
        for h in range(n_heads):
            p = jnp.exp2(st_ref[h, slot] - m_ref[h, slot])
            l = jnp.sum(p, axis=0, keepdims=True)
            ot = _dot(vt_ref[0, h, :, pl.ds(tok0, NA_KB)], p.astype(BF16)) / l
            o_ref[0, pl.ds(q0, NA_QB), h * dh:(h + 1) * dh] = ot.T.astype(o_ref.dtype)

    scores(0, 0)

    def body(t, carry):
        rb = 2 * t + 1
        scores(rb, 1)
        output(rb - 1, 0)
        scores(rb + 1, 0)
        output(rb, 1)
        return carry

    lax.fori_loop(0, n_blocks // 2 - 1, body, 0)
    scores(n_blocks - 1, 1)
    output(n_blocks - 2, 0)
    output(n_blocks - 1, 1)


def _na(q, k, vt, tab):
    B, H, S, Dh = q.shape
    hs = NA_HEADS_PER_STEP
    return pl.pallas_call(
        _na_kernel,
        out_shape=jax.ShapeDtypeStruct((B, S, H * Dh), BF16),
        grid=(H // hs, B),
        in_specs=[
            pl.BlockSpec((1, hs, S, Dh), lambda h, b: (b, h, 0, 0)),
            pl.BlockSpec((1, hs, S, Dh), lambda h, b: (b, h, 0, 0)),
            pl.BlockSpec((1, hs, Dh, S), lambda h, b: (b, h, 0, 0)),
            pl.BlockSpec((hs, 3, NA_KB, NA_QB), lambda h, b: (h, 0, 0, 0),
                         pipeline_mode=pl.Buffered(1)),
        ],
        out_specs=pl.BlockSpec((1, S, hs * Dh), lambda h, b: (b, 0, h)),
        scratch_shapes=[pltpu.VMEM((hs, 2, NA_KB, NA_QB), F32), pltpu.VMEM((hs, 2, 1, NA_QB), F32)],
        compiler_params=_params("arbitrary", "arbitrary"),
        name="na",
    )(q, k, vt, tab)


def _mla_kernel(q_ref, k_ref, vt_ref, o_ref, st_ref):
    n_chunks = k_ref.shape[2] // MLA_KC
    n_tiles = q_ref.shape[2] // MLA_TQ

    def pass1(j, slot):
        q = q_ref[0, 0, j * MLA_TQ:(j + 1) * MLA_TQ, :]
        m = None
        for c in range(n_chunks):
            ks = slice(c * MLA_KC, (c + 1) * MLA_KC)
            st = _dot_nt(k_ref[0, 0, ks, :], q)
            st_ref[slot, ks, :] = st
            mc = jnp.max(st, axis=0, keepdims=True)
            m = mc if c == 0 else jnp.maximum(m, mc)
        return m

    def pass2(j, slot, m):
        l = acc = None
        for c in range(n_chunks):
            ks = slice(c * MLA_KC, (c + 1) * MLA_KC)
            p = jnp.exp2(st_ref[slot, ks, :] - m)
            ps = jnp.sum(p, axis=0, keepdims=True)
            pv = _dot(vt_ref[0, 0, :, ks], p.astype(BF16))
            l = ps if c == 0 else l + ps
            acc = pv if c == 0 else acc + pv
        o_ref[0, j * MLA_TQ:(j + 1) * MLA_TQ, :] = (acc / l).T.astype(o_ref.dtype)

    m_prev = pass1(0, 0)
    for j in range(1, n_tiles):
        m_cur = pass1(j, j % 2)
        pass2(j - 1, (j - 1) % 2, m_prev)
        m_prev = m_cur
    pass2(n_tiles - 1, (n_tiles - 1) % 2, m_prev)


def _mla(q, k, vt):
    B, H, S, Dq = q.shape
    Dv = MLA_V_DIM
    tq = MLA_TQ * MLA_TILES
    return pl.pallas_call(
        _mla_kernel,
        out_shape=jax.ShapeDtypeStruct((B, S, H * Dv), BF16),
        grid=(B, H, S // tq),
        in_specs=[
            pl.BlockSpec((1, 1, tq, Dq), lambda b, h, i: (b, h, i, 0)),
            pl.BlockSpec((1, 1, S, Dq), lambda b, h, i: (b, h, 0, 0)),
            pl.BlockSpec((1, 1, Dv, S), lambda b, h, i: (b, h, 0, 0)),
        ],
        out_specs=pl.BlockSpec((1, tq, Dv), lambda b, h, i: (b, i, h)),
        scratch_shapes=[pltpu.VMEM((2, S, MLA_TQ), F32)],
        compiler_params=_params("arbitrary", "arbitrary", "arbitrary"),
        name="mla",
    )(q, k, vt)


def _layer_norm(z, g, b):
    mu = jnp.mean(z, axis=-1, keepdims=True)
    zc = z - mu
    var = jnp.mean(zc * zc, axis=-1, keepdims=True)
    return zc * lax.rsqrt(var + LN_EPS) * g + b


def _first_lane_where(cond, lane):
    return jnp.min(jnp.where(cond, lane, LANES), axis=-1, keepdims=True)


def _mix_kernel(ona_ref, omla_ref, x_ref, wo_ref, g_ref, b_ref, wr_ref, br_ref,
                x1_ref, ri_ref, gate_ref, cnt_ref, carry_ref):
    step = pl.program_id(0)

    @pl.when(step == 0)
    def _():
        carry_ref[...] = jnp.zeros_like(carry_ref)

    ts = MIX_SUB
    subs = [slice(j * ts, (j + 1) * ts) for j in range(x_ref.shape[0] // ts)]
    lane = lax.broadcasted_iota(jnp.int32, (ts, LANES), 1)
    g_mask = (lane >= N_EXPERTS) & (lane < N_EXPERTS + N_GROUPS)
    r_i = lax.broadcasted_iota(jnp.int32, (ts, ts), 0)
    c_i = lax.broadcasted_iota(jnp.int32, (ts, ts), 1)
    tri = jnp.where(c_i < r_i, 1.0, 0.0).astype(BF16)

    def project(rows):
        return (_dot(ona_ref[rows, :], wo_ref[0:NA_WIDTH, :])
                + _dot(omla_ref[rows, :], wo_ref[NA_WIDTH:, :]))

    mix = project(subs[0])
    for j, rows in enumerate(subs):
        nxt = project(subs[j + 1]) if j + 1 < len(subs) else None
        _norm_route(rows, mix, lane, g_mask, tri, x_ref, g_ref, b_ref, wr_ref, br_ref,
                    x1_ref, ri_ref, gate_ref, carry_ref)
        mix = nxt
    cnt_ref[...] = carry_ref[...]


def _norm_route(rows, mix, lane, g_mask, tri, x_ref, g_ref, b_ref, wr_ref, br_ref,
                x1_ref, ri_ref, gate_ref, carry_ref):
    x1 = _layer_norm(ALPHA * x_ref[rows, :] + mix, g_ref[...], b_ref[...])
    x1_ref[rows, :] = x1

    logit = _dot(x1.astype(BF16), wr_ref[...]) + br_ref[...]
    gl = jnp.where(g_mask, logit, NEG_BIG)
    gmax = jnp.max(gl, axis=-1, keepdims=True)
    gsum = jnp.sum(jnp.where(g_mask, jnp.exp(gl - gmax), 0.0), axis=-1, keepdims=True)
    g_val = 1.0 / gsum
    g_idx = _first_lane_where(gl == gmax, lane) - N_EXPERTS

    e_mask = (lane < N_EXPERTS) & ((lane // EXPERTS_PER_GROUP) == g_idx)
    el = jnp.where(e_mask, logit, NEG_BIG)
    emax = jnp.max(el, axis=-1, keepdims=True)
    esum = jnp.sum(jnp.where(e_mask, jnp.exp(el - emax), 0.0), axis=-1, keepdims=True)
    i1 = _first_lane_where(el == emax, lane)
    el2 = jnp.where(lane == i1, NEG_BIG, el)
    emax2 = jnp.max(el2, axis=-1, keepdims=True)
    i2 = _first_lane_where(el2 == emax2, lane)
    v1 = 1.0 / esum
    v2 = jnp.exp(emax2 - emax) / esum
    vsum = v1 + v2
    w1 = g_val * (v1 / vsum)
    w2 = g_val * (v2 / vsum)

    oh1 = lane == i1
    oh2 = lane == i2
    oh = jnp.where(oh1 | oh2, 1.0, 0.0)
    before = _dot(tri, oh.astype(BF16)) + carry_ref[...]
    rank1 = jnp.sum(jnp.where(oh1, before, 0.0), axis=-1, keepdims=True).astype(jnp.int32)
    rank2 = jnp.sum(jnp.where(oh2, before, 0.0), axis=-1, keepdims=True).astype(jnp.int32)
    carry_ref[...] = carry_ref[...] + jnp.sum(oh, axis=0, keepdims=True)

    ri = jnp.where(lane == 0, i1, jnp.where(lane == 1, i2,
                   jnp.where(lane == 2, rank1, jnp.where(lane == 3, rank2, 0))))
    ri_ref[:, rows] = ri.T[0:SUBLANES, :]
    gate_ref[rows, :] = jnp.where(lane == 0, w1, jnp.where(lane == 1, w2, 0.0))


def _mix_out(o_na, o_mla, x, wo, g, b, wr, br):
    N, D = x.shape
    tm = MIX_TM
    row = lambda i: (i, 0)
    return pl.pallas_call(
        _mix_kernel,
        out_shape=(
            jax.ShapeDtypeStruct((N, D), F32),
            jax.ShapeDtypeStruct((SUBLANES, N), jnp.int32),
            jax.ShapeDtypeStruct((N, LANES), F32),
            jax.ShapeDtypeStruct((1, LANES), F32),
        ),
        grid=(N // tm,),
        in_specs=[
            pl.BlockSpec((tm, NA_WIDTH), row),
            pl.BlockSpec((tm, NA_WIDTH), row),
            pl.BlockSpec((tm, D), row),
            _resident(wo.shape), _resident(g.shape), _resident(b.shape),
            _resident(wr.shape), _resident(br.shape),
        ],
        out_specs=(
            pl.BlockSpec((tm, D), row),
            pl.BlockSpec((SUBLANES, tm), lambda i: (0, i)),
            pl.BlockSpec((tm, LANES), row),
            pl.BlockSpec((1, LANES), lambda i: (0, 0)),
        ),
        scratch_shapes=[pltpu.VMEM((1, LANES), F32)],
        compiler_params=_params("arbitrary"),
        name="mix_out",
    )(o_na, o_mla, x, wo, g, b, wr, br)


def _pos_kernel(start_ref, ri_ref, pos_ref):
    eid = ri_ref[0:TOP_K_EXPERT, :]
    pos = ri_ref[TOP_K_EXPERT:2 * TOP_K_EXPERT, :]
    for e in range(N_EXPERTS):
        pos = pos + jnp.where(eid == e, start_ref[e], 0)
    pos_ref[...] = pos


def _positions(pad_starts, ri):
    n = ri.shape[1]
    return pl.pallas_call(
        _pos_kernel,
        out_shape=jax.ShapeDtypeStruct((TOP_K_EXPERT, n), jnp.int32),
        grid_spec=pltpu.PrefetchScalarGridSpec(
            num_scalar_prefetch=1,
            grid=(1,),
            in_specs=[pl.BlockSpec(ri.shape, lambda i, s: (0, 0))],
            out_specs=pl.BlockSpec((TOP_K_EXPERT, n), lambda i, s: (0, 0)),
        ),
        compiler_params=_params("arbitrary"),
        name="positions",
    )(pad_starts, ri)


def _row_copy(src, src_row, dst, dst_row, sem):
    return pltpu.make_async_copy(src.at[pl.ds(src_row, 1), :], dst.at[pl.ds(dst_row, 1), :], sem)


_PAD_CHUNKS = tuple(1 << b for b in reversed(range(3, MOE_TB.bit_length() - 1)))


def _dispatch_kernel(pos_ref, fill_ref, x_ref, xs_ref, zero_ref, sem, zsem):
    tm = x_ref.shape[0]
    n_tok = pl.num_programs(0) * tm
    base = pl.program_id(0) * tm

    def pad_copies(fn):
        def per_expert(e, carry):
            start = fill_ref[e]
            n = fill_ref[N_EXPERTS + e]
            head = (-start) & (SUBLANES - 1)
            for r in range(SUBLANES - 1):
                @pl.when(r < head)
                def _(r=r):
                    fn(_row_copy(zero_ref, 0, xs_ref, start + r, zsem))
            off = start + head
            rem = n - head
            for c in _PAD_CHUNKS:
                hit = (rem & c) != 0

                @pl.when(hit)
                def _(off=off, c=c):
                    fn(pltpu.make_async_copy(zero_ref.at[pl.ds(0, c), :],
                                             xs_ref.at[pl.ds(pl.multiple_of(off, SUBLANES), c), :],
                                             zsem))
                off = off + jnp.where(hit, c, 0)
            return carry
        lax.fori_loop(0, N_EXPERTS, per_expert, 0)

        zrows = zero_ref.shape[0]

        def tail(t, carry):
            row = pl.multiple_of(fill_ref[2 * N_EXPERTS] + t * zrows, zrows)
            fn(pltpu.make_async_copy(zero_ref, xs_ref.at[pl.ds(row, zrows), :], zsem))
            return carry
        lax.fori_loop(0, (xs_ref.shape[0] - fill_ref[2 * N_EXPERTS]) // zrows, tail, 0)

    @pl.when(pl.program_id(0) == 0)
    def _():
        zero_ref[...] = jnp.zeros_like(zero_ref)
        pad_copies(lambda d: d.start())

    def issue(j, carry):
        for kk in range(TOP_K_EXPERT):
            _row_copy(x_ref, j, xs_ref, pos_ref[kk * n_tok + base + j], sem).start()
        return carry

    lax.fori_loop(0, tm, issue, 0, unroll=8)

    @pl.when(pl.program_id(0) == 0)
    def _():
        pad_copies(lambda d: d.wait())

    for kk in range(TOP_K_EXPERT):
        pltpu.make_async_copy(x_ref, xs_ref.at[pl.ds(0, tm), :], sem).wait()


def _dispatch(pos, fill, x1, n_rows):
    N, C = x1.shape
    tm = DISP_TM
    return pl.pallas_call(
        _dispatch_kernel,
        out_shape=jax.ShapeDtypeStruct((n_rows, C), x1.dtype),
        grid_spec=pltpu.PrefetchScalarGridSpec(
            num_scalar_prefetch=2,
            grid=(N // tm,),
            in_specs=[pl.BlockSpec((tm, C), lambda i, pos, fill: (i, 0))],
            out_specs=pl.BlockSpec(memory_space=pl.ANY),
            scratch_shapes=[pltpu.VMEM((_PAD_CHUNKS[0], C), x1.dtype),
                            pltpu.SemaphoreType.DMA, pltpu.SemaphoreType.DMA],
        ),
        compiler_params=_params("arbitrary"),
        name="dispatch",
    )(pos, fill, x1)


def _experts_kernel(be_ref, nxt_ref, nb_ref, xs_ref, wg_hbm, wu_hbm, wd_hbm, ys_ref,
                    wg_f, wu_f, wd_f, wg_s, wu_s, wd_s, sem):
    i = pl.program_id(0)
    active = i < nb_ref[0]
    e = be_ref[i]
    fresh = active & ((i == 0) | (e != be_ref[jnp.maximum(i - 1, 0)]))

    def fetch(ex):
        return (pltpu.make_async_copy(wg_hbm.at[ex], wg_f, sem.at[0]),
                pltpu.make_async_copy(wu_hbm.at[ex], wu_f, sem.at[1]),
                pltpu.make_async_copy(wd_hbm.at[ex], wd_f, sem.at[2]))

    @pl.when(i == 0)
    def _():
        for d in fetch(e):
            d.start()

    @pl.when(fresh)
    def _():
        for d in fetch(e):
            d.wait()
        wg_s[...] = wg_f[...].astype(BF16)
        wu_s[...] = wu_f[...].astype(BF16)
        wd_s[...] = wd_f[...].astype(BF16)
        nx = nxt_ref[i]

        @pl.when(nx >= 0)
        def _():
            for d in fetch(nx):
                d.start()

    @pl.when(active)
    def _():
        xb = xs_ref[...].astype(BF16)
        gp = _dot(xb, wg_s[...])
        up = _dot(xb, wu_s[...])
        hdn = (gp * jax.nn.sigmoid(gp) * up).astype(BF16)
        ys_ref[...] = _dot(hdn, wd_s[...])

    @pl.when(jnp.logical_not(active))
    def _():
        ys_ref[...] = jnp.zeros_like(ys_ref)


def _experts(block_e, next_e, n_used, xs, w_gate, w_up, w_down):
    P, C = xs.shape
    E, D, F = w_gate.shape
    tb = MOE_TB
    n_blocks = P // tb
    hbm = pl.BlockSpec(memory_space=pl.ANY)
    return pl.pallas_call(
        _experts_kernel,
        out_shape=jax.ShapeDtypeStruct((P, C), F32),
        grid_spec=pltpu.PrefetchScalarGridSpec(
            num_scalar_prefetch=3,
            grid=(n_blocks,),
            in_specs=[
                pl.BlockSpec((tb, C), lambda i, be, nx, nb: (jnp.minimum(i, nb[0] - 1), 0)),
                hbm, hbm, hbm,
            ],
            out_specs=pl.BlockSpec((tb, C), lambda i, be, nx, nb: (i, 0)),
            scratch_shapes=[pltpu.VMEM((D, F), F32), pltpu.VMEM((D, F), F32), pltpu.VMEM((F, D), F32),
                            pltpu.VMEM((D, F), BF16), pltpu.VMEM((D, F), BF16),
                            pltpu.VMEM((F, D), BF16), pltpu.SemaphoreType.DMA((3,))],
        ),
        compiler_params=_params("arbitrary"),
        name="experts",
    )(block_e, next_e, n_used, xs, w_gate, w_up, w_down)


def _final_kernel(pos_ref, x1_ref, gate_ref, ys_ref, p_ref, g_ref, b_ref, wpg_ref, wpe_ref,
                  gp_ref, o_ref, buf, sem):
    tm = x1_ref.shape[0]
    i = pl.program_id(0)
    last = pl.num_programs(0) - 1
    slot = i % 2

    def issue(tile, s, unroll):
        n_tok = pl.num_programs(0) * tm
        base = tile * tm

        def body(j, carry):
            for kk in range(TOP_K_EXPERT):
                _row_copy(ys_ref, pos_ref[kk * n_tok + base + j],
                          buf.at[s, kk], j, sem.at[s]).start()
            return carry
        lax.fori_loop(0, tm, body, 0, unroll=unroll)

    def drain(s):
        for kk in range(TOP_K_EXPERT):
            pltpu.make_async_copy(ys_ref.at[pl.ds(0, tm), :], buf.at[s, kk], sem.at[s]).wait()

    @pl.when(i == 0)
    def _():
        issue(0, 0, 8)

    drain(slot)
    issue(jnp.minimum(i + 1, last), 1 - slot, True)

    gates = gate_ref[...]
    g1 = gates[:, 0:1]
    g2 = gates[:, 1:2]
    ffn = buf[slot, 0] * g1 + buf[slot, 1] * g2
    x2 = _layer_norm(ALPHA * x1_ref[...] + ffn, g_ref[...], b_ref[...])

    gate = jax.nn.sigmoid(_dot(x2.astype(BF16), wpg_ref[...]))
    e = _dot(p_ref[...].astype(BF16), wpe_ref[...])
    t = e * gate
    ple = t * lax.rsqrt(jnp.mean(t * t, axis=-1, keepdims=True) + RMS_EPS) * gp_ref[...]
    o_ref[...] = x2 + ple

    @pl.when(i == last)
    def _():
        drain(1 - slot)


def _final(pos, x1, gates, ys, p, g, b, wpg, wpe, gp):
    N, D = x1.shape
    tm = FIN_TM
    row = lambda i, pos: (i, 0)
    return pl.pallas_call(
        _final_kernel,
        out_shape=jax.ShapeDtypeStruct((N, D), F32),
        grid_spec=pltpu.PrefetchScalarGridSpec(
            num_scalar_prefetch=1,
            grid=(N // tm,),
            in_specs=[
                pl.BlockSpec((tm, D), row),
                pl.BlockSpec((tm, LANES), row),
                pl.BlockSpec(memory_space=pl.ANY),
                pl.BlockSpec((tm, PLE_DIM), row),
                _resident(g.shape), _resident(b.shape), _resident(wpg.shape),
                _resident(wpe.shape), _resident(gp.shape),
            ],
            out_specs=pl.BlockSpec((tm, D), row),
            scratch_shapes=[pltpu.VMEM((2, TOP_K_EXPERT, tm, D), F32),
                            pltpu.SemaphoreType.DMA((2,))],
        ),
        compiler_params=_params("arbitrary"),
        name="final",
    )(pos, x1, gates, ys, p, g, b, wpg, wpe, gp)


def _rope_table(positions):
    inv_freq = 1.0 / (ROPE_THETA ** (jnp.arange(0, MLA_ROPE_DIM, 2, dtype=F32) / MLA_ROPE_DIM))
    half = MLA_ROPE_DIM // 2
    phase = jnp.concatenate([jnp.zeros((2 * half,), F32), jnp.full((2 * half,), np.pi / 2, F32)])
    return jnp.cos(positions.astype(F32)[..., None] * jnp.tile(inv_freq, 4) - phase)


def _layer(x, p_i, rope_t, w_in, rpb, q_norm_g, kv_norm_g, w_uq, w_uk, w_uv, w_o, ln1_g, ln1_b,
           w_group, b_group, w_router, b_router, w_gate, w_up, w_down, ln2_g, ln2_b,
           w_ple, w_ple_gate, ple_norm_g):
    B, S, D = x.shape
    N = B * S
    s3 = 3 * NA_WIDTH
    half = MLA_ROPE_DIM // 2

    wqk = w_in[:, :2 * NA_WIDTH].astype(BF16)
    wc = w_in[:, s3:].astype(BF16)
    wvt = w_in[:, 2 * NA_WIDTH:s3].T.astype(BF16)
    uq = w_uq.reshape(Q_LORA_RANK, MLA_HEADS, MLA_QK_DIM)
    nope, x1c, x2c = (uq[..., :MLA_NOPE_DIM], uq[..., MLA_NOPE_DIM:MLA_NOPE_DIM + half],
                      uq[..., MLA_NOPE_DIM + half:])
    wuq = jnp.concatenate([nope, x1c, x2c, x2c, x1c], axis=-1).reshape(Q_LORA_RANK, -1).astype(BF16)
    wuk = w_uk.astype(BF16)
    wuvt = w_uv.T.astype(BF16)
    row = lambda v: v.reshape(1, -1).astype(F32)

    q_na, k_na, v_nat, q_m, k_m, v_mt = _proj(x, rope_t, wqk, wvt, wc, wuq, wuk, wuvt,
                                               row(q_norm_g), row(kv_norm_g))
    o_na = _na(q_na, k_na, v_nat, _na_bias_table(rpb))
    o_mla = _mla(q_m, k_m, v_mt)

    pad = LANES - N_EXPERTS - N_GROUPS
    wr = jnp.concatenate([w_router, w_group, jnp.zeros((D, pad), F32)], axis=1).astype(BF16)
    br = jnp.concatenate([b_router.reshape(-1), b_group, jnp.zeros((pad,), F32)]).reshape(1, -1)
    x1, ri, gates, cnt = _mix_out(o_na.reshape(N, -1), o_mla.reshape(N, -1), x.reshape(N, D),
                                       w_o.astype(BF16), row(ln1_g), row(ln1_b), wr, br)

    tb = MOE_TB
    counts = cnt[0, :N_EXPERTS].astype(jnp.int32)
    padded = ((counts + tb - 1) // tb) * tb
    pad_ends = jnp.cumsum(padded)
    pad_starts = pad_ends - padded
    pos = _positions(pad_starts.astype(jnp.int32), ri).reshape(-1)
    fill = jnp.concatenate([pad_starts + counts, padded - counts, pad_ends[-1:]]).astype(jnp.int32)
    n_blocks = (N * TOP_K_EXPERT + N_EXPERTS * (tb - 1) + tb - 1) // tb
    blk_row = jnp.arange(n_blocks, dtype=jnp.int32) * tb
    block_e = jnp.minimum(jnp.sum(pad_ends[None, :] <= blk_row[:, None], axis=1),
                          N_EXPERTS - 1).astype(jnp.int32)
    n_used = (pad_ends[-1:] // tb).astype(jnp.int32)
    nxt_blk = pad_ends[block_e] // tb
    next_e = jnp.where(nxt_blk < n_used[0], block_e[jnp.minimum(nxt_blk, n_blocks - 1)],
                       -1).astype(jnp.int32)

    xs = _dispatch(pos, fill, x1, n_blocks * tb)
    ys = _experts(block_e, next_e, n_used, xs, w_gate, w_up, w_down)
    out = _final(pos, x1, gates, ys, p_i.reshape(N, -1), row(ln2_g), row(ln2_b),
                 w_ple_gate.astype(BF16), w_ple.astype(BF16), row(ple_norm_g))
    return out.reshape(B, S, D)


def kernel(x, p, positions, w_in, rpb, q_norm_g, kv_norm_g, w_uq, w_uk, w_uv, w_o, ln1_g, ln1_b,
           w_group, b_group, w_router, b_router, w_gate, w_up, w_down, ln2_g, ln2_b,
           w_ple, w_ple_gate, ple_norm_g):
    rope_t = _rope_table(positions)
    for i in range(DEPTH):
        x = _layer(x, p[i], rope_t, w_in[i], rpb[i], q_norm_g[i], kv_norm_g[i], w_uq[i], w_uk[i],
                   w_uv[i], w_o[i], ln1_g[i], ln1_b[i], w_group[i], b_group[i], w_router[i],
                   b_router[i], w_gate[i], w_up[i], w_down[i], ln2_g[i], ln2_b[i],
                   w_ple[i], w_ple_gate[i], ple_norm_g[i])
    return x
```

```python
import numpy as np
import jax
import jax.numpy as jnp
from jax import lax
from jax.experimental import pallas as pl
from jax.experimental.pallas import tpu as pltpu

D_MODEL = 2048
DEPTH = 1
GRID_W = 64
PLE_DIM = 256
NA_HEADS = 8
NA_HEAD_DIM = 128
NA_WIN_H = 8
NA_WIN_W = 16
NA_WIDTH = NA_HEADS * NA_HEAD_DIM
MLA_HEADS = 8
MLA_NOPE_DIM = 128
MLA_ROPE_DIM = 64
MLA_V_DIM = 128
MLA_QK_DIM = MLA_NOPE_DIM + MLA_ROPE_DIM
Q_LORA_RANK = 512
KV_LORA_RANK = 512
ROPE_THETA = 10000.0
N_GROUPS = 8
EXPERTS_PER_GROUP = 8
N_EXPERTS = N_GROUPS * EXPERTS_PER_GROUP
TOP_K_EXPERT = 2
D_EXPERT = 512
ALPHA = (2 * DEPTH) ** 0.25
LN_EPS = 1e-5
RMS_EPS = 1e-6

LANES = 128
SUBLANES = 8
VMEM_LIMIT_BYTES = 56 * 1024 * 1024

PROJ_TM = 512
NA_ROWS = 4
NA_HEADS_PER_STEP = 4
NA_QB = NA_ROWS * GRID_W
NA_WIN_ROWS = NA_ROWS + NA_WIN_H
NA_KB = NA_WIN_ROWS * GRID_W
MLA_QK_PAD = MLA_NOPE_DIM + 2 * MLA_ROPE_DIM
MLA_TQ = 256
MLA_KC = 1024
MLA_TILES = 8
MIX_TM = 512
MIX_SUB = 256
MOE_TB = 256
DISP_TM = 256
FIN_TM = 256
NEG_BIG = -1e30
LOG2E = 1.4426950408889634

BF16 = jnp.bfloat16
F32 = jnp.float32

_NT = (((1,), (1,)), ((), ()))


def _dot(a, b):
    return jnp.dot(a, b, preferred_element_type=F32)


def _dot_nt(a, b):
    return lax.dot_general(a, b, _NT, preferred_element_type=F32)


def _params(*sem):
    return pltpu.CompilerParams(dimension_semantics=sem, vmem_limit_bytes=VMEM_LIMIT_BYTES)


def _resident(shape):
    nd = len(shape)
    return pl.BlockSpec(shape, lambda *_: (0,) * nd, pipeline_mode=pl.Buffered(1))


def _proj_kernel(x_ref, t_ref, wqk_ref, wvt_ref, wc_ref, wuq_ref, wuk_ref, wuvt_ref, gq_ref, gkv_ref,
                 qna_ref, kna_ref, vnat_ref, qm_ref, km_ref, vmt_ref):
    xb = x_ref[0].astype(BF16)
    na_scale = NA_HEAD_DIM ** -0.5 * LOG2E
    mla_scale = MLA_QK_DIM ** -0.5 * LOG2E
    q = _dot(xb, wqk_ref[:, 0:NA_WIDTH]) * na_scale
    k = _dot(xb, wqk_ref[:, NA_WIDTH:2 * NA_WIDTH])
    vt = _dot_nt(wvt_ref[...], xb)
    for h in range(NA_HEADS):
        sl = slice(h * NA_HEAD_DIM, (h + 1) * NA_HEAD_DIM)
        qna_ref[0, h] = q[:, sl].astype(BF16)
        kna_ref[0, h] = k[:, sl].astype(BF16)
        vnat_ref[0, h] = vt[sl, :].astype(BF16)

    half = MLA_ROPE_DIM // 2
    cq = _dot(xb, wc_ref[:, 0:Q_LORA_RANK])
    ckv = _dot(xb, wc_ref[:, Q_LORA_RANK:Q_LORA_RANK + KV_LORA_RANK])
    kr = _dot(xb, wc_ref[:, Q_LORA_RANK + KV_LORA_RANK:])
    kr4 = jnp.concatenate([kr, kr[:, half:], kr[:, :half]], axis=1)

    def rms(c, g):
        return c * lax.rsqrt(jnp.mean(c * c, axis=-1, keepdims=True) + RMS_EPS) * g

    cqn = rms(cq, gq_ref[...]).astype(BF16)
    ckvn = rms(ckv, gkv_ref[...]).astype(BF16)

    t = t_ref[0]
    e = kr4 * t
    lane = lax.broadcasted_iota(jnp.int32, (1, 2 * MLA_ROPE_DIM), 1)
    sign = jnp.where((lane // half) % 2 == 0, -1.0, 1.0).astype(F32)
    kpe = (e + sign * pltpu.roll(e, MLA_ROPE_DIM, 1)).astype(BF16)

    qf = _dot(cqn, wuq_ref[...])
    kn = _dot(ckvn, wuk_ref[...])
    vmt = _dot_nt(wuvt_ref[...], ckvn)
    ts = t * mla_scale
    nope, pad = MLA_NOPE_DIM, MLA_QK_PAD
    for h in range(MLA_HEADS):
        qm_ref[0, h, :, 0:nope] = (qf[:, h * pad:h * pad + nope] * mla_scale).astype(BF16)
        qm_ref[0, h, :, nope:pad] = (qf[:, h * pad + nope:(h + 1) * pad] * ts).astype(BF16)
        km_ref[0, h, :, 0:nope] = kn[:, h * nope:(h + 1) * nope].astype(BF16)
        km_ref[0, h, :, nope:pad] = kpe
        vmt_ref[0, h] = vmt[h * MLA_V_DIM:(h + 1) * MLA_V_DIM, :].astype(BF16)


def _proj(x, rope_t, wqk, wvt, wc, wuq, wuk, wuvt, gq, gkv):
    B, S, D = x.shape
    tm = PROJ_TM
    hm = lambda b, i: (b, 0, i, 0)
    hmt = lambda b, i: (b, 0, 0, i)
    out_shape = (
        jax.ShapeDtypeStruct((B, NA_HEADS, S, NA_HEAD_DIM), BF16),
        jax.ShapeDtypeStruct((B, NA_HEADS, S, NA_HEAD_DIM), BF16),
        jax.ShapeDtypeStruct((B, NA_HEADS, NA_HEAD_DIM, S), BF16),
        jax.ShapeDtypeStruct((B, MLA_HEADS, S, MLA_QK_PAD), BF16),
        jax.ShapeDtypeStruct((B, MLA_HEADS, S, MLA_QK_PAD), BF16),
        jax.ShapeDtypeStruct((B, MLA_HEADS, MLA_V_DIM, S), BF16),
    )
    return pl.pallas_call(
        _proj_kernel,
        out_shape=out_shape,
        grid=(B, S // tm),
        in_specs=[
            pl.BlockSpec((1, tm, D), lambda b, i: (b, i, 0)),
            pl.BlockSpec((1, tm, LANES), lambda b, i: (b, i, 0)),
            _resident(wqk.shape), _resident(wvt.shape), _resident(wc.shape), _resident(wuq.shape),
            _resident(wuk.shape), _resident(wuvt.shape), _resident(gq.shape), _resident(gkv.shape),
        ],
        out_specs=(
            pl.BlockSpec((1, NA_HEADS, tm, NA_HEAD_DIM), hm),
            pl.BlockSpec((1, NA_HEADS, tm, NA_HEAD_DIM), hm),
            pl.BlockSpec((1, NA_HEADS, NA_HEAD_DIM, tm), hmt),
            pl.BlockSpec((1, MLA_HEADS, tm, MLA_QK_PAD), hm),
            pl.BlockSpec((1, MLA_HEADS, tm, MLA_QK_PAD), hm),
            pl.BlockSpec((1, MLA_HEADS, MLA_V_DIM, tm), hmt),
        ),
        compiler_params=_params("arbitrary", "arbitrary"),
        name="proj",
    )(x, rope_t, wqk, wvt, wc, wuq, wuk, wuvt, gq, gkv)


def _na_bias_table(rpb):
    rows = GRID_W
    n_blocks = rows // NA_ROWS
    n_ro, n_co = 2 * NA_WIN_H - 1, 2 * NA_WIN_W - 1
    kr, qr = np.arange(NA_WIN_ROWS), np.arange(NA_ROWS)
    kc, qc = np.arange(GRID_W), np.arange(GRID_W)
    c0 = np.clip(qc - NA_WIN_W // 2, 0, GRID_W - NA_WIN_W)
    col_ok = (kc[:, None] >= c0[None, :]) & (kc[:, None] < c0[None, :] + NA_WIN_W)
    col_off = np.clip(kc[:, None] - qc[None, :] + (NA_WIN_W - 1), 0, n_co - 1)
    oh_c = np.zeros((n_co, GRID_W, GRID_W), np.float32)
    oh_c[col_off, kc[:, None], qc[None, :]] = 1.0
    oh_r = np.zeros((3, n_ro, NA_WIN_ROWS, NA_ROWS), np.float32)
    row_ok = np.zeros((3, NA_WIN_ROWS, NA_ROWS), bool)
    for c, rb in enumerate((0, 1, n_blocks - 1)):
        w0 = int(np.clip(NA_ROWS * rb - NA_WIN_H // 2, 0, rows - NA_WIN_ROWS))
        r = NA_ROWS * rb + qr
        r0 = np.clip(r - NA_WIN_H // 2, 0, rows - NA_WIN_H)
        krow = w0 + kr
        row_ok[c] = (krow[:, None] >= r0[None, :]) & (krow[:, None] < r0[None, :] + NA_WIN_H)
        row_off = np.clip(krow[:, None] - r[None, :] + (NA_WIN_H - 1), 0, n_ro - 1)
        oh_r[c, row_off, kr[:, None], qr[None, :]] = 1.0
    sel = np.zeros((NA_ROWS, n_co, GRID_W, NA_ROWS, GRID_W), np.float32)
    for r in range(NA_ROWS):
        sel[r, :, :, r, :] = oh_c
    sel = sel.reshape(NA_ROWS * n_co, GRID_W, NA_QB)
    ok = row_ok[:, :, None, :, None] & col_ok[None, None, :, None, :]
    mask = np.where(ok, 0.0, NEG_BIG).astype(np.float32).reshape(3, NA_WIN_ROWS, GRID_W, NA_QB)
    hi = lax.Precision.HIGHEST
    a = jnp.einsum('hrc,zrkq->hzkqc', rpb.astype(F32) * LOG2E, oh_r, precision=hi)
    a = a.reshape(rpb.shape[0], 3, NA_WIN_ROWS, NA_ROWS * n_co)
    tab = jnp.einsum('hzkj,jxn->hzkxn', a, sel, precision=hi) + mask[None]
    return tab.reshape(rpb.shape[0], 3, NA_KB, NA_QB)


def _na_kernel(q_ref, k_ref, vt_ref, tab_ref, o_ref, st_ref, m_ref):
    n_heads = q_ref.shape[1]
    n_blocks = q_ref.shape[2] // NA_QB
    rows = n_blocks * NA_ROWS
    dh = q_ref.shape[3]

    def window(rb):
        w0 = jnp.clip(NA_ROWS * rb - NA_WIN_H // 2, 0, rows - NA_WIN_ROWS)
        return pl.multiple_of(w0 * GRID_W, NA_QB)

    def scores(rb, slot):
        tok0 = window(rb)
        q0 = pl.multiple_of(rb * NA_QB, NA_QB)
        cls = jnp.where(rb == 0, 0, jnp.where(rb == n_blocks - 1, 2, 1))
        for h in range(n_heads):
            st = _dot_nt(k_ref[0, h, pl.ds(tok0, NA_KB), :],
                         q_ref[0, h, pl.ds(q0, NA_QB), :]) + tab_ref[h, cls]
            st_ref[h, slot] = st
            m_ref[h, slot] = jnp.max(st, axis=0, keepdims=True)

    def output(rb, slot):
        tok0 = window(rb)
        q0 = pl.multiple_of(rb * NA_QB, NA_QB)
        for h in range(n_heads):
            p = jnp.exp2(st_ref[h, slot] - m_ref[h, slot])
            l = jnp.sum(p, axis=0, keepdims=True)
            ot = _dot(vt_ref[0, h, :, pl.ds(tok0, NA_KB)], p.astype(BF16)) / l
            o_ref[0, pl.ds(q0, NA_QB), h * dh:(h + 1) * dh] = ot.T.astype(o_ref.dtype)

    scores(0, 0)

    def body(t, carry):
        rb = 2 * t + 1
        scores(rb, 1)
        output(rb - 1, 0)
        scores(rb + 1, 0)
        output(rb, 1)
        return carry

    lax.fori_loop(0, n_blocks // 2 - 1, body, 0)
    scores(n_blocks - 1, 1)
    output(n_blocks - 2, 0)
    output(n_blocks - 1, 1)


def _na(q, k, vt, tab):
    B, H, S, Dh = q.shape
    hs = NA_HEADS_PER_STEP
    return pl.pallas_call(
        _na_kernel,
        out_shape=jax.ShapeDtypeStruct((B, S, H * Dh), BF16),
        grid=(H // hs, B),
        in_specs=[
            pl.BlockSpec((1, hs, S, Dh), lambda h, b: (b, h, 0, 0)),
            pl.BlockSpec((1, hs, S, Dh), lambda h, b: (b, h, 0, 0)),
            pl.BlockSpec((1, hs, Dh, S), lambda h, b: (b, h, 0, 0)),
            pl.BlockSpec((hs, 3, NA_KB, NA_QB), lambda h, b: (h, 0, 0, 0),
                         pipeline_mode=pl.Buffered(1)),
        ],
        out_specs=pl.BlockSpec((1, S, hs * Dh), lambda h, b: (b, 0, h)),
        scratch_shapes=[pltpu.VMEM((hs, 2, NA_KB, NA_QB), F32), pltpu.VMEM((hs, 2, 1, NA_QB), F32)],
        compiler_params=_params("arbitrary", "arbitrary"),
        name="na",
    )(q, k, vt, tab)


def _mla_kernel(q_ref, k_ref, vt_ref, o_ref, st_ref):
    n_chunks = k_ref.shape[2] // MLA_KC
    n_tiles = q_ref.shape[2] // MLA_TQ

    def pass1(j, slot):
        q = q_ref[0, 0, j * MLA_TQ:(j + 1) * MLA_TQ, :]
        m = None
        for c in range(n_chunks):
            ks = slice(c * MLA_KC, (c + 1) * MLA_KC)
            st = _dot_nt(k_ref[0, 0, ks, :], q)
            st_ref[slot, ks, :] = st
            mc = jnp.max(st, axis=0, keepdims=True)
            m = mc if c == 0 else jnp.maximum(m, mc)
        return m

    def pass2(j, slot, m):
        l = acc = None
        for c in range(n_chunks):
            ks = slice(c * MLA_KC, (c + 1) * MLA_KC)
            p = jnp.exp2(st_ref[slot, ks, :] - m)
            ps = jnp.sum(p, axis=0, keepdims=True)
            pv = _dot(vt_ref[0, 0, :, ks], p.astype(BF16))
            l = ps if c == 0 else l + ps
            acc = pv if c == 0 else acc + pv
        o_ref[0, j * MLA_TQ:(j + 1) * MLA_TQ, :] = (acc / l).T.astype(o_ref.dtype)

    m_prev = pass1(0, 0)
    for j in range(1, n_tiles):
        m_cur = pass1(j, j % 2)
        pass2(j - 1, (j - 1) % 2, m_prev)
        m_prev = m_cur
    pass2(n_tiles - 1, (n_tiles - 1) % 2, m_prev)


def _mla(q, k, vt):
    B, H, S, Dq = q.shape
    Dv = MLA_V_DIM
    tq = MLA_TQ * MLA_TILES
    return pl.pallas_call(
        _mla_kernel,
        out_shape=jax.ShapeDtypeStruct((B, S, H * Dv), BF16),
        grid=(B, H, S // tq),
        in_specs=[
            pl.BlockSpec((1, 1, tq, Dq), lambda b, h, i: (b, h, i, 0)),
            pl.BlockSpec((1, 1, S, Dq), lambda b, h, i: (b, h, 0, 0)),
            pl.BlockSpec((1, 1, Dv, S), lambda b, h, i: (b, h, 0, 0)),
        ],
        out_specs=pl.BlockSpec((1, tq, Dv), lambda b, h, i: (b, i, h)),
        scratch_shapes=[pltpu.VMEM((2, S, MLA_TQ), F32)],
        compiler_params=_params("arbitrary", "arbitrary", "arbitrary"),
        name="mla",
    )(q, k, vt)


def _layer_norm(z, g, b):
    mu = jnp.mean(z, axis=-1, keepdims=True)
    zc = z - mu
    var = jnp.mean(zc * zc, axis=-1, keepdims=True)
    return zc * lax.rsqrt(var + LN_EPS) * g + b


def _first_lane_where(cond, lane):
    return jnp.min(jnp.where(cond, lane, LANES), axis=-1, keepdims=True)


def _mix_kernel(ona_ref, omla_ref, x_ref, wo_ref, g_ref, b_ref, wr_ref, br_ref,
                x1_ref, ri_ref, gate_ref, cnt_ref, carry_ref):
    step = pl.program_id(0)

    @pl.when(step == 0)
    def _():
        carry_ref[...] = jnp.zeros_like(carry_ref)

    ts = MIX_SUB
    subs = [slice(j * ts, (j + 1) * ts) for j in range(x_ref.shape[0] // ts)]
    lane = lax.broadcasted_iota(jnp.int32, (ts, LANES), 1)
    g_mask = (lane >= N_EXPERTS) & (lane < N_EXPERTS + N_GROUPS)
    r_i = lax.broadcasted_iota(jnp.int32, (ts, ts), 0)
    c_i = lax.broadcasted_iota(jnp.int32, (ts, ts), 1)
    tri = jnp.where(c_i < r_i, 1.0, 0.0).astype(BF16)

    def project(rows):
        return (_dot(ona_ref[rows, :], wo_ref[0:NA_WIDTH, :])
                + _dot(omla_ref[rows, :], wo_ref[NA_WIDTH:, :]))

    mix = project(subs[0])
    for j, rows in enumerate(subs):
        nxt = project(subs[j + 1]) if j + 1 < len(subs) else None
        _norm_route(rows, mix, lane, g_mask, tri, x_ref, g_ref, b_ref, wr_ref, br_ref,
                    x1_ref, ri_ref, gate_ref, carry_ref)
        mix = nxt
    cnt_ref[...] = carry_ref[...]


def _norm_route(rows, mix, lane, g_mask, tri, x_ref, g_ref, b_ref, wr_ref, br_ref,
                x1_ref, ri_ref, gate_ref, carry_ref):
    x1 = _layer_norm(ALPHA * x_ref[rows, :] + mix, g_ref[...], b_ref[...])
    x1_ref[rows, :] = x1

    logit = _dot(x1.astype(BF16), wr_ref[...]) + br_ref[...]
    gl = jnp.where(g_mask, logit, NEG_BIG)
    gmax = jnp.max(gl, axis=-1, keepdims=True)
    gsum = jnp.sum(jnp.where(g_mask, jnp.exp(gl - gmax), 0.0), axis=-1, keepdims=True)
    g_val = 1.0 / gsum
    g_idx = _first_lane_where(gl == gmax, lane) - N_EXPERTS

    e_mask = (lane < N_EXPERTS) & ((lane // EXPERTS_PER_GROUP) == g_idx)
    el = jnp.where(e_mask, logit, NEG_BIG)
    emax = jnp.max(el, axis=-1, keepdims=True)
    esum = jnp.sum(jnp.where(e_mask, jnp.exp(el - emax), 0.0), axis=-1, keepdims=True)
    i1 = _first_lane_where(el == emax, lane)
    el2 = jnp.where(lane == i1, NEG_BIG, el)
    emax2 = jnp.max(el2, axis=-1, keepdims=True)
    i2 = _first_lane_where(el2 == emax2, lane)
    v1 = 1.0 / esum
    v2 = jnp.exp(emax2 - emax) / esum
    vsum = v1 + v2
    w1 = g_val * (v1 / vsum)
    w2 = g_val * (v2 / vsum)

    oh1 = lane == i1
    oh2 = lane == i2
    oh = jnp.where(oh1 | oh2, 1.0, 0.0)
    before = _dot(tri, oh.astype(BF16)) + carry_ref[...]
    rank1 = jnp.sum(jnp.where(oh1, before, 0.0), axis=-1, keepdims=True).astype(jnp.int32)
    rank2 = jnp.sum(jnp.where(oh2, before, 0.0), axis=-1, keepdims=True).astype(jnp.int32)
    carry_ref[...] = carry_ref[...] + jnp.sum(oh, axis=0, keepdims=True)

    ri = jnp.where(lane == 0, i1, jnp.where(lane == 1, i2,
                   jnp.where(lane == 2, rank1, jnp.where(lane == 3, rank2, 0))))
    ri_ref[:, rows] = ri.T[0:SUBLANES, :]
    gate_ref[rows, :] = jnp.where(lane == 0, w1, jnp.where(lane == 1, w2, 0.0))


def _mix_out(o_na, o_mla, x, wo, g, b, wr, br):
    N, D = x.shape
    tm = MIX_TM
    row = lambda i: (i, 0)
    return pl.pallas_call(
        _mix_kernel,
        out_shape=(
            jax.ShapeDtypeStruct((N, D), F32),
            jax.ShapeDtypeStruct((SUBLANES, N), jnp.int32),
            jax.ShapeDtypeStruct((N, LANES), F32),
            jax.ShapeDtypeStruct((1, LANES), F32),
        ),
        grid=(N // tm,),
        in_specs=[
            pl.BlockSpec((tm, NA_WIDTH), row),
            pl.BlockSpec((tm, NA_WIDTH), row),
            pl.BlockSpec((tm, D), row),
            _resident(wo.shape), _resident(g.shape), _resident(b.shape),
            _resident(wr.shape), _resident(br.shape),
        ],
        out_specs=(
            pl.BlockSpec((tm, D), row),
            pl.BlockSpec((SUBLANES, tm), lambda i: (0, i)),
            pl.BlockSpec((tm, LANES), row),
            pl.BlockSpec((1, LANES), lambda i: (0, 0)),
        ),
        scratch_shapes=[pltpu.VMEM((1, LANES), F32)],
        compiler_params=_params("arbitrary"),
        name="mix_out",
    )(o_na, o_mla, x, wo, g, b, wr, br)


def _pos_kernel(start_ref, ri_ref, pos_ref):
    eid = ri_ref[0:TOP_K_EXPERT, :]
    pos = ri_ref[TOP_K_EXPERT:2 * TOP_K_EXPERT, :]
    for e in range(N_EXPERTS):
        pos = pos + jnp.where(eid == e, start_ref[e], 0)
    pos_ref[...] = pos


def _positions(pad_starts, ri):
    n = ri.shape[1]
    return pl.pallas_call(
        _pos_kernel,
        out_shape=jax.ShapeDtypeStruct((TOP_K_EXPERT, n), jnp.int32),
        grid_spec=pltpu.PrefetchScalarGridSpec(
            num_scalar_prefetch=1,
            grid=(1,),
            in_specs=[pl.BlockSpec(ri.shape, lambda i, s: (0, 0))],
            out_specs=pl.BlockSpec((TOP_K_EXPERT, n), lambda i, s: (0, 0)),
        ),
        compiler_params=_params("arbitrary"),
        name="positions",
    )(pad_starts, ri)


def _row_copy(src, src_row, dst, dst_row, sem):
    return pltpu.make_async_copy(src.at[pl.ds(src_row, 1), :], dst.at[pl.ds(dst_row, 1), :], sem)


_PAD_CHUNKS = tuple(1 << b for b in reversed(range(3, MOE_TB.bit_length() - 1)))


def _dispatch_kernel(pos_ref, fill_ref, x_ref, xs_ref, zero_ref, sem, zsem):
    tm = x_ref.shape[0]
    n_tok = pl.num_programs(0) * tm
    base = pl.program_id(0) * tm

    def pad_copies(fn):
        def per_expert(e, carry):
            start = fill_ref[e]
            n = fill_ref[N_EXPERTS + e]
            head = (-start) & (SUBLANES - 1)
            for r in range(SUBLANES - 1):
                @pl.when(r < head)
                def _(r=r):
                    fn(_row_copy(zero_ref, 0, xs_ref, start + r, zsem))
            off = start + head
            rem = n - head
            for c in _PAD_CHUNKS:
                hit = (rem & c) != 0

                @pl.when(hit)
                def _(off=off, c=c):
                    fn(pltpu.make_async_copy(zero_ref.at[pl.ds(0, c), :],
                                             xs_ref.at[pl.ds(pl.multiple_of(off, SUBLANES), c), :],
                                             zsem))
                off = off + jnp.where(hit, c, 0)
            return carry
        lax.fori_loop(0, N_EXPERTS, per_expert, 0)

        zrows = zero_ref.shape[0]

        def tail(t, carry):
            row = pl.multiple_of(fill_ref[2 * N_EXPERTS] + t * zrows, zrows)
            fn(pltpu.make_async_copy(zero_ref, xs_ref.at[pl.ds(row, zrows), :], zsem))
            return carry
        lax.fori_loop(0, (xs_ref.shape[0] - fill_ref[2 * N_EXPERTS]) // zrows, tail, 0)

    @pl.when(pl.program_id(0) == 0)
    def _():
        zero_ref[...] = jnp.zeros_like(zero_ref)
        pad_copies(lambda d: d.start())

    def issue(j, carry):
        for kk in range(TOP_K_EXPERT):
            _row_copy(x_ref, j, xs_ref, pos_ref[kk * n_tok + base + j], sem).start()
        return carry

    lax.fori_loop(0, tm, issue, 0, unroll=8)

    @pl.when(pl.program_id(0) == 0)
    def _():
        pad_copies(lambda d: d.wait())

    for kk in range(TOP_K_EXPERT):
        pltpu.make_async_copy(x_ref, xs_ref.at[pl.ds(0, tm), :], sem).wait()


def _dispatch(pos, fill, x1, n_rows):
    N, C = x1.shape
    tm = DISP_TM
    return pl.pallas_call(
        _dispatch_kernel,
        out_shape=jax.ShapeDtypeStruct((n_rows, C), x1.dtype),
        grid_spec=pltpu.PrefetchScalarGridSpec(
            num_scalar_prefetch=2,
            grid=(N // tm,),
            in_specs=[pl.BlockSpec((tm, C), lambda i, pos, fill: (i, 0))],
            out_specs=pl.BlockSpec(memory_space=pl.ANY),
            scratch_shapes=[pltpu.VMEM((_PAD_CHUNKS[0], C), x1.dtype),
                            pltpu.SemaphoreType.DMA, pltpu.SemaphoreType.DMA],
        ),
        compiler_params=_params("arbitrary"),
        name="dispatch",
    )(pos, fill, x1)


def _experts_kernel(be_ref, nxt_ref, nb_ref, xs_ref, wg_hbm, wu_hbm, wd_hbm, ys_ref,
                    wg_f, wu_f, wd_f, wg_s, wu_s, wd_s, sem):
    i = pl.program_id(0)
    active = i < nb_ref[0]
    e = be_ref[i]
    fresh = active & ((i == 0) | (e != be_ref[jnp.maximum(i - 1, 0)]))

    def fetch(ex):
        return (pltpu.make_async_copy(wg_hbm.at[ex], wg_f, sem.at[0]),
                pltpu.make_async_copy(wu_hbm.at[ex], wu_f, sem.at[1]),
                pltpu.make_async_copy(wd_hbm.at[ex], wd_f, sem.at[2]))

    @pl.when(i == 0)
    def _():
        for d in fetch(e):
            d.start()

    @pl.when(fresh)
    def _():
        for d in fetch(e):
            d.wait()
        wg_s[...] = wg_f[...].astype(BF16)
        wu_s[...] = wu_f[...].astype(BF16)
        wd_s[...] = wd_f[...].astype(BF16)
        nx = nxt_ref[i]

        @pl.when(nx >= 0)
        def _():
            for d in fetch(nx):
                d.start()

    @pl.when(active)
    def _():
        xb = xs_ref[...].astype(BF16)
        gp = _dot(xb, wg_s[...])
        up = _dot(xb, wu_s[...])
        hdn = (gp * jax.nn.sigmoid(gp) * up).astype(BF16)
        ys_ref[...] = _dot(hdn, wd_s[...])

    @pl.when(jnp.logical_not(active))
    def _():
        ys_ref[...] = jnp.zeros_like(ys_ref)


def _experts(block_e, next_e, n_used, xs, w_gate, w_up, w_down):
    P, C = xs.shape
    E, D, F = w_gate.shape
    tb = MOE_TB
    n_blocks = P // tb
    hbm = pl.BlockSpec(memory_space=pl.ANY)
    return pl.pallas_call(
        _experts_kernel,
        out_shape=jax.ShapeDtypeStruct((P, C), F32),
        grid_spec=pltpu.PrefetchScalarGridSpec(
            num_scalar_prefetch=3,
            grid=(n_blocks,),
            in_specs=[
                pl.BlockSpec((tb, C), lambda i, be, nx, nb: (jnp.minimum(i, nb[0] - 1), 0)),
                hbm, hbm, hbm,
            ],
            out_specs=pl.BlockSpec((tb, C), lambda i, be, nx, nb: (i, 0)),
            scratch_shapes=[pltpu.VMEM((D, F), F32), pltpu.VMEM((D, F), F32), pltpu.VMEM((F, D), F32),
                            pltpu.VMEM((D, F), BF16), pltpu.VMEM((D, F), BF16),
                            pltpu.VMEM((F, D), BF16), pltpu.SemaphoreType.DMA((3,))],
        ),
        compiler_params=_params("arbitrary"),
        name="experts",
    )(block_e, next_e, n_used, xs, w_gate, w_up, w_down)


def _final_kernel(pos_ref, x1_ref, gate_ref, ys_ref, p_ref, g_ref, b_ref, wpg_ref, wpe_ref,
                  gp_ref, o_ref, buf, sem):
    tm = x1_ref.shape[0]
    i = pl.program_id(0)
    last = pl.num_programs(0) - 1
    slot = i % 2

    def issue(tile, s, unroll):
        n_tok = pl.num_programs(0) * tm
        base = tile * tm

        def body(j, carry):
            for kk in range(TOP_K_EXPERT):
                _row_copy(ys_ref, pos_ref[kk * n_tok + base + j],
                          buf.at[s, kk], j, sem.at[s]).start()
            return carry
        lax.fori_loop(0, tm, body, 0, unroll=unroll)

    def drain(s):
        for kk in range(TOP_K_EXPERT):
            pltpu.make_async_copy(ys_ref.at[pl.ds(0, tm), :], buf.at[s, kk], sem.at[s]).wait()

    @pl.when(i == 0)
    def _():
        issue(0, 0, 8)

    drain(slot)
    issue(jnp.minimum(i + 1, last), 1 - slot, True)

    gates = gate_ref[...]
    g1 = gates[:, 0:1]
    g2 = gates[:, 1:2]
    ffn = buf[slot, 0] * g1 + buf[slot, 1] * g2
    x2 = _layer_norm(ALPHA * x1_ref[...] + ffn, g_ref[...], b_ref[...])

    gate = jax.nn.sigmoid(_dot(x2.astype(BF16), wpg_ref[...]))
    e = _dot(p_ref[...].astype(BF16), wpe_ref[...])
    t = e * gate
    ple = t * lax.rsqrt(jnp.mean(t * t, axis=-1, keepdims=True) + RMS_EPS) * gp_ref[...]
    o_ref[...] = x2 + ple

    @pl.when(i == last)
    def _():
        drain(1 - slot)


def _final(pos, x1, gates, ys, p, g, b, wpg, wpe, gp):
    N, D = x1.shape
    tm = FIN_TM
    row = lambda i, pos: (i, 0)
    return pl.pallas_call(
        _final_kernel,
        out_shape=jax.ShapeDtypeStruct((N, D), F32),
        grid_spec=pltpu.PrefetchScalarGridSpec(
            num_scalar_prefetch=1,
            grid=(N // tm,),
            in_specs=[
                pl.BlockSpec((tm, D), row),
                pl.BlockSpec((tm, LANES), row),
                pl.BlockSpec(memory_space=pl.ANY),
                pl.BlockSpec((tm, PLE_DIM), row),
                _resident(g.shape), _resident(b.shape), _resident(wpg.shape),
                _resident(wpe.shape), _resident(gp.shape),
            ],
            out_specs=pl.BlockSpec((tm, D), row),
            scratch_shapes=[pltpu.VMEM((2, TOP_K_EXPERT, tm, D), F32),
                            pltpu.SemaphoreType.DMA((2,))],
        ),
        compiler_params=_params("arbitrary"),
        name="final",
    )(pos, x1, gates, ys, p, g, b, wpg, wpe, gp)


def _rope_table(positions):
    inv_freq = 1.0 / (ROPE_THETA ** (jnp.arange(0, MLA_ROPE_DIM, 2, dtype=F32) / MLA_ROPE_DIM))
    half = MLA_ROPE_DIM // 2
    phase = jnp.concatenate([jnp.zeros((2 * half,), F32), jnp.full((2 * half,), np.pi / 2, F32)])
    return jnp.cos(positions.astype(F32)[..., None] * jnp.tile(inv_freq, 4) - phase)


def _layer(x, p_i, rope_t, w_in, rpb, q_norm_g, kv_norm_g, w_uq, w_uk, w_uv, w_o, ln1_g, ln1_b,
           w_group, b_group, w_router, b_router, w_gate, w_up, w_down, ln2_g, ln2_b,
           w_ple, w_ple_gate, ple_norm_g):
    B, S, D = x.shape
    N = B * S
    s3 = 3 * NA_WIDTH
    half = MLA_ROPE_DIM // 2

    wqk = w_in[:, :2 * NA_WIDTH].astype(BF16)
    wc = w_in[:, s3:].astype(BF16)
    wvt = w_in[:, 2 * NA_WIDTH:s3].T.astype(BF16)
    uq = w_uq.reshape(Q_LORA_RANK, MLA_HEADS, MLA_QK_DIM)
    nope, x1c, x2c = (uq[..., :MLA_NOPE_DIM], uq[..., MLA_NOPE_DIM:MLA_NOPE_DIM + half],
                      uq[..., MLA_NOPE_DIM + half:])
    wuq = jnp.concatenate([nope, x1c, x2c, x2c, x1c], axis=-1).reshape(Q_LORA_RANK, -1).astype(BF16)
    wuk = w_uk.astype(BF16)
    wuvt = w_uv.T.astype(BF16)
    row = lambda v: v.reshape(1, -1).astype(F32)

    q_na, k_na, v_nat, q_m, k_m, v_mt = _proj(x, rope_t, wqk, wvt, wc, wuq, wuk, wuvt,
                                               row(q_norm_g), row(kv_norm_g))
    o_na = _na(q_na, k_na, v_nat, _na_bias_table(rpb))
    o_mla = _mla(q_m, k_m, v_mt)

    pad = LANES - N_EXPERTS - N_GROUPS
    wr = jnp.concatenate([w_router, w_group, jnp.zeros((D, pad), F32)], axis=1).astype(BF16)
    br = jnp.concatenate([b_router.reshape(-1), b_group, jnp.zeros((pad,), F32)]).reshape(1, -1)
    x1, ri, gates, cnt = _mix_out(o_na.reshape(N, -1), o_mla.reshape(N, -1), x.reshape(N, D),
                                       w_o.astype(BF16), row(ln1_g), row(ln1_b), wr, br)

    tb = MOE_TB
    counts = cnt[0, :N_EXPERTS].astype(jnp.int32)
    padded = ((counts + tb - 1) // tb) * tb
    pad_ends = jnp.cumsum(padded)
    pad_starts = pad_ends - padded
    pos = _positions(pad_starts.astype(jnp.int32), ri).reshape(-1)
    fill = jnp.concatenate([pad_starts + counts, padded - counts, pad_ends[-1:]]).astype(jnp.int32)
    n_blocks = (N * TOP_K_EXPERT + N_EXPERTS * (tb - 1) + tb - 1) // tb
    blk_row = jnp.arange(n_blocks, dtype=jnp.int32) * tb
    block_e = jnp.minimum(jnp.sum(pad_ends[None, :] <= blk_row[:, None], axis=1),
                          N_EXPERTS - 1).astype(jnp.int32)
    n_used = (pad_ends[-1:] // tb).astype(jnp.int32)
    nxt_blk = pad_ends[block_e] // tb
    next_e = jnp.where(nxt_blk < n_used[0], block_e[jnp.minimum(nxt_blk, n_blocks - 1)],
                       -1).astype(jnp.int32)

    xs = _dispatch(pos, fill, x1, n_blocks * tb)
    ys = _experts(block_e, next_e, n_used, xs, w_gate, w_up, w_down)
    out = _final(pos, x1, gates, ys, p_i.reshape(N, -1), row(ln2_g), row(ln2_b),
                 w_ple_gate.astype(BF16), w_ple.astype(BF16), row(ple_norm_g))
    return out.reshape(B, S, D)


def kernel(x, p, positions, w_in, rpb, q_norm_g, kv_norm_g, w_uq, w_uk, w_uv, w_o, ln1_g, ln1_b,
           w_group, b_group, w_router, b_router, w_gate, w_up, w_down, ln2_g, ln2_b,
           w_ple, w_ple_gate, ple_norm_g):
    rope_t = _rope_table(positions)
    for i in range(DEPTH):
        x = _layer(x, p[i], rope_t, w_in[i], rpb[i], q_norm_g[i], kv_norm_g[i], w_uq[i], w_uk[i],
                   w_uv[i], w_o[i], ln1_g[i], ln1_b[i], w_group[i], b_group[i], w_router[i],
                   b_router[i], w_gate[i], w_up[i], w_down[i], ln2_g[i], ln2_b[i],
                   w_ple[i], w_ple_gate[i], ple_norm_g[i])
    return x
```
